```python
import math
import jax
import jax.numpy as jnp
from jax import lax
import numpy as np


D_MODEL = 1024
BATCH = 8
SEQ = 4096
DEPTH = 4

HEAD_DIM = 64
RMS_EPS = 1e-6
ROPE_THETA = 10000.0
Q_BLOCK = 128
FOX_HEADS = 8
FOX_WIDTH = FOX_HEADS * HEAD_DIM
SSD_HEADS = 8
SSD_HEAD_DIM = 64
SSD_INNER = SSD_HEADS * SSD_HEAD_DIM
SSD_GROUPS = 2
SSD_STATE = 128
SSD_CONV = 4
SSD_CHUNK = 128
SSD_CONV_DIM = SSD_INNER + 2 * SSD_GROUPS * SSD_STATE
EVEN_SIZES = (FOX_WIDTH, FOX_WIDTH, FOX_WIDTH, FOX_HEADS, SSD_INNER, SSD_CONV_DIM, SSD_HEADS)
EVEN_IN = sum(EVEN_SIZES)
EVEN_SPLITS = tuple(int(v) for v in np.cumsum(EVEN_SIZES)[:-1])
EVEN_MIX = FOX_WIDTH + SSD_INNER
DIL_HEADS = 16
DIL_WIDTH = DIL_HEADS * HEAD_DIM
DIL_PATTERNS = ((128, 1), (512, 4), (2048, 16))
DIL_BLOCK = 128
FFN_DIM = 2816
N_EXPERTS = 8
TOP_K = 2
EXPERT_DIM = 3584
N_EVEN = (DEPTH + 1) // 2
N_ODD = DEPTH // 2

kernel_name = "hybrid_fox_ssd_dilated_moe"


def rms_norm(x, g):
    x32 = x.astype(jnp.float32)
    y = x32 * lax.rsqrt(jnp.mean(x32 * x32, axis=-1, keepdims=True) + RMS_EPS)
    return (y * g.astype(jnp.float32)).astype(x.dtype)


def rope(x):
    S, Dh = x.shape[1], x.shape[3]
    half = Dh // 2
    inv_freq = ROPE_THETA ** (-jnp.arange(half, dtype=jnp.float32) / half)
    ang = jnp.arange(S, dtype=jnp.float32)[:, None] * inv_freq[None, :]
    cos = jnp.cos(ang)[None, :, None, :]
    sin = jnp.sin(ang)[None, :, None, :]
    x32 = x.astype(jnp.float32)
    x1, x2 = x32[..., :half], x32[..., half:]
    return jnp.concatenate([x1 * cos - x2 * sin, x2 * cos + x1 * sin], axis=-1).astype(x.dtype)


def forgetting_attention(q, k, v, log_f):
    B, S, H, Dh = q.shape
    nq = S // Q_BLOCK
    c = jnp.cumsum(log_f, axis=1)
    c_keys = jnp.transpose(c, (0, 2, 1))
    q_blocks = jnp.moveaxis(q.reshape(B, nq, Q_BLOCK, H, Dh), 1, 0)
    c_blocks = jnp.moveaxis(c.reshape(B, nq, Q_BLOCK, H), 1, 0)
    q_pos = jnp.arange(S).reshape(nq, Q_BLOCK)
    k_pos = jnp.arange(S)
    scale = HEAD_DIM ** -0.5

    def block(args):
        q_i, c_i, p_i = args
        s = jnp.einsum('bqhd,bkhd->bhqk', q_i, k).astype(jnp.float32) * scale
        s = s + jnp.transpose(c_i, (0, 2, 1))[..., None] - c_keys[:, :, None, :]
        causal = p_i[:, None] >= k_pos[None, :]
        s = jnp.where(causal[None, None], s, -jnp.inf)
        p = jax.nn.softmax(s, axis=-1)
        return jnp.einsum('bhqk,bkhd->bqhd', p.astype(v.dtype), v)

    o = lax.map(block, (q_blocks, c_blocks, q_pos))
    return jnp.moveaxis(o, 0, 1).reshape(B, S, H, Dh)


def dilated_branch(q, k, v, window, dilation):
    B, S, H, Dh = q.shape
    span = window // dilation
    unit = dilation * DIL_BLOCK
    s_pad = -(-S // unit) * unit
    n_sub = s_pad // dilation
    nb = n_sub // DIL_BLOCK

    def to_blocks(t):
        t = jnp.pad(t, ((0, 0), (0, s_pad - S), (0, 0), (0, 0)))
        t = jnp.transpose(t.reshape(B, n_sub, dilation, H, Dh), (2, 0, 1, 3, 4))
        return t.reshape(dilation, B, nb, DIL_BLOCK, H, Dh)

    def with_prev(t):
        prev = jnp.pad(t, ((0, 0), (0, 0), (1, 0), (0, 0), (0, 0), (0, 0)))[:, :, :-1]
        return jnp.concatenate([prev, t], axis=3)

    def items(t):
        t = jnp.transpose(t, (0, 2, 1, 3, 4, 5))
        return t.reshape((dilation * nb,) + t.shape[2:])

    qb = items(to_blocks(q))
    kb = items(with_prev(to_blocks(k)))
    vb = items(with_prev(to_blocks(v)))
    q_idx = jnp.tile(jnp.arange(nb)[:, None] * DIL_BLOCK + jnp.arange(DIL_BLOCK)[None, :], (dilation, 1))
    k_idx = jnp.tile(jnp.arange(nb)[:, None] * DIL_BLOCK + jnp.arange(-DIL_BLOCK, DIL_BLOCK)[None, :], (dilation, 1))
    scale = HEAD_DIM ** -0.5

    def block(args):
        q_i, k_i, v_i, qi, ki = args
        s = jnp.einsum('bqhd,bkhd->bhqk', q_i, k_i).astype(jnp.float32) * scale
        rel = qi[:, None] - ki[None, :]
        mask = (rel >= 0) & (rel <= span) & (ki[None, :] >= 0)
        s = jnp.where(mask[None, None], s, -jnp.inf)
        m = jnp.max(s, axis=-1, keepdims=True)
        e = jnp.exp(s - m)
        den = jnp.sum(e, axis=-1, keepdims=True)
        o = jnp.einsum('bhqk,bkhd->bqhd', (e / den).astype(v_i.dtype), v_i)
        lse = jnp.transpose(m[..., 0] + jnp.log(den[..., 0]), (0, 2, 1))
        return o, lse

    o, lse = lax.map(block, (qb, kb, vb, q_idx, k_idx))
    o = jnp.transpose(o.reshape(dilation, nb, B, DIL_BLOCK, H, Dh), (2, 1, 3, 0, 4, 5))
    lse = jnp.transpose(lse.reshape(dilation, nb, B, DIL_BLOCK, H), (2, 1, 3, 0, 4))
    return o.reshape(B, s_pad, H, Dh)[:, :S], lse.reshape(B, s_pad, H)[:, :S]


def dilated_mixer(h, w_qkv, qn, kn, w_out):
    B, S, _ = h.shape
    q, k, v = jnp.split(h @ w_qkv, 3, axis=-1)
    q = rope(rms_norm(q.reshape(B, S, DIL_HEADS, HEAD_DIM), qn))
    k = rope(rms_norm(k.reshape(B, S, DIL_HEADS, HEAD_DIM), kn))
    v = v.reshape(B, S, DIL_HEADS, HEAD_DIM)
    outs, lses = zip(*[dilated_branch(q, k, v, w, d) for (w, d) in DIL_PATTERNS])
    alpha = jax.nn.softmax(jnp.stack(lses, axis=0), axis=0)
    o = jnp.einsum('gbsh,gbshd->bshd', alpha, jnp.stack(outs, axis=0).astype(jnp.float32))
    return o.reshape(B, S, DIL_WIDTH).astype(h.dtype) @ w_out


def causal_depthwise_conv(x, w, b):
    out = lax.conv_general_dilated(
        x, w[:, None, :], window_strides=(1,), padding=[(SSD_CONV - 1, 0)],
        dimension_numbers=('NWC', 'WIO', 'NWC'), feature_group_count=x.shape[-1])
    return out + b


def ssd_chunked_scan(xs, dt, a, bm, cm, d_skip):
    f32 = jnp.float32
    B, S, H, P = xs.shape
    G, N = bm.shape[2], bm.shape[3]
    L = SSD_CHUNK
    nc = S // L
    rep = H // G
    xs = xs.astype(f32)
    bh = jnp.repeat(bm.astype(f32), rep, axis=2).reshape(B, nc, L, H, N)
    ch = jnp.repeat(cm.astype(f32), rep, axis=2).reshape(B, nc, L, H, N)
    x_c = xs.reshape(B, nc, L, H, P)
    dt_c = dt.reshape(B, nc, L, H)
    cum = jnp.cumsum(dt_c * a, axis=2)
    xdt = x_c * dt_c[..., None]
    causal = jnp.tril(jnp.ones((L, L), dtype=bool))[None, None, :, :, None]
    seg = jnp.exp(jnp.where(causal, cum[:, :, :, None, :] - cum[:, :, None, :, :], -jnp.inf))
    scores = jnp.einsum('bclhn,bcshn->bclsh', ch, bh) * seg
    y_diag = jnp.einsum('bclsh,bcshp->bclhp', scores, xdt)
    decay_to_end = jnp.exp(cum[:, :, -1:, :] - cum)
    states = jnp.einsum('bcshn,bcsh,bcshp->bchpn', bh, decay_to_end, xdt)
    chunk_decay = jnp.exp(cum[:, :, -1, :])

    def step(hstate, inp):
        st, dec = inp
        return dec[:, :, None, None] * hstate + st, hstate

    _, h_prev = lax.scan(step, jnp.zeros((B, H, P, N), f32),
                         (jnp.moveaxis(states, 1, 0), jnp.moveaxis(chunk_decay, 1, 0)))
    h_prev = jnp.moveaxis(h_prev, 0, 1)
    y_off = jnp.einsum('bclhn,bchpn,bclh->bclhp', ch, h_prev, jnp.exp(cum))
    return (y_diag + y_off).reshape(B, S, H, P) + xs * d_skip.astype(f32)[:, None]


def fox_ssd_mixer(h, w_in, fox_qn, fox_kn, fox_fbias, conv_w, conv_b, dt_bias, a_log, d_skip, ssd_norm, w_out):
    f32 = jnp.float32
    B, S, _ = h.shape
    q, k, v, f, z, xbc, dt = jnp.split(h @ w_in, EVEN_SPLITS, axis=-1)
    q = rms_norm(q.reshape(B, S, FOX_HEADS, HEAD_DIM), fox_qn)
    k = rms_norm(k.reshape(B, S, FOX_HEADS, HEAD_DIM), fox_kn)
    v = v.reshape(B, S, FOX_HEADS, HEAD_DIM)
    log_f = jax.nn.log_sigmoid(f.astype(f32) + fox_fbias.astype(f32))
    o_fox = forgetting_attention(q, k, v, log_f).reshape(B, S, FOX_WIDTH)
    xbc = jax.nn.silu(causal_depthwise_conv(xbc, conv_w, conv_b))
    xs, bm, cm = jnp.split(xbc, (SSD_INNER, SSD_INNER + SSD_GROUPS * SSD_STATE), axis=-1)
    dt = jax.nn.softplus(dt.astype(f32) + dt_bias.astype(f32))
    a = -jnp.exp(a_log.astype(f32))
    y = ssd_chunked_scan(xs.reshape(B, S, SSD_HEADS, SSD_HEAD_DIM), dt, a,
                         bm.reshape(B, S, SSD_GROUPS, SSD_STATE), cm.reshape(B, S, SSD_GROUPS, SSD_STATE), d_skip)
    y = y.reshape(B, S, SSD_INNER) * jax.nn.silu(z.astype(f32))
    y = rms_norm(y.reshape(B, S, SSD_GROUPS, SSD_INNER // SSD_GROUPS),
                 ssd_norm.reshape(SSD_GROUPS, SSD_INNER // SSD_GROUPS)).reshape(B, S, SSD_INNER)
    return jnp.concatenate([o_fox, y.astype(h.dtype)], axis=-1) @ w_out


def swiglu(h, w13, w2):
    g, u = jnp.split(h @ w13, 2, axis=-1)
    return (jax.nn.silu(g) * u) @ w2


def moe_swiglu(h, router, w13, w2):
    logits = (h @ router).astype(jnp.float32)
    top_val, top_idx = lax.top_k(logits, TOP_K)
    gates = jax.nn.softmax(top_val, axis=-1)
    combine = jnp.einsum('bske,bsk->bse', jax.nn.one_hot(top_idx, N_EXPERTS, dtype=jnp.float32), gates).astype(h.dtype)
    out = jnp.zeros_like(h)
    for e in range(N_EXPERTS):
        out = out + combine[..., e:e + 1] * swiglu(h, w13[e], w2[e])
    return out


def setup_inputs(seed: int = 0) -> dict:
    key = jax.random.key(seed)
    ks = jax.random.split(key, 32)
    f32 = jnp.float32
    ne, no = N_EVEN, N_ODD

    def dense(k, shape, fan_in):
        return jax.random.normal(k, shape, f32) * fan_in ** -0.5

    def gain(k, shape):
        return 1.0 + 0.05 * jax.random.normal(k, shape, f32)

    dt0 = jnp.exp(jax.random.uniform(ks[10], (ne, SSD_HEADS), f32, math.log(1e-3), math.log(1e-1)))
    return {
        'x': jax.random.normal(ks[0], (BATCH, SEQ, D_MODEL), f32),
        'e_norm1': gain(ks[1], (ne, D_MODEL)),
        'e_w_in': dense(ks[2], (ne, D_MODEL, EVEN_IN), D_MODEL),
        'e_fox_qn': gain(ks[3], (ne, FOX_HEADS, HEAD_DIM)),
        'e_fox_kn': gain(ks[4], (ne, FOX_HEADS, HEAD_DIM)),
        'e_fox_fbias': 3.0 + 0.5 * jax.random.normal(ks[5], (ne, FOX_HEADS), f32),
        'e_conv_w': 0.5 * jax.random.normal(ks[6], (ne, SSD_CONV, SSD_CONV_DIM), f32),
        'e_conv_b': 0.02 * jax.random.normal(ks[7], (ne, SSD_CONV_DIM), f32),
        'e_dt_bias': dt0 + jnp.log(-jnp.expm1(-dt0)),
        'e_a_log': jnp.log(jax.random.uniform(ks[8], (ne, SSD_HEADS), f32, 1.0, 16.0)),
        'e_d_skip': gain(ks[9], (ne, SSD_HEADS)),
        'e_ssd_norm': gain(ks[11], (ne, SSD_INNER)),
        'e_w_out': dense(ks[12], (ne, EVEN_MIX, D_MODEL), EVEN_MIX),
        'e_norm2': gain(ks[13], (ne, D_MODEL)),
        'e_ffn_w13': dense(ks[14], (ne, D_MODEL, 2 * FFN_DIM), D_MODEL),
        'e_ffn_w2': dense(ks[15], (ne, FFN_DIM, D_MODEL), FFN_DIM),
        'o_norm1': gain(ks[16], (no, D_MODEL)),
        'o_w_qkv': dense(ks[17], (no, D_MODEL, 3 * DIL_WIDTH), D_MODEL),
        'o_qn': gain(ks[18], (no, DIL_HEADS, HEAD_DIM)),
        'o_kn': gain(ks[19], (no, DIL_HEADS, HEAD_DIM)),
        'o_w_out': dense(ks[20], (no, DIL_WIDTH, D_MODEL), DIL_WIDTH),
        'o_norm2': gain(ks[21], (no, D_MODEL)),
        'o_router': dense(ks[22], (no, D_MODEL, N_EXPERTS), D_MODEL),
        'o_moe_w13': dense(ks[23], (no, N_EXPERTS, D_MODEL, 2 * EXPERT_DIM), D_MODEL),
        'o_moe_w2': dense(ks[24], (no, N_EXPERTS, EXPERT_DIM, D_MODEL), EXPERT_DIM),
    }


def reference(x, e_norm1, e_w_in, e_fox_qn, e_fox_kn, e_fox_fbias, e_conv_w, e_conv_b, e_dt_bias,
              e_a_log, e_d_skip, e_ssd_norm, e_w_out, e_norm2, e_ffn_w13, e_ffn_w2,
              o_norm1, o_w_qkv, o_qn, o_kn, o_w_out, o_norm2, o_router, o_moe_w13, o_moe_w2):
    for i in range(DEPTH):
        j = i // 2
        if i % 2 == 0:
            x = x + fox_ssd_mixer(rms_norm(x, e_norm1[j]), e_w_in[j], e_fox_qn[j], e_fox_kn[j], e_fox_fbias[j],
                                  e_conv_w[j], e_conv_b[j], e_dt_bias[j], e_a_log[j], e_d_skip[j],
                                  e_ssd_norm[j], e_w_out[j])
            x = x + swiglu(rms_norm(x, e_norm2[j]), e_ffn_w13[j], e_ffn_w2[j])
        else:
            x = x + dilated_mixer(rms_norm(x, o_norm1[j]), o_w_qkv[j], o_qn[j], o_kn[j], o_w_out[j])
            x = x + moe_swiglu(rms_norm(x, o_norm2[j]), o_router[j], o_moe_w13[j], o_moe_w2[j])
    return x
```

```python
import functools
import math

import jax
import jax.numpy as jnp
import numpy as np
from jax import lax
from jax.experimental import pallas as pl
from jax.experimental.pallas import tpu as pltpu

F32 = jnp.float32
BF16 = jnp.bfloat16

D_MODEL = 1024
HEAD_DIM = 64
RMS_EPS = 1e-6
ROPE_THETA = 10000.0
FOX_HEADS = 8
FOX_WIDTH = FOX_HEADS * HEAD_DIM
SSD_HEADS = 8
SSD_INNER = 512
SSD_GROUPS = 2
SSD_STATE = 128
SSD_CONV = 4
SSD_CHUNK = 128
SSD_CONV_DIM = SSD_INNER + 2 * SSD_GROUPS * SSD_STATE
DIL_HEADS = 16
DIL_WIDTH = DIL_HEADS * HEAD_DIM
DIL_PATTERNS = ((128, 1), (512, 4), (2048, 16))
DIL_BLOCK = 128
FFN_DIM = 2816
N_EXPERTS = 8
EXPERT_DIM = 3584

LANES = 128
NEG_BIG = -1e30
VMEM_LIMIT = 56 * 1024 * 1024

ROW_TILE = 512
ATTN_TILE = 512
FFN_CHUNK = 256
MOE_CHUNK = 512
MOE_TILE = 512


def _cparams(sem):
    return pltpu.CompilerParams(dimension_semantics=sem, vmem_limit_bytes=VMEM_LIMIT)


def _softplus_parts(x):
    return jnp.log(1.0 + jnp.exp(-jnp.abs(x)))


def _split_bf16(a, parts):
    out = []
    r = a
    for _ in range(parts):
        p = r.astype(BF16)
        out.append(p)
        r = r - p.astype(F32)
    return out


def _dot(a, b):
    return jnp.dot(a, b, preferred_element_type=F32)


def _dot_nt(a, b):
    return lax.dot_general(a, b, (((1,), (1,)), ((), ())), preferred_element_type=F32)


def _dot_exact_rhs(a_f32, b_bf16, parts):
    acc = None
    for p in _split_bf16(a_f32, parts):
        t = _dot(p, b_bf16)
        acc = t if acc is None else acc + t
    return acc


def _dot_exact_lhs(a_bf16, b_f32, parts):
    acc = None
    for p in _split_bf16(b_f32, parts):
        t = _dot(a_bf16, p)
        acc = t if acc is None else acc + t
    return acc


def _rms(x, g):
    return x * lax.rsqrt(jnp.mean(x * x, axis=-1, keepdims=True) + RMS_EPS) * g


def _head_norm(x, hm, gain):
    ms = _dot((x * x).astype(BF16), hm)
    return x * lax.rsqrt(ms + RMS_EPS) * gain


def _silu(x):
    return x * (1.0 / (1.0 + jnp.exp(-x)))


def _even_in_kernel(tiles_per_seq, x_ref, g_ref, wqkv_ref, wzx_ref, wfd_ref, qg_ref, kg_ref,
                    fdb_ref, hm_ref, tril_ref, q_ref, k_ref, v_ref, z_ref, xbc_ref, fd_ref,
                    carry_ref):
    i = pl.program_id(0)

    @pl.when(i % tiles_per_seq == 0)
    def _():
        carry_ref[...] = jnp.zeros_like(carry_ref)

    h = _rms(x_ref[...], g_ref[...]).astype(BF16)
    qkv = _dot(h, wqkv_ref[...])
    hm = hm_ref[...]
    w = FOX_WIDTH
    q_ref[...] = (_head_norm(qkv[:, :w], hm, qg_ref[...]) * (HEAD_DIM ** -0.5)).astype(BF16)
    k_ref[...] = _head_norm(qkv[:, w:2 * w], hm, kg_ref[...]).astype(BF16)
    v_ref[...] = qkv[:, 2 * w:].astype(BF16)
    zx = _dot(h, wzx_ref[...])
    z_ref[...] = zx[:, :SSD_INNER].astype(BF16)
    xbc_ref[...] = zx[:, SSD_INNER:]
    fd = _dot(h, wfd_ref[...]) + fdb_ref[...]
    t = _softplus_parts(fd)
    log_f = jnp.minimum(fd, 0.0) - t
    dt = jnp.maximum(fd, 0.0) + t
    c = _dot_exact_lhs(tril_ref[...], log_f, 2) + carry_ref[...]
    carry_ref[...] = c[-1:, :]
    lane = lax.broadcasted_iota(jnp.int32, fd.shape, 1)
    fd_ref[...] = jnp.where(lane < FOX_HEADS, c, dt)


def _even_in_proj(x2d, seq, g, wqkv, wzx, wfd, qg, kg, fdb, hm, tril):
    t_rows = x2d.shape[0]
    tm = min(ROW_TILE, seq)
    n = t_rows // tm
    row = lambda w: pl.BlockSpec((tm, w), lambda i: (i, 0))
    full = lambda a: pl.BlockSpec(a.shape, lambda i: (0,) * a.ndim)
    return pl.pallas_call(
        functools.partial(_even_in_kernel, seq // tm),
        grid=(n,),
        in_specs=[row(D_MODEL), full(g), full(wqkv), full(wzx), full(wfd), full(qg), full(kg),
                  full(fdb), full(hm), full(tril)],
        out_specs=[row(FOX_WIDTH), row(FOX_WIDTH), row(FOX_WIDTH), row(SSD_INNER),
                   row(SSD_CONV_DIM), row(LANES)],
        out_shape=[jax.ShapeDtypeStruct((t_rows, FOX_WIDTH), BF16)] * 3
        + [jax.ShapeDtypeStruct((t_rows, SSD_INNER), BF16),
           jax.ShapeDtypeStruct((t_rows, SSD_CONV_DIM), F32),
           jax.ShapeDtypeStruct((t_rows, LANES), F32)],
        scratch_shapes=[pltpu.VMEM((1, LANES), F32)],
        compiler_params=_cparams(("arbitrary",)),
        name="even_in_proj",
    )(x2d, g, wqkv, wzx, wfd, qg, kg, fdb, hm, tril)


def _fox_kernel(q_ref, k_ref, v_ref, cq_ref, ck_ref, o_ref, m_ref, l_ref, acc_ref):
    qi = pl.program_id(2)
    ki = pl.program_id(3)
    tq = q_ref.shape[0]
    tk = k_ref.shape[0]

    @pl.when(ki == 0)
    def _():
        m_ref[...] = jnp.full_like(m_ref, NEG_BIG)
        l_ref[...] = jnp.zeros_like(l_ref)
        acc_ref[...] = jnp.zeros_like(acc_ref)

    @pl.when(ki <= qi)
    def _():
        q2 = q_ref[...]
        k2 = k_ref[...]
        v2 = v_ref[...]
        lane = lax.broadcasted_iota(jnp.int32, q2.shape, 1)
        rows = qi * tq + lax.broadcasted_iota(jnp.int32, (tq, tk), 0)
        cols = ki * tk + lax.broadcasted_iota(jnp.int32, (tq, tk), 1)
        causal = rows >= cols
        for h in range(2):
            qm = jnp.where((lane >= h * HEAD_DIM) & (lane < (h + 1) * HEAD_DIM), q2, jnp.zeros_like(q2))
            s = _dot_nt(qm, k2)
            s = s + cq_ref[:, h:h + 1] - ck_ref[h:h + 1, :]
            s = jnp.where(causal, s, NEG_BIG)
            m_prev = m_ref[h]
            m_new = jnp.maximum(m_prev, jnp.max(s, axis=-1, keepdims=True))
            alpha = jnp.exp(m_prev - m_new)
            p = jnp.exp(s - m_new)
            l_ref[h] = alpha * l_ref[h] + jnp.sum(p, axis=-1, keepdims=True)
            acc_ref[h] = alpha * acc_ref[h] + _dot(p.astype(BF16), v2)
            m_ref[h] = m_new

    @pl.when(ki == qi)
    def _():
        lane = lax.broadcasted_iota(jnp.int32, (tq, LANES), 1)
        o0 = acc_ref[0] / l_ref[0]
        o1 = acc_ref[1] / l_ref[1]
        o_ref[...] = jnp.where(lane < HEAD_DIM, o0, o1).astype(o_ref.dtype)


def _fox_attention(q, k, v, cq, ck):
    b, s, _ = q.shape
    t = min(ATTN_TILE, s)
    n = s // t
    pairs = FOX_HEADS // 2
    qspec = pl.BlockSpec((None, t, LANES), lambda bi, p, i, j: (bi, i, p))
    kspec = pl.BlockSpec((None, t, LANES), lambda bi, p, i, j: (bi, jnp.minimum(j, i), p))
    return pl.pallas_call(
        _fox_kernel,
        grid=(b, pairs, n, n),
        in_specs=[qspec, kspec, kspec,
                  pl.BlockSpec((None, None, t, 2), lambda bi, p, i, j: (bi, p, i, 0)),
                  pl.BlockSpec((None, None, 2, t), lambda bi, p, i, j: (bi, p, 0, jnp.minimum(j, i)))],
        out_specs=qspec,
        out_shape=jax.ShapeDtypeStruct((b, s, FOX_WIDTH), BF16),
        scratch_shapes=[pltpu.VMEM((2, t, 1), F32), pltpu.VMEM((2, t, 1), F32),
                        pltpu.VMEM((2, t, LANES), F32)],
        compiler_params=_cparams(("arbitrary",) * 4),
        name="fox_attention",
    )(q, k, v, cq, ck)


def _ssd_kernel(xbc_ref, prev_ref, fd_ref, dtt_ref, z_ref, cw_ref, cb_ref, alog_ref, alogc_ref,
                dsk_ref, nrm_ref, tri_ref, y_ref, state_ref):
    c_idx = pl.program_id(1)
    L = SSD_CHUNK
    P = HEAD_DIM
    N = SSD_STATE

    @pl.when(c_idx == 0)
    def _():
        state_ref[...] = jnp.zeros_like(state_ref)

    cur = xbc_ref[...]
    prev = jnp.where(c_idx > 0, prev_ref[...], 0.0)
    ext = jnp.concatenate([prev, cur], axis=0)
    cw = cw_ref[...]
    conv = cb_ref[...] + cw[SSD_CONV - 1:SSD_CONV, :] * cur
    for kk in range(SSD_CONV - 1):
        sh = SSD_CONV - 1 - kk
        conv = conv + cw[kk:kk + 1, :] * ext[8 - sh:8 - sh + L, :]
    xc = _silu(conv)
    xs = xc[:, :SSD_INNER]
    bm = xc[:, SSD_INNER:SSD_INNER + SSD_GROUPS * N]
    cm = xc[:, SSD_INNER + SSD_GROUPS * N:]

    dt = fd_ref[:, FOX_HEADS:FOX_HEADS + SSD_HEADS]
    a_row = -jnp.exp(alog_ref[...])
    a_col = -jnp.exp(alogc_ref[...])
    tri = tri_ref[...]
    cum = _dot_exact_lhs(tri, dt * a_row, 2)
    cum_t = lax.dot_general(_split_bf16(dtt_ref[...] * a_col, 2)[0], tri, (((1,), (1,)), ((), ())),
                            preferred_element_type=F32)
    cum_t = cum_t + lax.dot_general(_split_bf16(dtt_ref[...] * a_col, 2)[1], tri,
                                    (((1,), (1,)), ((), ())), preferred_element_type=F32)
    row_i = lax.broadcasted_iota(jnp.int32, (L, L), 0)
    col_i = lax.broadcasted_iota(jnp.int32, (L, L), 1)
    causal = row_i >= col_i

    ys = []
    for g in range(SSD_GROUPS):
        bm_g = bm[:, g * N:(g + 1) * N]
        cm_g = cm[:, g * N:(g + 1) * N]
        cb_g = _dot_nt(cm_g.astype(BF16), bm_g.astype(BF16))
        bm_t = bm_g.T
        for hh in range(SSD_HEADS // SSD_GROUPS):
            h = g * (SSD_HEADS // SSD_GROUPS) + hh
            cum_c = cum[:, h:h + 1]
            cum_r = cum_t[h:h + 1, :]
            seg = jnp.where(causal, jnp.exp(jnp.minimum(cum_c - cum_r, 0.0)), 0.0)
            x_h = xs[:, h * P:(h + 1) * P]
            xdt = (x_h * dt[:, h:h + 1]).astype(BF16)
            y_h = _dot((cb_g * seg).astype(BF16), xdt)
            h_prev = state_ref[h]
            y_h = y_h + _dot((cm_g * jnp.exp(cum_c)).astype(BF16), h_prev.astype(BF16))
            last = cum_t[h:h + 1, L - 1:L]
            dte = jnp.exp(last - cum_r)
            st = _dot((bm_t * dte).astype(BF16), xdt)
            state_ref[h] = jnp.exp(last) * h_prev + st
            ys.append(y_h)
    y = jnp.concatenate(ys, axis=1) + xs * dsk_ref[...]
    y = y * _silu(z_ref[...].astype(F32))
    gw = SSD_INNER // SSD_GROUPS
    outs = []
    for g in range(SSD_GROUPS):
        yg = y[:, g * gw:(g + 1) * gw]
        outs.append(yg * lax.rsqrt(jnp.mean(yg * yg, axis=-1, keepdims=True) + RMS_EPS))
    y_ref[...] = (jnp.concatenate(outs, axis=1) * nrm_ref[...]).astype(y_ref.dtype)


def _ssd(xbc, fd, dtt, z, conv_w, conv_b, a_log, d_skip_c, ssd_norm, tri, batch, seq):
    L = SSD_CHUNK
    nc = seq // L
    t_rows = batch * seq
    row = lambda w: pl.BlockSpec((L, w), lambda b, c: (b * nc + c, 0))
    full = lambda a: pl.BlockSpec(a.shape, lambda b, c: (0,) * a.ndim)
    prev = pl.BlockSpec((8, SSD_CONV_DIM), lambda b, c: (jnp.maximum((b * nc + c) * (L // 8) - 1, 0), 0))
    return pl.pallas_call(
        _ssd_kernel,
        grid=(batch, nc),
        in_specs=[row(SSD_CONV_DIM), prev, row(LANES),
                  pl.BlockSpec((None, SSD_HEADS, L), lambda b, c: (b, 0, c)),
                  row(SSD_INNER), full(conv_w), full(conv_b), full(a_log[None, :]),
                  full(a_log[:, None]), full(d_skip_c), full(ssd_norm), full(tri)],
        out_specs=row(SSD_INNER),
        out_shape=jax.ShapeDtypeStruct((t_rows, SSD_INNER), BF16),
        scratch_shapes=[pltpu.VMEM((SSD_HEADS, SSD_STATE, HEAD_DIM), F32)],
        compiler_params=_cparams(("arbitrary", "arbitrary")),
        name="ssd_scan",
    )(xbc, xbc, fd, dtt, z, conv_w, conv_b, a_log[None, :], a_log[:, None], d_skip_c, ssd_norm, tri)


def _even_out_kernel(n_chunks, o_ref, y_ref, x_ref, wo_ref, g_ref, w13_ref, w2_ref, out_ref, acc_ref):
    w = FOX_WIDTH
    x1 = x_ref[...] + _dot(o_ref[...], wo_ref[:w, :]) + _dot(y_ref[...], wo_ref[w:, :])
    h = _rms(x1, g_ref[...]).astype(BF16)
    acc_ref[...] = x1

    def body(c, carry):
        gu = _dot(h, w13_ref[c])
        tf = gu.shape[1] // 2
        a = (_silu(gu[:, :tf]) * gu[:, tf:]).astype(BF16)
        acc_ref[...] += _dot(a, w2_ref[c])
        return carry

    lax.fori_loop(0, n_chunks, body, 0)
    out_ref[...] = acc_ref[...]


def _even_out_ffn(o_fox, y_ssd, x2d, wo, g, w13r, w2r):
    t_rows = x2d.shape[0]
    tm = min(ROW_TILE, t_rows)
    row = lambda w: pl.BlockSpec((tm, w), lambda i: (i, 0))
    full = lambda a: pl.BlockSpec(a.shape, lambda i: (0,) * a.ndim, pipeline_mode=pl.Buffered(1))
    return pl.pallas_call(
        functools.partial(_even_out_kernel, w13r.shape[0]),
        grid=(t_rows // tm,),
        in_specs=[row(FOX_WIDTH), row(SSD_INNER), row(D_MODEL), full(wo), full(g), full(w13r), full(w2r)],
        out_specs=row(D_MODEL),
        out_shape=jax.ShapeDtypeStruct((t_rows, D_MODEL), F32),
        scratch_shapes=[pltpu.VMEM((tm, D_MODEL), F32)],
        compiler_params=_cparams(("arbitrary",)),
        name="even_out_ffn",
    )(o_fox, y_ssd, x2d, wo, g, w13r, w2r)


def _rope(x, cos, sin_signed):
    n = x.shape[1]
    lane = lax.broadcasted_iota(jnp.int32, x.shape, 1)
    first = (lane % HEAD_DIM) < (HEAD_DIM // 2)
    rot = jnp.where(first, pltpu.roll(x, n - HEAD_DIM // 2, 1), pltpu.roll(x, HEAD_DIM // 2, 1))
    reps = n // LANES
    cos_t = jnp.concatenate([cos] * reps, axis=1)
    sin_t = jnp.concatenate([sin_signed] * reps, axis=1)
    return x * cos_t + rot * sin_t


def _odd_qkv_kernel(x_ref, g_ref, w_ref, qg_ref, kg_ref, hm_ref, cos_ref, sin_ref, q_ref, k_ref, v_ref):
    h = _rms(x_ref[...], g_ref[...]).astype(BF16)
    hm = hm_ref[...]
    w = DIL_WIDTH
    cos = cos_ref[...]
    sin = sin_ref[...]
    q = _head_norm(_dot(h, w_ref[:, :w]), hm, qg_ref[...])
    q_ref[...] = (_rope(q, cos, sin) * (HEAD_DIM ** -0.5)).astype(BF16)
    k = _head_norm(_dot(h, w_ref[:, w:2 * w]), hm, kg_ref[...])
    k_ref[...] = _rope(k, cos, sin).astype(BF16)
    v_ref[...] = _dot(h, w_ref[:, 2 * w:]).astype(BF16)


def _odd_qkv(x2d, seq, g, w, qg, kg, hm, cos, sin):
    t_rows = x2d.shape[0]
    tm = min(ROW_TILE, seq)
    per_seq = seq // tm
    row = lambda wd: pl.BlockSpec((tm, wd), lambda i: (i, 0))
    full = lambda a: pl.BlockSpec(a.shape, lambda i: (0,) * a.ndim)
    tab = pl.BlockSpec((tm, LANES), lambda i: (i % per_seq, 0))
    return pl.pallas_call(
        _odd_qkv_kernel,
        grid=(t_rows // tm,),
        in_specs=[row(D_MODEL), full(g), full(w), full(qg), full(kg), full(hm), tab, tab],
        out_specs=[row(DIL_WIDTH)] * 3,
        out_shape=[jax.ShapeDtypeStruct((t_rows, DIL_WIDTH), BF16)] * 3,
        compiler_params=_cparams(("arbitrary",)),
        name="odd_qkv",
    )(x2d, g, w, qg, kg, hm, cos, sin)


def _dilated_kernel(q_ref, kp_ref, kc_ref, vp_ref, vc_ref, band_ref, o_ref, lse_ref):
    j = pl.program_id(2)
    blk = DIL_BLOCK
    k2 = jnp.concatenate([kp_ref[...], kc_ref[...]], axis=0)
    v2 = jnp.concatenate([vp_ref[...], vc_ref[...]], axis=0)
    col = lax.broadcasted_iota(jnp.int32, (blk, 2 * blk), 1)
    bias = jnp.where((j == 0) & (col < blk), NEG_BIG, band_ref[...])
    lane = lax.broadcasted_iota(jnp.int32, (blk, LANES), 1)
    lse_acc = jnp.zeros((blk, LANES), F32)
    for pr in range(DIL_HEADS // 2):
        sl = slice(pr * LANES, (pr + 1) * LANES)
        q2 = q_ref[:, sl]
        kk = k2[:, sl]
        vv = v2[:, sl]
        halves = []
        for h in range(2):
            qm = jnp.where((lane >= h * HEAD_DIM) & (lane < (h + 1) * HEAD_DIM), q2, jnp.zeros_like(q2))
            s = _dot_nt(qm, kk) + bias
            m = jnp.max(s, axis=-1, keepdims=True)
            e = jnp.exp(s - m)
            den = jnp.sum(e, axis=-1, keepdims=True)
            halves.append(_dot(e.astype(BF16), vv) / den)
            lse_acc = jnp.where(lane == 2 * pr + h, m + jnp.log(den), lse_acc)
        o_ref[:, sl] = jnp.where(lane < HEAD_DIM, halves[0], halves[1]).astype(o_ref.dtype)
    lse_ref[...] = lse_acc


def _dilated_branch(q, k, v, band, dilation):
    b, s, w = q.shape
    d = dilation
    n_sub = s // d
    nb = n_sub // DIL_BLOCK
    view = lambda t: t.reshape(b, n_sub, d * w)
    cur = pl.BlockSpec((None, DIL_BLOCK, w), lambda bi, r, j: (bi, j, r))
    prev = pl.BlockSpec((None, DIL_BLOCK, w), lambda bi, r, j: (bi, jnp.maximum(j - 1, 0), r))
    lse_spec = pl.BlockSpec((None, DIL_BLOCK, LANES), lambda bi, r, j: (bi, j, r))
    o, lse = pl.pallas_call(
        _dilated_kernel,
        grid=(b, d, nb),
        in_specs=[cur, prev, cur, prev, cur, pl.BlockSpec(band.shape, lambda bi, r, j: (0, 0))],
        out_specs=[cur, lse_spec],
        out_shape=[jax.ShapeDtypeStruct((b, n_sub, d * w), BF16),
                   jax.ShapeDtypeStruct((b, n_sub, d * LANES), F32)],
        compiler_params=_cparams(("arbitrary",) * 3),
        name=f"dilated_attn_d{d}",
    )(view(q), view(k), view(k), view(v), view(v), band)
    return o.reshape(b * s, w), lse.reshape(b * s, LANES)


def _odd_out_kernel(tiles_total, o1_ref, o2_ref, o3_ref, l1_ref, l2_ref, l3_ref, x_ref, wo_ref, g_ref,
                    rw_ref, ex_ref, tril_ref, x1_ref, h_ref, info_ref, cnt_ref, carry_ref):
    i = pl.program_id(0)

    @pl.when(i == 0)
    def _():
        carry_ref[...] = jnp.zeros_like(carry_ref)

    l1, l2, l3 = l1_ref[...], l2_ref[...], l3_ref[...]
    mx = jnp.maximum(jnp.maximum(l1, l2), l3)
    a1, a2, a3 = jnp.exp(l1 - mx), jnp.exp(l2 - mx), jnp.exp(l3 - mx)
    inv = 1.0 / (a1 + a2 + a3)
    ex = ex_ref[...]
    o = (_dot_exact_rhs(a1 * inv, ex, 2) * o1_ref[...].astype(F32)
         + _dot_exact_rhs(a2 * inv, ex, 2) * o2_ref[...].astype(F32)
         + _dot_exact_rhs(a3 * inv, ex, 2) * o3_ref[...].astype(F32))
    x1 = x_ref[...] + _dot(o.astype(BF16), wo_ref[...])
    x1_ref[...] = x1
    h = _rms(x1, g_ref[...])
    h_ref[...] = h

    logits = jnp.dot(h, rw_ref[...], preferred_element_type=F32, precision=lax.Precision.HIGHEST)
    lane = lax.broadcasted_iota(jnp.int32, logits.shape, 1)
    logits = jnp.where(lane < N_EXPERTS, logits, -jnp.inf)
    m1 = jnp.max(logits, axis=-1, keepdims=True)
    i1 = jnp.min(jnp.where(logits == m1, lane, LANES), axis=-1, keepdims=True)
    rest = jnp.where(lane == i1, -jnp.inf, logits)
    m2 = jnp.max(rest, axis=-1, keepdims=True)
    i2 = jnp.min(jnp.where(rest == m2, lane, LANES), axis=-1, keepdims=True)
    e2 = jnp.exp(m2 - m1)
    g1 = 1.0 / (1.0 + e2)
    g2 = e2 * g1
    hot1 = lane == i1
    hot2 = lane == i2
    onehot = jnp.where(hot1 | hot2, 1.0, 0.0).astype(BF16)
    before = _dot(tril_ref[...], onehot) + carry_ref[...]
    r1 = jnp.sum(jnp.where(hot1, before, 0.0), axis=-1, keepdims=True)
    r2 = jnp.sum(jnp.where(hot2, before, 0.0), axis=-1, keepdims=True)
    total = before[-1:, :] + onehot[-1:, :].astype(F32)
    carry_ref[...] = total
    cnt_ref[...] = jnp.broadcast_to(total, cnt_ref.shape)
    info = jnp.where(lane == 0, i1.astype(F32), 0.0)
    info = jnp.where(lane == 1, i2.astype(F32), info)
    info = jnp.where(lane == 2, g1, info)
    info = jnp.where(lane == 3, g2, info)
    info = jnp.where(lane == 4, r1, info)
    info = jnp.where(lane == 5, r2, info)
    info_ref[...] = info


def _odd_out_router(o1, o2, o3, l1, l2, l3, x2d, wo, g, rw, ex, tril_strict):
    t_rows = x2d.shape[0]
    tm = min(ROW_TILE, t_rows)
    n = t_rows // tm
    row = lambda w: pl.BlockSpec((tm, w), lambda i: (i, 0))
    full = lambda a: pl.BlockSpec(a.shape, lambda i: (0,) * a.ndim)
    return pl.pallas_call(
        functools.partial(_odd_out_kernel, n),
        grid=(n,),
        in_specs=[row(DIL_WIDTH)] * 3 + [row(LANES)] * 3 + [row(D_MODEL), full(wo), full(g), full(rw),
                                                             full(ex), full(tril_strict)],
        out_specs=[row(D_MODEL), row(D_MODEL), row(LANES), pl.BlockSpec((8, LANES), lambda i: (0, 0))],
        out_shape=[jax.ShapeDtypeStruct((t_rows, D_MODEL), F32), jax.ShapeDtypeStruct((t_rows, D_MODEL), F32),
                   jax.ShapeDtypeStruct((t_rows, LANES), F32), jax.ShapeDtypeStruct((8, LANES), F32)],
        scratch_shapes=[pltpu.VMEM((1, LANES), F32)],
        compiler_params=_cparams(("arbitrary",)),
        name="odd_out_router",
    )(o1, o2, o3, l1, l2, l3, x2d, wo, g, rw, ex, tril_strict)


def _moe_kernel(n_chunks, te_ref, nu_ref, src_cur_ref, src_nxt_ref, dst_ref, h_hbm, w13_ref, w2_ref,
                y_hbm, xbuf, ybuf, gsem, ssem):
    i = pl.program_id(0)
    n_used = nu_ref[0]
    gm = xbuf.shape[1]
    slot = i % 2

    def gather(src_ref, s):
        def body(r, carry):
            tok = src_ref[0, 0, r]
            pltpu.make_async_copy(h_hbm.at[pl.ds(tok, 1)], xbuf.at[s, pl.ds(r, 1)], gsem.at[s]).start()
            return carry
        lax.fori_loop(0, gm, body, 0)

    def wait_rows(buf, sem, s):
        pltpu.make_async_copy(h_hbm.at[pl.ds(0, gm)], buf.at[s], sem.at[s]).wait()

    @pl.when(i == 0)
    def _():
        gather(src_cur_ref, 0)

    @pl.when(i + 1 < n_used)
    def _():
        gather(src_nxt_ref, 1 - slot)

    @pl.when(i < n_used)
    def _():
        wait_rows(xbuf, gsem, slot)

        @pl.when(i >= 2)
        def _():
            wait_rows(ybuf, ssem, slot)

        x = xbuf[slot].astype(BF16)
        ybuf[slot] = jnp.zeros(ybuf.shape[1:], F32)

        def body(c, carry):
            gu = _dot(x, w13_ref[c])
            tf = gu.shape[1] // 2
            a = (_silu(gu[:, :tf]) * gu[:, tf:]).astype(BF16)
            ybuf[slot] += _dot(a, w2_ref[c])
            return carry

        lax.fori_loop(0, n_chunks, body, 0)

        def scatter(r, carry):
            row = dst_ref[0, 0, r]
            pltpu.make_async_copy(ybuf.at[slot, pl.ds(r, 1)], y_hbm.at[pl.ds(row, 1)], ssem.at[slot]).start()
            return carry
        lax.fori_loop(0, gm, scatter, 0)

    @pl.when(i == pl.num_programs(0) - 1)
    def _():
        last = n_used - 1

        @pl.when(n_used >= 2)
        def _():
            wait_rows(ybuf, ssem, (last - 1) % 2)

        @pl.when(n_used >= 1)
        def _():
            wait_rows(ybuf, ssem, last % 2)


def _moe_ffn(h2d, w13r, w2r, tile_expert, n_used, src, dst, out_rows):
    nt = tile_expert.shape[0]
    gm = src.shape[-1]
    nc = w13r.shape[1]
    grid_spec = pltpu.PrefetchScalarGridSpec(
        num_scalar_prefetch=2,
        grid=(nt,),
        in_specs=[
            pl.BlockSpec((1, 1, gm), lambda i, te, nu: (i, 0, 0), memory_space=pltpu.SMEM),
            pl.BlockSpec((1, 1, gm), lambda i, te, nu: (jnp.minimum(i + 1, nt - 1), 0, 0), memory_space=pltpu.SMEM),
            pl.BlockSpec((1, 1, gm), lambda i, te, nu: (i, 0, 0), memory_space=pltpu.SMEM),
            pl.BlockSpec(memory_space=pl.ANY),
            pl.BlockSpec((None,) + w13r.shape[1:], lambda i, te, nu: (te[i], 0, 0, 0), pipeline_mode=pl.Buffered(1)),
            pl.BlockSpec((None,) + w2r.shape[1:], lambda i, te, nu: (te[i], 0, 0, 0), pipeline_mode=pl.Buffered(1)),
        ],
        out_specs=pl.BlockSpec(memory_space=pl.ANY),
        scratch_shapes=[pltpu.VMEM((2, gm, D_MODEL), F32), pltpu.VMEM((2, gm, D_MODEL), F32),
                        pltpu.SemaphoreType.DMA((2,)), pltpu.SemaphoreType.DMA((2,))],
    )
    return pl.pallas_call(
        functools.partial(_moe_kernel, nc),
        grid_spec=grid_spec,
        out_shape=jax.ShapeDtypeStruct((out_rows, D_MODEL), F32),
        compiler_params=_cparams(("arbitrary",)),
        name="moe_ffn",
    )(tile_expert, n_used, src, src, dst, h2d, w13r, w2r)


def _combine_kernel(x_ref, y0_ref, y1_ref, info_ref, o_ref):
    o_ref[...] = x_ref[...] + info_ref[:, 2:3] * y0_ref[...] + info_ref[:, 3:4] * y1_ref[...]


def _combine(x1, y, info):
    t_rows = x1.shape[0]
    tm = min(ROW_TILE, t_rows)
    row = lambda w: pl.BlockSpec((tm, w), lambda i: (i, 0))
    ysp = lambda kk: pl.BlockSpec((None, tm, D_MODEL), lambda i: (kk, i, 0))
    return pl.pallas_call(
        _combine_kernel,
        grid=(t_rows // tm,),
        in_specs=[row(D_MODEL), ysp(0), ysp(1), row(LANES)],
        out_specs=row(D_MODEL),
        out_shape=jax.ShapeDtypeStruct((t_rows, D_MODEL), F32),
        compiler_params=_cparams(("arbitrary",)),
        name="moe_combine",
    )(x1, y, y, info)


def _head_mean_matrix(width):
    idx = np.arange(width) // HEAD_DIM
    return jnp.asarray((idx[:, None] == idx[None, :]).astype(np.float32) / HEAD_DIM, dtype=BF16)


def _tril(n, strict):
    r = np.arange(n)
    m = (r[None, :] < r[:, None]) if strict else (r[None, :] <= r[:, None])
    return jnp.asarray(m.astype(np.float32), dtype=BF16)


def _rope_tables(seq):
    half = HEAD_DIM // 2
    inv_freq = ROPE_THETA ** (-jnp.arange(half, dtype=F32) / half)
    ang = jnp.arange(seq, dtype=F32)[:, None] * inv_freq[None, :]
    cos = jnp.cos(ang)
    sin = jnp.sin(ang)
    cos_t = jnp.concatenate([cos, cos, cos, cos], axis=1)
    sin_t = jnp.concatenate([-sin, sin, -sin, sin], axis=1)
    return cos_t, sin_t


def _band_bias():
    qi = np.arange(DIL_BLOCK)[:, None]
    ki = np.arange(-DIL_BLOCK, DIL_BLOCK)[None, :]
    rel = qi - ki
    return jnp.asarray(np.where((rel >= 0) & (rel <= DIL_BLOCK), 0.0, NEG_BIG).astype(np.float32))


def _head_expand():
    m = np.zeros((LANES, DIL_WIDTH), np.float32)
    for h in range(DIL_HEADS):
        m[h, h * HEAD_DIM:(h + 1) * HEAD_DIM] = 1.0
    return jnp.asarray(m, dtype=BF16)


def _chunk_w13(w13, tf):
    lead = w13.shape[:-2]
    d, f2 = w13.shape[-2:]
    f = f2 // 2
    w = w13.reshape(lead + (d, 2, f // tf, tf))
    w = jnp.moveaxis(w, -2, -4)
    return w.reshape(lead + (f // tf, d, 2 * tf)).astype(BF16)


def _chunk_w2(w2, tf):
    lead = w2.shape[:-2]
    f, d = w2.shape[-2:]
    return w2.reshape(lead + (f // tf, tf, d)).astype(BF16)


def _even_layer(x2d, batch, seq, norm1, w_in, fox_qn, fox_kn, fox_fbias, conv_w, conv_b, dt_bias, a_log,
                d_skip, ssd_norm, w_out, norm2, ffn_w13, ffn_w2):
    w = FOX_WIDTH
    o_z = 3 * w + FOX_HEADS
    o_x = o_z + SSD_INNER
    o_dt = o_x + SSD_CONV_DIM
    wqkv = w_in[:, :3 * w].astype(BF16)
    wzx = w_in[:, o_z:o_dt].astype(BF16)
    pad = LANES - FOX_HEADS - SSD_HEADS
    wfd = jnp.concatenate([w_in[:, 3 * w:o_z], w_in[:, o_dt:], jnp.zeros((D_MODEL, pad), F32)], axis=1).astype(BF16)
    fdb = jnp.concatenate([fox_fbias, dt_bias, jnp.zeros((pad,), F32)])[None, :]
    tm = min(ROW_TILE, seq)
    q, k, v, z, xbc, fd = _even_in_proj(
        x2d, seq, norm1[None, :], wqkv, wzx, wfd, fox_qn.reshape(1, w), fox_kn.reshape(1, w), fdb,
        _head_mean_matrix(w), _tril(tm, False))

    pairs = FOX_HEADS // 2
    c = fd[:, :FOX_HEADS].reshape(batch, seq, pairs, 2)
    cq = jnp.transpose(c, (0, 2, 1, 3))
    ck = jnp.transpose(c, (0, 2, 3, 1))
    shp = (batch, seq, w)
    o_fox = _fox_attention(q.reshape(shp), k.reshape(shp), v.reshape(shp), cq, ck).reshape(batch * seq, w)

    dtt = jnp.transpose(fd[:, FOX_HEADS:FOX_HEADS + SSD_HEADS].reshape(batch, seq, SSD_HEADS), (0, 2, 1))
    d_skip_c = jnp.repeat(d_skip, HEAD_DIM)[None, :]
    y = _ssd(xbc, fd, dtt, z, conv_w, conv_b[None, :], a_log, d_skip_c, ssd_norm[None, :],
             _tril(SSD_CHUNK, False), batch, seq)

    return _even_out_ffn(o_fox, y, x2d, w_out.astype(BF16), norm2[None, :],
                         _chunk_w13(ffn_w13, FFN_CHUNK), _chunk_w2(ffn_w2, FFN_CHUNK))


def _moe_tables(info, counts, t_rows, gm):
    e1 = info[:, 0].astype(jnp.int32)
    e2 = info[:, 1].astype(jnp.int32)
    r1 = info[:, 4].astype(jnp.int32)
    r2 = info[:, 5].astype(jnp.int32)
    cnt = counts[0, :N_EXPERTS].astype(jnp.int32)
    tiles = (cnt + gm - 1) // gm
    tile_end = jnp.cumsum(tiles)
    offs = (tile_end - tiles) * gm
    nt = (2 * t_rows) // gm + N_EXPERTS
    tile_expert = jnp.minimum(jnp.searchsorted(tile_end, jnp.arange(nt, dtype=jnp.int32), side="right"),
                              N_EXPERTS - 1).astype(jnp.int32)
    n_used = tile_end[-1:].astype(jnp.int32)
    slot1 = offs[e1] + r1
    slot2 = offs[e2] + r2
    tok = jnp.arange(t_rows, dtype=jnp.int32)
    out_half = t_rows + N_EXPERTS * gm
    src = jnp.zeros((nt * gm,), jnp.int32).at[slot1].set(tok).at[slot2].set(tok)
    spare = t_rows + jnp.repeat(tile_expert, gm) * gm + jnp.tile(jnp.arange(gm, dtype=jnp.int32), nt)
    dst = spare.at[slot1].set(tok).at[slot2].set(out_half + tok)
    return tile_expert, n_used, src.reshape(nt, 1, gm), dst.reshape(nt, 1, gm), out_half


def _odd_layer(x2d, batch, seq, norm1, w_qkv, qn, kn, w_out, norm2, router, moe_w13, moe_w2):
    t_rows = batch * seq
    cos, sin = _rope_tables(seq)
    q, k, v = _odd_qkv(x2d, seq, norm1[None, :], w_qkv.astype(BF16), qn.reshape(1, DIL_WIDTH),
                       kn.reshape(1, DIL_WIDTH), _head_mean_matrix(DIL_WIDTH), cos, sin)
    shp = (batch, seq, DIL_WIDTH)
    band = _band_bias()
    outs, lses = zip(*[_dilated_branch(q.reshape(shp), k.reshape(shp), v.reshape(shp), band, d)
                       for (_, d) in DIL_PATTERNS])
    rw = jnp.concatenate([router, jnp.zeros((D_MODEL, LANES - N_EXPERTS), F32)], axis=1)
    tm = min(ROW_TILE, t_rows)
    x1, h, info, counts = _odd_out_router(*outs, *lses, x2d, w_out.astype(BF16), norm2[None, :], rw,
                                          _head_expand(), _tril(tm, True))
    gm = min(MOE_TILE, t_rows)
    tile_expert, n_used, src, dst, out_half = _moe_tables(info, counts, t_rows, gm)
    y = _moe_ffn(h, _chunk_w13(moe_w13, MOE_CHUNK), _chunk_w2(moe_w2, MOE_CHUNK), tile_expert, n_used,
                 src, dst, 2 * out_half)
    return _combine(x1, y.reshape(2, out_half, D_MODEL), info)


def kernel(x, e_norm1, e_w_in, e_fox_qn, e_fox_kn, e_fox_fbias, e_conv_w, e_conv_b, e_dt_bias, e_a_log, e_d_skip, e_ssd_norm, e_w_out, e_norm2, e_ffn_w13, e_ffn_w2, o_norm1, o_w_qkv, o_qn, o_kn, o_w_out, o_norm2, o_router, o_moe_w13, o_moe_w2):
    batch, seq, _ = x.shape
    depth = e_norm1.shape[0] + o_norm1.shape[0]
    x2d = x.reshape(batch * seq, D_MODEL)
    for i in range(depth):
        j = i // 2
        if i % 2 == 0:
            x2d = _even_layer(x2d, batch, seq, e_norm1[j], e_w_in[j], e_fox_qn[j], e_fox_kn[j], e_fox_fbias[j],
                              e_conv_w[j], e_conv_b[j], e_dt_bias[j], e_a_log[j], e_d_skip[j], e_ssd_norm[j],
                              e_w_out[j], e_norm2[j], e_ffn_w13[j], e_ffn_w2[j])
        else:
            x2d = _odd_layer(x2d, batch, seq, o_norm1[j], o_w_qkv[j], o_qn[j], o_kn[j], o_w_out[j], o_norm2[j],
                             o_router[j], o_moe_w13[j], o_moe_w2[j])
    return x2d.reshape(batch, seq, D_MODEL)
```

```python
import functools
import math

import jax
import jax.numpy as jnp
import numpy as np
from jax import lax
from jax.experimental import pallas as pl
from jax.experimental.pallas import tpu as pltpu

F32 = jnp.float32
BF16 = jnp.bfloat16

D_MODEL = 1024
HEAD_DIM = 64
RMS_EPS = 1e-6
ROPE_THETA = 10000.0
FOX_HEADS = 8
FOX_WIDTH = FOX_HEADS * HEAD_DIM
SSD_HEADS = 8
SSD_INNER = 512
SSD_GROUPS = 2
SSD_STATE = 128
SSD_CONV = 4
SSD_CHUNK = 128
SSD_CONV_DIM = SSD_INNER + 2 * SSD_GROUPS * SSD_STATE
DIL_HEADS = 16
DIL_WIDTH = DIL_HEADS * HEAD_DIM
DIL_PATTERNS = ((128, 1), (512, 4), (2048, 16))
DIL_BLOCK = 128
FFN_DIM = 2816
N_EXPERTS = 8
EXPERT_DIM = 3584

LANES = 128
NEG_BIG = -1e30
LOG2E = math.log2(math.e)
BIAS_PARTS = 3
BIAS_LANES = 2 * BIAS_PARTS
VMEM_LIMIT = 56 * 1024 * 1024

ROW_TILE = 512
ATTN_TILE = 512
FFN_CHUNK = 256
MOE_CHUNK = 512
MOE_TILE = 512


def _cparams(sem):
    return pltpu.CompilerParams(dimension_semantics=sem, vmem_limit_bytes=VMEM_LIMIT)


def _softplus_parts(x):
    return jnp.log(1.0 + jnp.exp(-jnp.abs(x)))


def _split_bf16(a, parts):
    out = []
    r = a
    for _ in range(parts):
        p = r.astype(BF16)
        out.append(p)
        r = r - p.astype(F32)
    return out


def _dot(a, b):
    return jnp.dot(a, b, preferred_element_type=F32)


def _dot_nt(a, b):
    return lax.dot_general(a, b, (((1,), (1,)), ((), ())), preferred_element_type=F32)


def _dot_exact_rhs(a_f32, b_bf16, parts):
    acc = None
    for p in _split_bf16(a_f32, parts):
        t = _dot(p, b_bf16)
        acc = t if acc is None else acc + t
    return acc


def _dot_exact_lhs(a_bf16, b_f32, parts):
    acc = None
    for p in _split_bf16(b_f32, parts):
        t = _dot(a_bf16, p)
        acc = t if acc is None else acc + t
    return acc


def _rms(x, g):
    return x * lax.rsqrt(jnp.mean(x * x, axis=-1, keepdims=True) + RMS_EPS) * g


def _head_norm(x, hm, gain):
    ms = _dot((x * x).astype(BF16), hm)
    return x * lax.rsqrt(ms + RMS_EPS) * gain


def _silu(x):
    return x * (1.0 / (1.0 + jnp.exp(-x)))


def _even_in_kernel(tiles_per_seq, x_ref, g_ref, wqk_ref, wvt_ref, wzx_ref, wfd_ref, qg_ref, kg_ref,
                    fdb_ref, hm_ref, tril_ref, pq_ref, pk_ref, qone_ref, kone_ref,
                    q_ref, qb_ref, k_ref, kb_ref, vt_ref, z_ref, xbc_ref, fd_ref, carry_ref):
    i = pl.program_id(0)

    @pl.when(i % tiles_per_seq == 0)
    def _():
        carry_ref[...] = jnp.zeros_like(carry_ref)

    h = _rms(x_ref[...], g_ref[...]).astype(BF16)
    qk = _dot(h, wqk_ref[...])
    hm = hm_ref[...]
    w = FOX_WIDTH
    q_ref[...] = (_head_norm(qk[:, :w], hm, qg_ref[...]) * (HEAD_DIM ** -0.5 * LOG2E)).astype(BF16)
    k_ref[...] = _head_norm(qk[:, w:], hm, kg_ref[...]).astype(BF16)
    vt_ref[...] = _dot_nt(wvt_ref[...], h).astype(BF16)
    zx = _dot(h, wzx_ref[...])
    z_ref[...] = zx[:, :SSD_INNER].astype(BF16)
    xbc_ref[...] = zx[:, SSD_INNER:]
    fd = _dot(h, wfd_ref[...]) + fdb_ref[...]
    t = _softplus_parts(fd)
    log_f = jnp.minimum(fd, 0.0) - t
    dt = jnp.maximum(fd, 0.0) + t
    c = _dot_exact_lhs(tril_ref[...], log_f, 2) + carry_ref[...]
    carry_ref[...] = c[-1:, :]
    lane = lax.broadcasted_iota(jnp.int32, fd.shape, 1)
    fd_ref[...] = jnp.where(lane < FOX_HEADS, c, dt)
    qb = qone_ref[...]
    kb = kone_ref[...]
    for j, part in enumerate(_split_bf16(c * LOG2E, 3)):
        qb = qb + _dot(part, pq_ref[j])
        kb = kb + _dot(part, pk_ref[j])
    qb_ref[...] = qb.astype(BF16)
    kb_ref[...] = kb.astype(BF16)


def _even_in_proj(x2d, seq, g, wqk, wvt, wzx, wfd, qg, kg, fdb, hm, tril, pq, pk, qone, kone):
    t_rows = x2d.shape[0]
    tm = min(ROW_TILE, seq)
    n = t_rows // tm
    row = lambda w: pl.BlockSpec((tm, w), lambda i: (i, 0))
    full = lambda a: pl.BlockSpec(a.shape, lambda i: (0,) * a.ndim)
    consts = (g, wqk, wvt, wzx, wfd, qg, kg, fdb, hm, tril, pq, pk, qone, kone)
    return pl.pallas_call(
        functools.partial(_even_in_kernel, seq // tm),
        grid=(n,),
        in_specs=[row(D_MODEL)] + [full(a) for a in consts],
        out_specs=[row(FOX_WIDTH)] * 4 + [pl.BlockSpec((FOX_WIDTH, tm), lambda i: (0, i)),
                                           row(SSD_INNER), row(SSD_CONV_DIM), row(LANES)],
        out_shape=[jax.ShapeDtypeStruct((t_rows, FOX_WIDTH), BF16)] * 4
        + [jax.ShapeDtypeStruct((FOX_WIDTH, t_rows), BF16),
           jax.ShapeDtypeStruct((t_rows, SSD_INNER), BF16),
           jax.ShapeDtypeStruct((t_rows, SSD_CONV_DIM), F32),
           jax.ShapeDtypeStruct((t_rows, LANES), F32)],
        scratch_shapes=[pltpu.VMEM((1, LANES), F32)],
        compiler_params=_cparams(("arbitrary",)),
        name="even_in_proj",
    )(x2d, *consts)


def _fox_kernel(q_ref, qb_ref, k_ref, kb_ref, vt_ref, o_ref, qs_ref, m_ref, l_ref, acc_ref):
    qi = pl.program_id(2)
    ki = pl.program_id(3)
    tq = q_ref.shape[0]
    tk = k_ref.shape[0]

    @pl.when(ki == 0)
    def _():
        m_ref[...] = jnp.full_like(m_ref, NEG_BIG)
        l_ref[...] = jnp.zeros_like(l_ref)
        acc_ref[...] = jnp.zeros_like(acc_ref)
        q2 = q_ref[...]
        qb = qb_ref[...]
        lane = lax.broadcasted_iota(jnp.int32, q2.shape, 1)
        zero = jnp.zeros_like(q2)
        for h in range(2):
            qh = jnp.where((lane >= h * HEAD_DIM) & (lane < (h + 1) * HEAD_DIM), q2, zero)
            bh = jnp.where((lane >= h * BIAS_LANES) & (lane < (h + 1) * BIAS_LANES), qb, zero)
            qs_ref[h] = jnp.concatenate([qh, bh], axis=1)

    def step(masked):
        ka = jnp.concatenate([k_ref[...], kb_ref[...]], axis=1)
        vt = vt_ref[...]
        if masked:
            keep = (lax.broadcasted_iota(jnp.int32, (tk, tq), 0)
                    <= lax.broadcasted_iota(jnp.int32, (tk, tq), 1))
        for h in range(2):
            s = _dot_nt(ka, qs_ref[h])
            if masked:
                s = jnp.where(keep, s, NEG_BIG)
            m_prev = m_ref[h]
            m_new = jnp.maximum(m_prev, jnp.max(s, axis=0, keepdims=True))
            alpha = jnp.exp2(m_prev - m_new)
            p = jnp.exp2(s - m_new)
            l_ref[h] = alpha * l_ref[h] + jnp.sum(p, axis=0, keepdims=True)
            acc_ref[h] = alpha * acc_ref[h] + _dot(vt, p.astype(BF16))
            m_ref[h] = m_new

    @pl.when(ki < qi)
    def _():
        step(False)

    @pl.when(ki == qi)
    def _():
        step(True)
        o0 = acc_ref[0] / l_ref[0]
        o1 = acc_ref[1] / l_ref[1]
        o_t = jnp.concatenate([o0[:HEAD_DIM], o1[HEAD_DIM:]], axis=0)
        o_ref[...] = o_t.T.astype(o_ref.dtype)


def _fox_attention(q, qb, k, kb, vt, batch, seq):
    t = min(ATTN_TILE, seq)
    n = seq // t
    pairs = FOX_HEADS // 2
    qspec = pl.BlockSpec((t, LANES), lambda bi, p, i, j: (bi * n + i, p))
    kspec = pl.BlockSpec((t, LANES), lambda bi, p, i, j: (bi * n + jnp.minimum(j, i), p))
    vspec = pl.BlockSpec((LANES, t), lambda bi, p, i, j: (p, bi * n + jnp.minimum(j, i)))
    return pl.pallas_call(
        _fox_kernel,
        grid=(batch, pairs, n, n),
        in_specs=[qspec, qspec, kspec, kspec, vspec],
        out_specs=qspec,
        out_shape=jax.ShapeDtypeStruct((batch * seq, FOX_WIDTH), BF16),
        scratch_shapes=[pltpu.VMEM((2, t, 2 * LANES), BF16), pltpu.VMEM((2, 1, t), F32),
                        pltpu.VMEM((2, 1, t), F32), pltpu.VMEM((2, LANES, t), F32)],
        compiler_params=_cparams(("arbitrary",) * 4),
        name="fox_attention",
    )(q, qb, k, kb, vt)


def _ssd_kernel(xbc_ref, prev_ref, fd_ref, dtt_ref, z_ref, cw_ref, cb_ref, alog_ref, alogc_ref,
                dsk_ref, nrm_ref, tri_ref, y_ref, state_ref):
    c_idx = pl.program_id(1)
    L = SSD_CHUNK
    P = HEAD_DIM
    N = SSD_STATE

    @pl.when(c_idx == 0)
    def _():
        state_ref[...] = jnp.zeros_like(state_ref)

    cur = xbc_ref[...]
    prev = jnp.where(c_idx > 0, prev_ref[...], 0.0)
    ext = jnp.concatenate([prev, cur], axis=0)
    cw = cw_ref[...]
    conv = cb_ref[...] + cw[SSD_CONV - 1:SSD_CONV, :] * cur
    for kk in range(SSD_CONV - 1):
        sh = SSD_CONV - 1 - kk
        conv = conv + cw[kk:kk + 1, :] * ext[8 - sh:8 - sh + L, :]
    xc = _silu(conv)
    xs = xc[:, :SSD_INNER]
    bm = xc[:, SSD_INNER:SSD_INNER + SSD_GROUPS * N]
    cm = xc[:, SSD_INNER + SSD_GROUPS * N:]

    dt = fd_ref[:, FOX_HEADS:FOX_HEADS + SSD_HEADS]
    a_row = -jnp.exp(alog_ref[...])
    a_col = -jnp.exp(alogc_ref[...])
    tri = tri_ref[...]
    cum = _dot_exact_lhs(tri, dt * a_row, 2)
    cum_t = lax.dot_general(_split_bf16(dtt_ref[...] * a_col, 2)[0], tri, (((1,), (1,)), ((), ())),
                            preferred_element_type=F32)
    cum_t = cum_t + lax.dot_general(_split_bf16(dtt_ref[...] * a_col, 2)[1], tri,
                                    (((1,), (1,)), ((), ())), preferred_element_type=F32)
    row_i = lax.broadcasted_iota(jnp.int32, (L, L), 0)
    col_i = lax.broadcasted_iota(jnp.int32, (L, L), 1)
    causal = row_i >= col_i

    ys = []
    for g in range(SSD_GROUPS):
        bm_g = bm[:, g * N:(g + 1) * N]
        cm_g = cm[:, g * N:(g + 1) * N]
        cb_g = _dot_nt(cm_g.astype(BF16), bm_g.astype(BF16))
        bm_t = bm_g.T
        for hh in range(SSD_HEADS // SSD_GROUPS):
            h = g * (SSD_HEADS // SSD_GROUPS) + hh
            cum_c = cum[:, h:h + 1]
            cum_r = cum_t[h:h + 1, :]
            seg = jnp.where(causal, jnp.exp(jnp.minimum(cum_c - cum_r, 0.0)), 0.0)
            x_h = xs[:, h * P:(h + 1) * P]
            xdt = (x_h * dt[:, h:h + 1]).astype(BF16)
            y_h = _dot((cb_g * seg).astype(BF16), xdt)
            h_prev = state_ref[h]
            y_h = y_h + _dot((cm_g * jnp.exp(cum_c)).astype(BF16), h_prev.astype(BF16))
            last = cum_t[h:h + 1, L - 1:L]
            dte = jnp.exp(last - cum_r)
            st = _dot((bm_t * dte).astype(BF16), xdt)
            state_ref[h] = jnp.exp(last) * h_prev + st
            ys.append(y_h)
    y = jnp.concatenate(ys, axis=1) + xs * dsk_ref[...]
    y = y * _silu(z_ref[...].astype(F32))
    gw = SSD_INNER // SSD_GROUPS
    outs = []
    for g in range(SSD_GROUPS):
        yg = y[:, g * gw:(g + 1) * gw]
        outs.append(yg * lax.rsqrt(jnp.mean(yg * yg, axis=-1, keepdims=True) + RMS_EPS))
    y_ref[...] = (jnp.concatenate(outs, axis=1) * nrm_ref[...]).astype(y_ref.dtype)


def _ssd(xbc, fd, dtt, z, conv_w, conv_b, a_log, d_skip_c, ssd_norm, tri, batch, seq):
    L = SSD_CHUNK
    nc = seq // L
    t_rows = batch * seq
    row = lambda w: pl.BlockSpec((L, w), lambda b, c: (b * nc + c, 0))
    full = lambda a: pl.BlockSpec(a.shape, lambda b, c: (0,) * a.ndim)
    prev = pl.BlockSpec((8, SSD_CONV_DIM), lambda b, c: (jnp.maximum((b * nc + c) * (L // 8) - 1, 0), 0))
    return pl.pallas_call(
        _ssd_kernel,
        grid=(batch, nc),
        in_specs=[row(SSD_CONV_DIM), prev, row(LANES),
                  pl.BlockSpec((None, SSD_HEADS, L), lambda b, c: (b, 0, c)),
                  row(SSD_INNER), full(conv_w), full(conv_b), full(a_log[None, :]),
                  full(a_log[:, None]), full(d_skip_c), full(ssd_norm), full(tri)],
        out_specs=row(SSD_INNER),
        out_shape=jax.ShapeDtypeStruct((t_rows, SSD_INNER), BF16),
        scratch_shapes=[pltpu.VMEM((SSD_HEADS, SSD_STATE, HEAD_DIM), F32)],
        compiler_params=_cparams(("arbitrary", "arbitrary")),
        name="ssd_scan",
    )(xbc, xbc, fd, dtt, z, conv_w, conv_b, a_log[None, :], a_log[:, None], d_skip_c, ssd_norm, tri)


def _swiglu_chunk(h, w13_ref, w2_ref, c, tf):
    f = w13_ref.shape[1] // 2
    off = pl.multiple_of(c * tf, tf)
    gate = _dot(h, w13_ref[:, pl.ds(off, tf)])
    up = _dot(h, w13_ref[:, pl.ds(f + off, tf)])
    return _dot((_silu(gate) * up).astype(BF16), w2_ref[pl.ds(off, tf), :])


def _even_out_kernel(o_ref, y_ref, x_ref, wo_ref, g_ref, w13_ref, w2_ref, out_ref, acc_ref):
    w = FOX_WIDTH
    x1 = x_ref[...] + _dot(o_ref[...], wo_ref[:w, :]) + _dot(y_ref[...], wo_ref[w:, :])
    h = _rms(x1, g_ref[...]).astype(BF16)
    acc_ref[...] = x1

    def body(c, carry):
        acc_ref[...] += _swiglu_chunk(h, w13_ref, w2_ref, c, FFN_CHUNK)
        return carry

    lax.fori_loop(0, w2_ref.shape[0] // FFN_CHUNK, body, 0)
    out_ref[...] = acc_ref[...]


def _even_out_ffn(o_fox, y_ssd, x2d, wo, g, w13, w2):
    t_rows = x2d.shape[0]
    tm = min(ROW_TILE, t_rows)
    row = lambda w: pl.BlockSpec((tm, w), lambda i: (i, 0))
    full = lambda a: pl.BlockSpec(a.shape, lambda i: (0,) * a.ndim, pipeline_mode=pl.Buffered(1))
    return pl.pallas_call(
        _even_out_kernel,
        grid=(t_rows // tm,),
        in_specs=[row(FOX_WIDTH), row(SSD_INNER), row(D_MODEL), full(wo), full(g), full(w13), full(w2)],
        out_specs=row(D_MODEL),
        out_shape=jax.ShapeDtypeStruct((t_rows, D_MODEL), F32),
        scratch_shapes=[pltpu.VMEM((tm, D_MODEL), F32)],
        compiler_params=_cparams(("arbitrary",)),
        name="even_out_ffn",
    )(o_fox, y_ssd, x2d, wo, g, w13, w2)


def _rope(x, cos, sin_signed):
    n = x.shape[1]
    lane = lax.broadcasted_iota(jnp.int32, x.shape, 1)
    first = (lane % HEAD_DIM) < (HEAD_DIM // 2)
    rot = jnp.where(first, pltpu.roll(x, n - HEAD_DIM // 2, 1), pltpu.roll(x, HEAD_DIM // 2, 1))
    reps = n // LANES
    cos_t = jnp.concatenate([cos] * reps, axis=1)
    sin_t = jnp.concatenate([sin_signed] * reps, axis=1)
    return x * cos_t + rot * sin_t


def _odd_qkv_kernel(x_ref, g_ref, w_ref, qg_ref, kg_ref, hm_ref, cos_ref, sin_ref, q_ref, k_ref, v_ref):
    h = _rms(x_ref[...], g_ref[...]).astype(BF16)
    hm = hm_ref[...]
    w = DIL_WIDTH
    cos = cos_ref[...]
    sin = sin_ref[...]
    q = _head_norm(_dot(h, w_ref[:, :w]), hm, qg_ref[...])
    q_ref[...] = (_rope(q, cos, sin) * (HEAD_DIM ** -0.5)).astype(BF16)
    k = _head_norm(_dot(h, w_ref[:, w:2 * w]), hm, kg_ref[...])
    k_ref[...] = _rope(k, cos, sin).astype(BF16)
    v_ref[...] = _dot(h, w_ref[:, 2 * w:]).astype(BF16)


def _odd_qkv(x2d, seq, g, w, qg, kg, hm, cos, sin):
    t_rows = x2d.shape[0]
    tm = min(ROW_TILE, seq)
    per_seq = seq // tm
    row = lambda wd: pl.BlockSpec((tm, wd), lambda i: (i, 0))
    full = lambda a: pl.BlockSpec(a.shape, lambda i: (0,) * a.ndim)
    tab = pl.BlockSpec((tm, LANES), lambda i: (i % per_seq, 0))
    return pl.pallas_call(
        _odd_qkv_kernel,
        grid=(t_rows // tm,),
        in_specs=[row(D_MODEL), full(g), full(w), full(qg), full(kg), full(hm), tab, tab],
        out_specs=[row(DIL_WIDTH)] * 3,
        out_shape=[jax.ShapeDtypeStruct((t_rows, DIL_WIDTH), BF16)] * 3,
        compiler_params=_cparams(("arbitrary",)),
        name="odd_qkv",
    )(x2d, g, w, qg, kg, hm, cos, sin)


def _dilated_kernel(q_ref, kp_ref, kc_ref, vp_ref, vc_ref, band_ref, o_ref, lse_ref):
    j = pl.program_id(2)
    blk = DIL_BLOCK
    k2 = jnp.concatenate([kp_ref[...], kc_ref[...]], axis=0)
    v2 = jnp.concatenate([vp_ref[...], vc_ref[...]], axis=0)
    col = lax.broadcasted_iota(jnp.int32, (blk, 2 * blk), 1)
    bias = jnp.where((j == 0) & (col < blk), NEG_BIG, band_ref[...])
    lane = lax.broadcasted_iota(jnp.int32, (blk, LANES), 1)
    lse_acc = jnp.zeros((blk, LANES), F32)
    for pr in range(DIL_HEADS // 2):
        sl = slice(pr * LANES, (pr + 1) * LANES)
        q2 = q_ref[:, sl]
        kk = k2[:, sl]
        vv = v2[:, sl]
        halves = []
        for h in range(2):
            qm = jnp.where((lane >= h * HEAD_DIM) & (lane < (h + 1) * HEAD_DIM), q2, jnp.zeros_like(q2))
            s = _dot_nt(qm, kk) + bias
            m = jnp.max(s, axis=-1, keepdims=True)
            e = jnp.exp(s - m)
            den = jnp.sum(e, axis=-1, keepdims=True)
            halves.append(_dot(e.astype(BF16), vv) / den)
            lse_acc = jnp.where(lane == 2 * pr + h, m + jnp.log(den), lse_acc)
        o_ref[:, sl] = jnp.where(lane < HEAD_DIM, halves[0], halves[1]).astype(o_ref.dtype)
    lse_ref[...] = lse_acc


def _dilated_branch(q, k, v, band, dilation):
    b, s, w = q.shape
    d = dilation
    n_sub = s // d
    nb = n_sub // DIL_BLOCK
    view = lambda t: t.reshape(b, n_sub, d * w)
    cur = pl.BlockSpec((None, DIL_BLOCK, w), lambda bi, r, j: (bi, j, r))
    prev = pl.BlockSpec((None, DIL_BLOCK, w), lambda bi, r, j: (bi, jnp.maximum(j - 1, 0), r))
    lse_spec = pl.BlockSpec((None, DIL_BLOCK, LANES), lambda bi, r, j: (bi, j, r))
    o, lse = pl.pallas_call(
        _dilated_kernel,
        grid=(b, d, nb),
        in_specs=[cur, prev, cur, prev, cur, pl.BlockSpec(band.shape, lambda bi, r, j: (0, 0))],
        out_specs=[cur, lse_spec],
        out_shape=[jax.ShapeDtypeStruct((b, n_sub, d * w), BF16),
                   jax.ShapeDtypeStruct((b, n_sub, d * LANES), F32)],
        compiler_params=_cparams(("arbitrary",) * 3),
        name=f"dilated_attn_d{d}",
    )(view(q), view(k), view(k), view(v), view(v), band)
    return o.reshape(b * s, w), lse.reshape(b * s, LANES)


def _odd_out_kernel(tiles_total, o1_ref, o2_ref, o3_ref, l1_ref, l2_ref, l3_ref, x_ref, wo_ref, g_ref,
                    rw_ref, ex_ref, tril_ref, x1_ref, h_ref, info_ref, cnt_ref, carry_ref):
    i = pl.program_id(0)

    @pl.when(i == 0)
    def _():
        carry_ref[...] = jnp.zeros_like(carry_ref)

    l1, l2, l3 = l1_ref[...], l2_ref[...], l3_ref[...]
    mx = jnp.maximum(jnp.maximum(l1, l2), l3)
    a1, a2, a3 = jnp.exp(l1 - mx), jnp.exp(l2 - mx), jnp.exp(l3 - mx)
    inv = 1.0 / (a1 + a2 + a3)
    ex = ex_ref[...]
    o = (_dot_exact_rhs(a1 * inv, ex, 2) * o1_ref[...].astype(F32)
         + _dot_exact_rhs(a2 * inv, ex, 2) * o2_ref[...].astype(F32)
         + _dot_exact_rhs(a3 * inv, ex, 2) * o3_ref[...].astype(F32))
    x1 = x_ref[...] + _dot(o.astype(BF16), wo_ref[...])
    x1_ref[...] = x1
    h = _rms(x1, g_ref[...])
    h_ref[...] = h

    logits = jnp.dot(h, rw_ref[...], preferred_element_type=F32, precision=lax.Precision.HIGHEST)
    lane = lax.broadcasted_iota(jnp.int32, logits.shape, 1)
    logits = jnp.where(lane < N_EXPERTS, logits, -jnp.inf)
    m1 = jnp.max(logits, axis=-1, keepdims=True)
    i1 = jnp.min(jnp.where(logits == m1, lane, LANES), axis=-1, keepdims=True)
    rest = jnp.where(lane == i1, -jnp.inf, logits)
    m2 = jnp.max(rest, axis=-1, keepdims=True)
    i2 = jnp.min(jnp.where(rest == m2, lane, LANES), axis=-1, keepdims=True)
    e2 = jnp.exp(m2 - m1)
    g1 = 1.0 / (1.0 + e2)
    g2 = e2 * g1
    hot1 = lane == i1
    hot2 = lane == i2
    onehot = jnp.where(hot1 | hot2, 1.0, 0.0).astype(BF16)
    before = _dot(tril_ref[...], onehot) + carry_ref[...]
    r1 = jnp.sum(jnp.where(hot1, before, 0.0), axis=-1, keepdims=True)
    r2 = jnp.sum(jnp.where(hot2, before, 0.0), axis=-1, keepdims=True)
    total = before[-1:, :] + onehot[-1:, :].astype(F32)
    carry_ref[...] = total
    cnt_ref[...] = jnp.broadcast_to(total, cnt_ref.shape)
    info = jnp.where(lane == 0, i1.astype(F32), 0.0)
    info = jnp.where(lane == 1, i2.astype(F32), info)
    info = jnp.where(lane == 2, g1, info)
    info = jnp.where(lane == 3, g2, info)
    info = jnp.where(lane == 4, r1, info)
    info = jnp.where(lane == 5, r2, info)
    info_ref[...] = info


def _odd_out_router(o1, o2, o3, l1, l2, l3, x2d, wo, g, rw, ex, tril_strict):
    t_rows = x2d.shape[0]
    tm = min(ROW_TILE, t_rows)
    n = t_rows // tm
    row = lambda w: pl.BlockSpec((tm, w), lambda i: (i, 0))
    full = lambda a: pl.BlockSpec(a.shape, lambda i: (0,) * a.ndim)
    return pl.pallas_call(
        functools.partial(_odd_out_kernel, n),
        grid=(n,),
        in_specs=[row(DIL_WIDTH)] * 3 + [row(LANES)] * 3 + [row(D_MODEL), full(wo), full(g), full(rw),
                                                             full(ex), full(tril_strict)],
        out_specs=[row(D_MODEL), row(D_MODEL), row(LANES), pl.BlockSpec((8, LANES), lambda i: (0, 0))],
        out_shape=[jax.ShapeDtypeStruct((t_rows, D_MODEL), F32), jax.ShapeDtypeStruct((t_rows, D_MODEL), F32),
                   jax.ShapeDtypeStruct((t_rows, LANES), F32), jax.ShapeDtypeStruct((8, LANES), F32)],
        scratch_shapes=[pltpu.VMEM((1, LANES), F32)],
        compiler_params=_cparams(("arbitrary",)),
        name="odd_out_router",
    )(o1, o2, o3, l1, l2, l3, x2d, wo, g, rw, ex, tril_strict)


def _moe_kernel(t_rows, te_ref, nu_ref, code_cur_ref, code_nxt_ref, h_hbm, w13_ref, w2_ref,
                y_hbm, xbuf, ybuf, gsem, ssem):
    i = pl.program_id(0)
    n_used = nu_ref[0]
    gm = xbuf.shape[1]
    slot = i % 2
    out_half = y_hbm.shape[0] // 2

    def gather(code_ref, s):
        def body(r, carry):
            code = code_ref[0, 0, r]
            tok = jnp.where(code < 0, 0, jnp.where(code >= t_rows, code - t_rows, code))
            pltpu.make_async_copy(h_hbm.at[pl.ds(tok, 1)], xbuf.at[s, pl.ds(r, 1)], gsem.at[s]).start()
            return carry
        lax.fori_loop(0, gm, body, 0)

    def wait_rows(buf, sem, s):
        pltpu.make_async_copy(h_hbm.at[pl.ds(0, gm)], buf.at[s], sem.at[s]).wait()

    @pl.when(i == 0)
    def _():
        gather(code_cur_ref, 0)

    @pl.when(i + 1 < n_used)
    def _():
        gather(code_nxt_ref, 1 - slot)

    @pl.when(i < n_used)
    def _():
        wait_rows(xbuf, gsem, slot)

        @pl.when(i >= 2)
        def _():
            wait_rows(ybuf, ssem, slot)

        x = xbuf[slot].astype(BF16)
        ybuf[slot] = jnp.zeros(ybuf.shape[1:], F32)

        def body(c, carry):
            ybuf[slot] += _swiglu_chunk(x, w13_ref, w2_ref, c, MOE_CHUNK)
            return carry

        lax.fori_loop(0, w2_ref.shape[0] // MOE_CHUNK, body, 0)

        spare = t_rows + te_ref[i] * gm

        def scatter(r, carry):
            code = code_cur_ref[0, 0, r]
            row = jnp.where(code < 0, spare + r,
                            jnp.where(code >= t_rows, code - t_rows + out_half, code))
            pltpu.make_async_copy(ybuf.at[slot, pl.ds(r, 1)], y_hbm.at[pl.ds(row, 1)], ssem.at[slot]).start()
            return carry
        lax.fori_loop(0, gm, scatter, 0)

    @pl.when(i == pl.num_programs(0) - 1)
    def _():
        last = n_used - 1

        @pl.when(n_used >= 2)
        def _():
            wait_rows(ybuf, ssem, (last - 1) % 2)

        @pl.when(n_used >= 1)
        def _():
            wait_rows(ybuf, ssem, last % 2)


def _moe_ffn(h2d, w13, w2, tile_expert, n_used, codes, out_rows):
    nt = tile_expert.shape[0]
    gm = codes.shape[-1]
    grid_spec = pltpu.PrefetchScalarGridSpec(
        num_scalar_prefetch=2,
        grid=(nt,),
        in_specs=[
            pl.BlockSpec((1, 1, gm), lambda i, te, nu: (i, 0, 0), memory_space=pltpu.SMEM),
            pl.BlockSpec((1, 1, gm), lambda i, te, nu: (jnp.minimum(i + 1, nt - 1), 0, 0), memory_space=pltpu.SMEM),
            pl.BlockSpec(memory_space=pl.ANY),
            pl.BlockSpec((None,) + w13.shape[1:], lambda i, te, nu: (te[i], 0, 0), pipeline_mode=pl.Buffered(1)),
            pl.BlockSpec((None,) + w2.shape[1:], lambda i, te, nu: (te[i], 0, 0), pipeline_mode=pl.Buffered(1)),
        ],
        out_specs=pl.BlockSpec(memory_space=pl.ANY),
        scratch_shapes=[pltpu.VMEM((2, gm, D_MODEL), F32), pltpu.VMEM((2, gm, D_MODEL), F32),
                        pltpu.SemaphoreType.DMA((2,)), pltpu.SemaphoreType.DMA((2,))],
    )
    return pl.pallas_call(
        functools.partial(_moe_kernel, h2d.shape[0]),
        grid_spec=grid_spec,
        out_shape=jax.ShapeDtypeStruct((out_rows, D_MODEL), F32),
        compiler_params=_cparams(("arbitrary",)),
        name="moe_ffn",
    )(tile_expert, n_used, codes, codes, h2d, w13, w2)


def _combine_kernel(x_ref, y0_ref, y1_ref, info_ref, o_ref):
    o_ref[...] = x_ref[...] + info_ref[:, 2:3] * y0_ref[...] + info_ref[:, 3:4] * y1_ref[...]


def _combine(x1, y, info):
    t_rows = x1.shape[0]
    tm = min(ROW_TILE, t_rows)
    row = lambda w: pl.BlockSpec((tm, w), lambda i: (i, 0))
    ysp = lambda kk: pl.BlockSpec((None, tm, D_MODEL), lambda i: (kk, i, 0))
    return pl.pallas_call(
        _combine_kernel,
        grid=(t_rows // tm,),
        in_specs=[row(D_MODEL), ysp(0), ysp(1), row(LANES)],
        out_specs=row(D_MODEL),
        out_shape=jax.ShapeDtypeStruct((t_rows, D_MODEL), F32),
        compiler_params=_cparams(("arbitrary",)),
        name="moe_combine",
    )(x1, y, y, info)


def _head_mean_matrix(width):
    idx = np.arange(width) // HEAD_DIM
    return jnp.asarray((idx[:, None] == idx[None, :]).astype(np.float32) / HEAD_DIM, dtype=BF16)


def _tril(n, strict):
    r = np.arange(n)
    m = (r[None, :] < r[:, None]) if strict else (r[None, :] <= r[:, None])
    return jnp.asarray(m.astype(np.float32), dtype=BF16)


def _rope_tables(seq):
    half = HEAD_DIM // 2
    inv_freq = ROPE_THETA ** (-jnp.arange(half, dtype=F32) / half)
    ang = jnp.arange(seq, dtype=F32)[:, None] * inv_freq[None, :]
    cos = jnp.cos(ang)
    sin = jnp.sin(ang)
    cos_t = jnp.concatenate([cos, cos, cos, cos], axis=1)
    sin_t = jnp.concatenate([-sin, sin, -sin, sin], axis=1)
    return cos_t, sin_t


def _band_bias():
    qi = np.arange(DIL_BLOCK)[:, None]
    ki = np.arange(-DIL_BLOCK, DIL_BLOCK)[None, :]
    rel = qi - ki
    return jnp.asarray(np.where((rel >= 0) & (rel <= DIL_BLOCK), 0.0, NEG_BIG).astype(np.float32))


def _head_expand():
    m = np.zeros((LANES, DIL_WIDTH), np.float32)
    for h in range(DIL_HEADS):
        m[h, h * HEAD_DIM:(h + 1) * HEAD_DIM] = 1.0
    return jnp.asarray(m, dtype=BF16)


def _bias_placement():
    pq = np.zeros((BIAS_PARTS, LANES, FOX_WIDTH), np.float32)
    pk = np.zeros((BIAS_PARTS, LANES, FOX_WIDTH), np.float32)
    qone = np.zeros((1, FOX_WIDTH), np.float32)
    kone = np.zeros((1, FOX_WIDTH), np.float32)
    for head in range(FOX_HEADS):
        base = (head // 2) * LANES + (head % 2) * BIAS_LANES
        for j in range(BIAS_PARTS):
            pq[j, head, base + j] = 1.0
            pk[j, head, base + BIAS_PARTS + j] = -1.0
            qone[0, base + BIAS_PARTS + j] = 1.0
            kone[0, base + j] = 1.0
    return (jnp.asarray(pq, dtype=BF16), jnp.asarray(pk, dtype=BF16), jnp.asarray(qone), jnp.asarray(kone))


def _even_layer(x2d, batch, seq, norm1, w_in, fox_qn, fox_kn, fox_fbias, conv_w, conv_b, dt_bias, a_log,
                d_skip, ssd_norm, w_out, norm2, ffn_w13, ffn_w2):
    w = FOX_WIDTH
    o_z = 3 * w + FOX_HEADS
    o_x = o_z + SSD_INNER
    o_dt = o_x + SSD_CONV_DIM
    wqk = w_in[:, :2 * w].astype(BF16)
    wvt = w_in[:, 2 * w:3 * w].T.astype(BF16)
    wzx = w_in[:, o_z:o_dt].astype(BF16)
    pad = LANES - FOX_HEADS - SSD_HEADS
    wfd = jnp.concatenate([w_in[:, 3 * w:o_z], w_in[:, o_dt:], jnp.zeros((D_MODEL, pad), F32)], axis=1).astype(BF16)
    fdb = jnp.concatenate([fox_fbias, dt_bias, jnp.zeros((pad,), F32)])[None, :]
    tm = min(ROW_TILE, seq)
    q, qb, k, kb, vt, z, xbc, fd = _even_in_proj(
        x2d, seq, norm1[None, :], wqk, wvt, wzx, wfd, fox_qn.reshape(1, w), fox_kn.reshape(1, w), fdb,
        _head_mean_matrix(w), _tril(tm, False), *_bias_placement())

    o_fox = _fox_attention(q, qb, k, kb, vt, batch, seq)

    dtt = jnp.transpose(fd[:, FOX_HEADS:FOX_HEADS + SSD_HEADS].reshape(batch, seq, SSD_HEADS), (0, 2, 1))
    d_skip_c = jnp.repeat(d_skip, HEAD_DIM)[None, :]
    y = _ssd(xbc, fd, dtt, z, conv_w, conv_b[None, :], a_log, d_skip_c, ssd_norm[None, :],
             _tril(SSD_CHUNK, False), batch, seq)

    return _even_out_ffn(o_fox, y, x2d, w_out.astype(BF16), norm2[None, :],
                         ffn_w13.astype(BF16), ffn_w2.astype(BF16))


def _moe_tables(info, counts, t_rows, gm):
    e1 = info[:, 0].astype(jnp.int32)
    e2 = info[:, 1].astype(jnp.int32)
    r1 = info[:, 4].astype(jnp.int32)
    r2 = info[:, 5].astype(jnp.int32)
    cnt = counts[0, :N_EXPERTS].astype(jnp.int32)
    tiles = (cnt + gm - 1) // gm
    tile_end = jnp.cumsum(tiles)
    offs = (tile_end - tiles) * gm
    nt = (2 * t_rows) // gm + N_EXPERTS
    tile_idx = jnp.arange(nt, dtype=jnp.int32)
    tile_expert = jnp.minimum(jnp.sum((tile_idx[:, None] >= tile_end[None, :]).astype(jnp.int32), axis=1),
                              N_EXPERTS - 1)
    n_used = tile_end[-1:].astype(jnp.int32)
    slots = jnp.concatenate([offs[e1] + r1, offs[e2] + r2])
    codes = jnp.full((nt * gm,), -1, jnp.int32).at[slots].set(
        jnp.arange(2 * t_rows, dtype=jnp.int32), unique_indices=True)
    out_half = t_rows + N_EXPERTS * gm
    return tile_expert, n_used, codes.reshape(nt, 1, gm), out_half


def _odd_layer(x2d, batch, seq, norm1, w_qkv, qn, kn, w_out, norm2, router, moe_w13, moe_w2):
    t_rows = batch * seq
    cos, sin = _rope_tables(seq)
    q, k, v = _odd_qkv(x2d, seq, norm1[None, :], w_qkv.astype(BF16), qn.reshape(1, DIL_WIDTH),
                       kn.reshape(1, DIL_WIDTH), _head_mean_matrix(DIL_WIDTH), cos, sin)
    shp = (batch, seq, DIL_WIDTH)
    band = _band_bias()
    outs, lses = zip(*[_dilated_branch(q.reshape(shp), k.reshape(shp), v.reshape(shp), band, d)
                       for (_, d) in DIL_PATTERNS])
    rw = jnp.concatenate([router, jnp.zeros((D_MODEL, LANES - N_EXPERTS), F32)], axis=1)
    tm = min(ROW_TILE, t_rows)
    x1, h, info, counts = _odd_out_router(*outs, *lses, x2d, w_out.astype(BF16), norm2[None, :], rw,
                                          _head_expand(), _tril(tm, True))
    gm = min(MOE_TILE, t_rows)
    tile_expert, n_used, codes, out_half = _moe_tables(info, counts, t_rows, gm)
    y = _moe_ffn(h, moe_w13.astype(BF16), moe_w2.astype(BF16), tile_expert, n_used, codes, 2 * out_half)
    return _combine(x1, y.reshape(2, out_half, D_MODEL), info)


def kernel(x, e_norm1, e_w_in, e_fox_qn, e_fox_kn, e_fox_fbias, e_conv_w, e_conv_b, e_dt_bias, e_a_log, e_d_skip, e_ssd_norm, e_w_out, e_norm2, e_ffn_w13, e_ffn_w2, o_norm1, o_w_qkv, o_qn, o_kn, o_w_out, o_norm2, o_router, o_moe_w13, o_moe_w2):
    batch, seq, _ = x.shape
    depth = e_norm1.shape[0] + o_norm1.shape[0]
    x2d = x.reshape(batch * seq, D_MODEL)
    for i in range(depth):
        j = i // 2
        if i % 2 == 0:
            x2d = _even_layer(x2d, batch, seq, e_norm1[j], e_w_in[j], e_fox_qn[j], e_fox_kn[j], e_fox_fbias[j],
                              e_conv_w[j], e_conv_b[j], e_dt_bias[j], e_a_log[j], e_d_skip[j], e_ssd_norm[j],
                              e_w_out[j], e_norm2[j], e_ffn_w13[j], e_ffn_w2[j])
        else:
            x2d = _odd_layer(x2d, batch, seq, o_norm1[j], o_w_qkv[j], o_qn[j], o_kn[j], o_w_out[j], o_norm2[j],
                             o_router[j], o_moe_w13[j], o_moe_w2[j])
    return x2d.reshape(batch, seq, D_MODEL)
```

```python
import functools
import math

import jax
import jax.numpy as jnp
import numpy as np
from jax import lax
from jax.experimental import pallas as pl
from jax.experimental.pallas import tpu as pltpu

F32 = jnp.float32
BF16 = jnp.bfloat16

D_MODEL = 1024
HEAD_DIM = 64
RMS_EPS = 1e-6
ROPE_THETA = 10000.0
FOX_HEADS = 8
FOX_WIDTH = FOX_HEADS * HEAD_DIM
SSD_HEADS = 8
SSD_INNER = 512
SSD_GROUPS = 2
SSD_STATE = 128
SSD_CONV = 4
SSD_CHUNK = 128
SSD_CONV_DIM = SSD_INNER + 2 * SSD_GROUPS * SSD_STATE
DIL_HEADS = 16
DIL_WIDTH = DIL_HEADS * HEAD_DIM
DIL_PATTERNS = ((128, 1), (512, 4), (2048, 16))
DIL_BLOCK = 128
FFN_DIM = 2816
N_EXPERTS = 8
EXPERT_DIM = 3584

LANES = 128
TILE_SUBLANES = D_MODEL // LANES
NEG_BIG = -1e30
LOG2E = math.log2(math.e)
BIAS_PARTS = 3
BIAS_LANES = 2 * BIAS_PARTS
VMEM_LIMIT = 56 * 1024 * 1024

ROW_TILE = 512
ATTN_TILE = 512
FFN_CHUNK = 256
MOE_CHUNK = 256
MOE_UNROLL = 2
MOE_TILE = 512


def _cparams(sem):
    return pltpu.CompilerParams(dimension_semantics=sem, vmem_limit_bytes=VMEM_LIMIT)


def _softplus_parts(x):
    return jnp.log(1.0 + jnp.exp(-jnp.abs(x)))


def _split_bf16(a, parts):
    out = []
    r = a
    for _ in range(parts):
        p = r.astype(BF16)
        out.append(p)
        r = r - p.astype(F32)
    return out


def _dot(a, b):
    return jnp.dot(a, b, preferred_element_type=F32)


def _dot_nt(a, b):
    return lax.dot_general(a, b, (((1,), (1,)), ((), ())), preferred_element_type=F32)


def _dot_exact_rhs(a_f32, b_bf16, parts):
    acc = None
    for p in _split_bf16(a_f32, parts):
        t = _dot(p, b_bf16)
        acc = t if acc is None else acc + t
    return acc


def _dot_exact_lhs(a_bf16, b_f32, parts):
    acc = None
    for p in _split_bf16(b_f32, parts):
        t = _dot(a_bf16, p)
        acc = t if acc is None else acc + t
    return acc


def _rms(x, g):
    return x * lax.rsqrt(jnp.mean(x * x, axis=-1, keepdims=True) + RMS_EPS) * g


def _head_norm(x, hm, gain):
    ms = _dot((x * x).astype(BF16), hm)
    return x * lax.rsqrt(ms + RMS_EPS) * gain


def _silu(x):
    return x * (1.0 / (1.0 + jnp.exp(-x)))


def _even_in_kernel(tiles_per_seq, x_ref, g_ref, wqk_ref, wvt_ref, wzx_ref, wfd_ref, qg_ref, kg_ref,
                    fdb_ref, hm_ref, tril_ref, pq_ref, pk_ref, qone_ref, kone_ref,
                    q_ref, qb_ref, k_ref, kb_ref, vt_ref, z_ref, xbc_ref, fd_ref, carry_ref):
    i = pl.program_id(0)

    @pl.when(i % tiles_per_seq == 0)
    def _():
        carry_ref[...] = jnp.zeros_like(carry_ref)

    h = _rms(x_ref[...], g_ref[...]).astype(BF16)
    qk = _dot(h, wqk_ref[...])
    hm = hm_ref[...]
    w = FOX_WIDTH
    q_ref[...] = (_head_norm(qk[:, :w], hm, qg_ref[...]) * (HEAD_DIM ** -0.5 * LOG2E)).astype(BF16)
    k_ref[...] = _head_norm(qk[:, w:], hm, kg_ref[...]).astype(BF16)
    vt_ref[...] = _dot_nt(wvt_ref[...], h).astype(BF16)
    zx = _dot(h, wzx_ref[...])
    z_ref[...] = zx[:, :SSD_INNER].astype(BF16)
    xbc_ref[...] = zx[:, SSD_INNER:]
    fd = _dot(h, wfd_ref[...]) + fdb_ref[...]
    t = _softplus_parts(fd)
    log_f = jnp.minimum(fd, 0.0) - t
    dt = jnp.maximum(fd, 0.0) + t
    c = _dot_exact_lhs(tril_ref[...], log_f, 2) + carry_ref[...]
    carry_ref[...] = c[-1:, :]
    lane = lax.broadcasted_iota(jnp.int32, fd.shape, 1)
    fd_ref[...] = jnp.where(lane < FOX_HEADS, c, dt)
    qb = qone_ref[...]
    kb = kone_ref[...]
    for j, part in enumerate(_split_bf16(c * LOG2E, 3)):
        qb = qb + _dot(part, pq_ref[j])
        kb = kb + _dot(part, pk_ref[j])
    qb_ref[...] = qb.astype(BF16)
    kb_ref[...] = kb.astype(BF16)


def _even_in_proj(x2d, seq, g, wqk, wvt, wzx, wfd, qg, kg, fdb, hm, tril, pq, pk, qone, kone):
    t_rows = x2d.shape[0]
    tm = min(ROW_TILE, seq)
    n = t_rows // tm
    row = lambda w: pl.BlockSpec((tm, w), lambda i: (i, 0))
    full = lambda a: pl.BlockSpec(a.shape, lambda i: (0,) * a.ndim)
    consts = (g, wqk, wvt, wzx, wfd, qg, kg, fdb, hm, tril, pq, pk, qone, kone)
    return pl.pallas_call(
        functools.partial(_even_in_kernel, seq // tm),
        grid=(n,),
        in_specs=[row(D_MODEL)] + [full(a) for a in consts],
        out_specs=[row(FOX_WIDTH)] * 4 + [pl.BlockSpec((FOX_WIDTH, tm), lambda i: (0, i)),
                                           row(SSD_INNER), row(SSD_CONV_DIM), row(LANES)],
        out_shape=[jax.ShapeDtypeStruct((t_rows, FOX_WIDTH), BF16)] * 4
        + [jax.ShapeDtypeStruct((FOX_WIDTH, t_rows), BF16),
           jax.ShapeDtypeStruct((t_rows, SSD_INNER), BF16),
           jax.ShapeDtypeStruct((t_rows, SSD_CONV_DIM), F32),
           jax.ShapeDtypeStruct((t_rows, LANES), F32)],
        scratch_shapes=[pltpu.VMEM((1, LANES), F32)],
        compiler_params=_cparams(("arbitrary",)),
        name="even_in_proj",
    )(x2d, *consts)


def _fox_kernel(q_ref, qb_ref, k_ref, kb_ref, vt_ref, o_ref, qs_ref, m_ref, l_ref, acc_ref):
    qi = pl.program_id(2)
    ki = pl.program_id(3)
    tq = q_ref.shape[0]
    tk = k_ref.shape[0]

    @pl.when(ki == 0)
    def _():
        m_ref[...] = jnp.full_like(m_ref, NEG_BIG)
        l_ref[...] = jnp.zeros_like(l_ref)
        acc_ref[...] = jnp.zeros_like(acc_ref)
        q2 = q_ref[...]
        qb = qb_ref[...]
        lane = lax.broadcasted_iota(jnp.int32, q2.shape, 1)
        zero = jnp.zeros_like(q2)
        for h in range(2):
            qh = jnp.where((lane >= h * HEAD_DIM) & (lane < (h + 1) * HEAD_DIM), q2, zero)
            bh = jnp.where((lane >= h * BIAS_LANES) & (lane < (h + 1) * BIAS_LANES), qb, zero)
            qs_ref[h] = jnp.concatenate([qh, bh], axis=1)

    def step(masked):
        ka = jnp.concatenate([k_ref[...], kb_ref[...]], axis=1)
        vt = vt_ref[...]
        if masked:
            keep = (lax.broadcasted_iota(jnp.int32, (tk, tq), 0)
                    <= lax.broadcasted_iota(jnp.int32, (tk, tq), 1))
        for h in range(2):
            s = _dot_nt(ka, qs_ref[h])
            if masked:
                s = jnp.where(keep, s, NEG_BIG)
            m_prev = m_ref[h]
            m_new = jnp.maximum(m_prev, jnp.max(s, axis=0, keepdims=True))
            alpha = jnp.exp2(m_prev - m_new)
            p = jnp.exp2(s - m_new)
            l_ref[h] = alpha * l_ref[h] + jnp.sum(p, axis=0, keepdims=True)
            acc_ref[h] = alpha * acc_ref[h] + _dot(vt, p.astype(BF16))
            m_ref[h] = m_new

    @pl.when(ki < qi)
    def _():
        step(False)

    @pl.when(ki == qi)
    def _():
        step(True)
        o0 = acc_ref[0] / l_ref[0]
        o1 = acc_ref[1] / l_ref[1]
        o_t = jnp.concatenate([o0[:HEAD_DIM], o1[HEAD_DIM:]], axis=0)
        o_ref[...] = o_t.T.astype(o_ref.dtype)


def _fox_attention(q, qb, k, kb, vt, batch, seq):
    t = min(ATTN_TILE, seq)
    n = seq // t
    pairs = FOX_HEADS // 2
    qspec = pl.BlockSpec((t, LANES), lambda bi, p, i, j: (bi * n + i, p))
    kspec = pl.BlockSpec((t, LANES), lambda bi, p, i, j: (bi * n + jnp.minimum(j, i), p))
    vspec = pl.BlockSpec((LANES, t), lambda bi, p, i, j: (p, bi * n + jnp.minimum(j, i)))
    return pl.pallas_call(
        _fox_kernel,
        grid=(batch, pairs, n, n),
        in_specs=[qspec, qspec, kspec, kspec, vspec],
        out_specs=qspec,
        out_shape=jax.ShapeDtypeStruct((batch * seq, FOX_WIDTH), BF16),
        scratch_shapes=[pltpu.VMEM((2, t, 2 * LANES), BF16), pltpu.VMEM((2, 1, t), F32),
                        pltpu.VMEM((2, 1, t), F32), pltpu.VMEM((2, LANES, t), F32)],
        compiler_params=_cparams(("arbitrary",) * 4),
        name="fox_attention",
    )(q, qb, k, kb, vt)


def _ssd_kernel(xbc_ref, prev_ref, fd_ref, dtt_ref, z_ref, cw_ref, cb_ref, alog_ref, alogc_ref,
                dsk_ref, nrm_ref, tri_ref, y_ref, state_ref):
    c_idx = pl.program_id(1)
    L = SSD_CHUNK
    P = HEAD_DIM
    N = SSD_STATE

    @pl.when(c_idx == 0)
    def _():
        state_ref[...] = jnp.zeros_like(state_ref)

    cur = xbc_ref[...]
    prev = jnp.where(c_idx > 0, prev_ref[...], 0.0)
    ext = jnp.concatenate([prev, cur], axis=0)
    cw = cw_ref[...]
    conv = cb_ref[...] + cw[SSD_CONV - 1:SSD_CONV, :] * cur
    for kk in range(SSD_CONV - 1):
        sh = SSD_CONV - 1 - kk
        conv = conv + cw[kk:kk + 1, :] * ext[8 - sh:8 - sh + L, :]
    xc = _silu(conv)
    xs = xc[:, :SSD_INNER]
    bm = xc[:, SSD_INNER:SSD_INNER + SSD_GROUPS * N]
    cm = xc[:, SSD_INNER + SSD_GROUPS * N:]

    dt = fd_ref[:, FOX_HEADS:FOX_HEADS + SSD_HEADS]
    a_row = -jnp.exp(alog_ref[...])
    a_col = -jnp.exp(alogc_ref[...])
    tri = tri_ref[...]
    cum = _dot_exact_lhs(tri, dt * a_row, 2)
    cum_t = lax.dot_general(_split_bf16(dtt_ref[...] * a_col, 2)[0], tri, (((1,), (1,)), ((), ())),
                            preferred_element_type=F32)
    cum_t = cum_t + lax.dot_general(_split_bf16(dtt_ref[...] * a_col, 2)[1], tri,
                                    (((1,), (1,)), ((), ())), preferred_element_type=F32)
    row_i = lax.broadcasted_iota(jnp.int32, (L, L), 0)
    col_i = lax.broadcasted_iota(jnp.int32, (L, L), 1)
    causal = row_i >= col_i

    ys = []
    for g in range(SSD_GROUPS):
        bm_g = bm[:, g * N:(g + 1) * N]
        cm_g = cm[:, g * N:(g + 1) * N]
        cb_g = _dot_nt(cm_g.astype(BF16), bm_g.astype(BF16))
        bm_t = bm_g.T
        for hh in range(SSD_HEADS // SSD_GROUPS):
            h = g * (SSD_HEADS // SSD_GROUPS) + hh
            cum_c = cum[:, h:h + 1]
            cum_r = cum_t[h:h + 1, :]
            seg = jnp.where(causal, jnp.exp(jnp.minimum(cum_c - cum_r, 0.0)), 0.0)
            x_h = xs[:, h * P:(h + 1) * P]
            xdt = (x_h * dt[:, h:h + 1]).astype(BF16)
            y_h = _dot((cb_g * seg).astype(BF16), xdt)
            h_prev = state_ref[h]
            y_h = y_h + _dot((cm_g * jnp.exp(cum_c)).astype(BF16), h_prev.astype(BF16))
            last = cum_t[h:h + 1, L - 1:L]
            dte = jnp.exp(last - cum_r)
            st = _dot((bm_t * dte).astype(BF16), xdt)
            state_ref[h] = jnp.exp(last) * h_prev + st
            ys.append(y_h)
    y = jnp.concatenate(ys, axis=1) + xs * dsk_ref[...]
    y = y * _silu(z_ref[...].astype(F32))
    gw = SSD_INNER // SSD_GROUPS
    outs = []
    for g in range(SSD_GROUPS):
        yg = y[:, g * gw:(g + 1) * gw]
        outs.append(yg * lax.rsqrt(jnp.mean(yg * yg, axis=-1, keepdims=True) + RMS_EPS))
    y_ref[...] = (jnp.concatenate(outs, axis=1) * nrm_ref[...]).astype(y_ref.dtype)


def _ssd(xbc, fd, dtt, z, conv_w, conv_b, a_log, d_skip_c, ssd_norm, tri, batch, seq):
    L = SSD_CHUNK
    nc = seq // L
    t_rows = batch * seq
    row = lambda w: pl.BlockSpec((L, w), lambda b, c: (b * nc + c, 0))
    full = lambda a: pl.BlockSpec(a.shape, lambda b, c: (0,) * a.ndim)
    prev = pl.BlockSpec((8, SSD_CONV_DIM), lambda b, c: (jnp.maximum((b * nc + c) * (L // 8) - 1, 0), 0))
    return pl.pallas_call(
        _ssd_kernel,
        grid=(batch, nc),
        in_specs=[row(SSD_CONV_DIM), prev, row(LANES),
                  pl.BlockSpec((None, SSD_HEADS, L), lambda b, c: (b, 0, c)),
                  row(SSD_INNER), full(conv_w), full(conv_b), full(a_log[None, :]),
                  full(a_log[:, None]), full(d_skip_c), full(ssd_norm), full(tri)],
        out_specs=row(SSD_INNER),
        out_shape=jax.ShapeDtypeStruct((t_rows, SSD_INNER), BF16),
        scratch_shapes=[pltpu.VMEM((SSD_HEADS, SSD_STATE, HEAD_DIM), F32)],
        compiler_params=_cparams(("arbitrary", "arbitrary")),
        name="ssd_scan",
    )(xbc, xbc, fd, dtt, z, conv_w, conv_b, a_log[None, :], a_log[:, None], d_skip_c, ssd_norm, tri)


def _swiglu_hidden_chunk(h, w13_ref, act_ref, c, tf):
    f = w13_ref.shape[1] // 2
    off = pl.multiple_of(c * tf, tf)
    gate = _dot(h, w13_ref[:, pl.ds(off, tf)])
    up = _dot(h, w13_ref[:, pl.ds(f + off, tf)])
    act_ref[:, pl.ds(off, tf)] = (_silu(gate) * up).astype(BF16)


def _even_out_kernel(o_ref, y_ref, x_ref, wo_ref, g_ref, w13_ref, w2_ref, out_ref, act_ref):
    w = FOX_WIDTH
    x1 = x_ref[...] + _dot(o_ref[...], wo_ref[:w, :]) + _dot(y_ref[...], wo_ref[w:, :])
    h = _rms(x1, g_ref[...]).astype(BF16)

    def body(c, carry):
        _swiglu_hidden_chunk(h, w13_ref, act_ref, c, FFN_CHUNK)
        return carry

    lax.fori_loop(0, w2_ref.shape[0] // FFN_CHUNK, body, 0)
    out_ref[...] = x1 + _dot(act_ref[...], w2_ref[...])


def _even_out_ffn(o_fox, y_ssd, x2d, wo, g, w13, w2):
    t_rows = x2d.shape[0]
    tm = min(ROW_TILE, t_rows)
    row = lambda w: pl.BlockSpec((tm, w), lambda i: (i, 0))
    full = lambda a: pl.BlockSpec(a.shape, lambda i: (0,) * a.ndim, pipeline_mode=pl.Buffered(1))
    return pl.pallas_call(
        _even_out_kernel,
        grid=(t_rows // tm,),
        in_specs=[row(FOX_WIDTH), row(SSD_INNER), row(D_MODEL), full(wo), full(g), full(w13), full(w2)],
        out_specs=row(D_MODEL),
        out_shape=jax.ShapeDtypeStruct((t_rows, D_MODEL), F32),
        scratch_shapes=[pltpu.VMEM((tm, w2.shape[0]), BF16)],
        compiler_params=_cparams(("arbitrary",)),
        name="even_out_ffn",
    )(o_fox, y_ssd, x2d, wo, g, w13, w2)


def _rope(x, cos, sin_signed):
    n = x.shape[1]
    lane = lax.broadcasted_iota(jnp.int32, x.shape, 1)
    first = (lane % HEAD_DIM) < (HEAD_DIM // 2)
    rot = jnp.where(first, pltpu.roll(x, n - HEAD_DIM // 2, 1), pltpu.roll(x, HEAD_DIM // 2, 1))
    reps = n // LANES
    cos_t = jnp.concatenate([cos] * reps, axis=1)
    sin_t = jnp.concatenate([sin_signed] * reps, axis=1)
    return x * cos_t + rot * sin_t


def _odd_qkv_kernel(x_ref, g_ref, w_ref, qg_ref, kg_ref, hm_ref, cos_ref, sin_ref, q_ref, k_ref, v_ref):
    h = _rms(x_ref[...], g_ref[...]).astype(BF16)
    hm = hm_ref[...]
    w = DIL_WIDTH
    cos = cos_ref[...]
    sin = sin_ref[...]
    q = _head_norm(_dot(h, w_ref[:, :w]), hm, qg_ref[...])
    q_ref[...] = (_rope(q, cos, sin) * (HEAD_DIM ** -0.5)).astype(BF16)
    k = _head_norm(_dot(h, w_ref[:, w:2 * w]), hm, kg_ref[...])
    k_ref[...] = _rope(k, cos, sin).astype(BF16)
    v_ref[...] = _dot(h, w_ref[:, 2 * w:]).astype(BF16)


def _odd_qkv(x2d, seq, g, w, qg, kg, hm, cos, sin):
    t_rows = x2d.shape[0]
    tm = min(ROW_TILE, seq)
    per_seq = seq // tm
    row = lambda wd: pl.BlockSpec((tm, wd), lambda i: (i, 0))
    full = lambda a: pl.BlockSpec(a.shape, lambda i: (0,) * a.ndim)
    tab = pl.BlockSpec((tm, LANES), lambda i: (i % per_seq, 0))
    return pl.pallas_call(
        _odd_qkv_kernel,
        grid=(t_rows // tm,),
        in_specs=[row(D_MODEL), full(g), full(w), full(qg), full(kg), full(hm), tab, tab],
        out_specs=[row(DIL_WIDTH)] * 3,
        out_shape=[jax.ShapeDtypeStruct((t_rows, DIL_WIDTH), BF16)] * 3,
        compiler_params=_cparams(("arbitrary",)),
        name="odd_qkv",
    )(x2d, g, w, qg, kg, hm, cos, sin)


def _dilated_kernel(q_ref, kp_ref, kc_ref, vp_ref, vc_ref, band_ref, o_ref, lse_ref):
    j = pl.program_id(2)
    blk = DIL_BLOCK
    k2 = jnp.concatenate([kp_ref[...], kc_ref[...]], axis=0)
    v2 = jnp.concatenate([vp_ref[...], vc_ref[...]], axis=0)
    col = lax.broadcasted_iota(jnp.int32, (blk, 2 * blk), 1)
    bias = jnp.where((j == 0) & (col < blk), NEG_BIG, band_ref[...])
    lane = lax.broadcasted_iota(jnp.int32, (blk, LANES), 1)
    lse_acc = jnp.zeros((blk, LANES), F32)
    for pr in range(DIL_HEADS // 2):
        sl = slice(pr * LANES, (pr + 1) * LANES)
        q2 = q_ref[:, sl]
        kk = k2[:, sl]
        vv = v2[:, sl]
        halves = []
        for h in range(2):
            qm = jnp.where((lane >= h * HEAD_DIM) & (lane < (h + 1) * HEAD_DIM), q2, jnp.zeros_like(q2))
            s = _dot_nt(qm, kk) + bias
            m = jnp.max(s, axis=-1, keepdims=True)
            e = jnp.exp(s - m)
            den = jnp.sum(e, axis=-1, keepdims=True)
            halves.append(_dot(e.astype(BF16), vv) / den)
            lse_acc = jnp.where(lane == 2 * pr + h, m + jnp.log(den), lse_acc)
        o_ref[:, sl] = jnp.where(lane < HEAD_DIM, halves[0], halves[1]).astype(o_ref.dtype)
    lse_ref[...] = lse_acc


def _dilated_branch(q, k, v, band, dilation):
    b, s, w = q.shape
    d = dilation
    n_sub = s // d
    nb = n_sub // DIL_BLOCK
    view = lambda t: t.reshape(b, n_sub, d * w)
    cur = pl.BlockSpec((None, DIL_BLOCK, w), lambda bi, r, j: (bi, j, r))
    prev = pl.BlockSpec((None, DIL_BLOCK, w), lambda bi, r, j: (bi, jnp.maximum(j - 1, 0), r))
    lse_spec = pl.BlockSpec((None, DIL_BLOCK, LANES), lambda bi, r, j: (bi, j, r))
    o, lse = pl.pallas_call(
        _dilated_kernel,
        grid=(b, d, nb),
        in_specs=[cur, prev, cur, prev, cur, pl.BlockSpec(band.shape, lambda bi, r, j: (0, 0))],
        out_specs=[cur, lse_spec],
        out_shape=[jax.ShapeDtypeStruct((b, n_sub, d * w), BF16),
                   jax.ShapeDtypeStruct((b, n_sub, d * LANES), F32)],
        compiler_params=_cparams(("arbitrary",) * 3),
        name=f"dilated_attn_d{d}",
    )(view(q), view(k), view(k), view(v), view(v), band)
    return o.reshape(b * s, w), lse.reshape(b * s, LANES)


def _odd_out_kernel(tiles_total, o1_ref, o2_ref, o3_ref, l1_ref, l2_ref, l3_ref, x_ref, wo_ref, g_ref,
                    rw_ref, ex_ref, tril_ref, x1_ref, h_ref, info_ref, cnt_ref, carry_ref):
    i = pl.program_id(0)

    @pl.when(i == 0)
    def _():
        carry_ref[...] = jnp.zeros_like(carry_ref)

    l1, l2, l3 = l1_ref[...], l2_ref[...], l3_ref[...]
    mx = jnp.maximum(jnp.maximum(l1, l2), l3)
    a1, a2, a3 = jnp.exp(l1 - mx), jnp.exp(l2 - mx), jnp.exp(l3 - mx)
    inv = 1.0 / (a1 + a2 + a3)
    ex = ex_ref[...]
    o = (_dot_exact_rhs(a1 * inv, ex, 2) * o1_ref[...].astype(F32)
         + _dot_exact_rhs(a2 * inv, ex, 2) * o2_ref[...].astype(F32)
         + _dot_exact_rhs(a3 * inv, ex, 2) * o3_ref[...].astype(F32))
    x1 = x_ref[...] + _dot(o.astype(BF16), wo_ref[...])
    x1_ref[...] = x1
    h = _rms(x1, g_ref[...])
    _wide_to_rows(h_ref, h)

    logits = jnp.dot(h, rw_ref[...], preferred_element_type=F32, precision=lax.Precision.HIGHEST)
    lane = lax.broadcasted_iota(jnp.int32, logits.shape, 1)
    logits = jnp.where(lane < N_EXPERTS, logits, -jnp.inf)
    m1 = jnp.max(logits, axis=-1, keepdims=True)
    i1 = jnp.min(jnp.where(logits == m1, lane, LANES), axis=-1, keepdims=True)
    rest = jnp.where(lane == i1, -jnp.inf, logits)
    m2 = jnp.max(rest, axis=-1, keepdims=True)
    i2 = jnp.min(jnp.where(rest == m2, lane, LANES), axis=-1, keepdims=True)
    e2 = jnp.exp(m2 - m1)
    g1 = 1.0 / (1.0 + e2)
    g2 = e2 * g1
    hot1 = lane == i1
    hot2 = lane == i2
    onehot = jnp.where(hot1 | hot2, 1.0, 0.0).astype(BF16)
    before = _dot(tril_ref[...], onehot) + carry_ref[...]
    r1 = jnp.sum(jnp.where(hot1, before, 0.0), axis=-1, keepdims=True)
    r2 = jnp.sum(jnp.where(hot2, before, 0.0), axis=-1, keepdims=True)
    total = before[-1:, :] + onehot[-1:, :].astype(F32)
    carry_ref[...] = total
    cnt_ref[...] = jnp.broadcast_to(total, cnt_ref.shape)
    info = jnp.where(lane == 0, i1.astype(F32), 0.0)
    info = jnp.where(lane == 1, i2.astype(F32), info)
    info = jnp.where(lane == 2, g1, info)
    info = jnp.where(lane == 3, g2, info)
    info = jnp.where(lane == 4, r1, info)
    info = jnp.where(lane == 5, r2, info)
    info_ref[...] = info


def _odd_out_router(o1, o2, o3, l1, l2, l3, x2d, wo, g, rw, ex, tril_strict):
    t_rows = x2d.shape[0]
    tm = min(ROW_TILE, t_rows)
    n = t_rows // tm
    row = lambda w: pl.BlockSpec((tm, w), lambda i: (i, 0))
    full = lambda a: pl.BlockSpec(a.shape, lambda i: (0,) * a.ndim)
    return pl.pallas_call(
        functools.partial(_odd_out_kernel, n),
        grid=(n,),
        in_specs=[row(DIL_WIDTH)] * 3 + [row(LANES)] * 3 + [row(D_MODEL), full(wo), full(g), full(rw),
                                                             full(ex), full(tril_strict)],
        out_specs=[row(D_MODEL), pl.BlockSpec((tm * TILE_SUBLANES, LANES), lambda i: (i, 0)), row(LANES),
                   pl.BlockSpec((8, LANES), lambda i: (0, 0))],
        out_shape=[jax.ShapeDtypeStruct((t_rows, D_MODEL), F32),
                   jax.ShapeDtypeStruct((t_rows * TILE_SUBLANES, LANES), F32),
                   jax.ShapeDtypeStruct((t_rows, LANES), F32), jax.ShapeDtypeStruct((8, LANES), F32)],
        scratch_shapes=[pltpu.VMEM((1, LANES), F32)],
        compiler_params=_cparams(("arbitrary",)),
        name="odd_out_router",
    )(o1, o2, o3, l1, l2, l3, x2d, wo, g, rw, ex, tril_strict)


def _rows_to_wide(ref):
    n = ref.shape[0] // TILE_SUBLANES
    return jnp.concatenate([ref[pl.ds(c, n, stride=TILE_SUBLANES), :] for c in range(TILE_SUBLANES)], axis=1)


def _wide_to_rows(ref, val):
    n = ref.shape[0] // TILE_SUBLANES
    for c in range(TILE_SUBLANES):
        ref[pl.ds(c, n, stride=TILE_SUBLANES), :] = val[:, c * LANES:(c + 1) * LANES]


def _moe_kernel(te_ref, nu_ref, src_cur_ref, src_nxt_ref, dst_prv_ref, dst_cur_ref, h_hbm, w13_ref, w2_ref,
                y_hbm, xa, xb, ya, yb, act_ref, gsem, ssem):
    i = pl.program_id(0)
    n_used = nu_ref[0]
    ts = TILE_SUBLANES
    gm = xa.shape[0] // ts
    n_chunks = w2_ref.shape[0] // MOE_CHUNK
    per_chunk = gm // n_chunks
    head_rows = gm - n_chunks * per_chunk

    def tile_of(ref, row):
        return ref.at[pl.ds(pl.multiple_of(row * ts, ts), ts)]

    def gather_row(src_ref, r, xdst, sem):
        pltpu.make_async_copy(tile_of(h_hbm, src_ref[0, 0, r]), tile_of(xdst, r), sem).start()

    def scatter_row(dst_ref, r, ysrc, sem):
        pltpu.make_async_copy(tile_of(ysrc, r), tile_of(y_hbm, dst_ref[0, 0, r]), sem).start()

    def wait_rows(buf, sem):
        pltpu.make_async_copy(h_hbm.at[pl.ds(0, gm * ts)], buf, sem).wait()

    @pl.when(i == 0)
    def _():
        yb[...] = jnp.zeros_like(yb)

        def body(r, carry):
            gather_row(src_cur_ref, r, xa, gsem.at[0])
            return carry
        lax.fori_loop(0, gm, body, 0)

    def tile(x_cur, x_nxt, y_cur, y_prv, p):
        wait_rows(x_cur, gsem.at[p])

        @pl.when(i >= 1)
        def _():
            wait_rows(y_cur, ssem.at[p])

        def issue(r):
            gather_row(src_nxt_ref, r, x_nxt, gsem.at[1 - p])
            scatter_row(dst_prv_ref, r, y_prv, ssem.at[1 - p])

        for r in range(head_rows):
            issue(n_chunks * per_chunk + r)
        x = _rows_to_wide(x_cur).astype(BF16)

        def body(c, carry):
            for u in range(MOE_UNROLL):
                _swiglu_hidden_chunk(x, w13_ref, act_ref, c * MOE_UNROLL + u, MOE_CHUNK)
            for j in range(per_chunk * MOE_UNROLL):
                issue(c * (per_chunk * MOE_UNROLL) + j)
            return carry

        lax.fori_loop(0, n_chunks // MOE_UNROLL, body, 0)
        _wide_to_rows(y_cur, _dot(act_ref[...], w2_ref[...]))

        @pl.when(i == n_used - 1)
        def _():
            def body(r, carry):
                scatter_row(dst_cur_ref, r, y_cur, ssem.at[p])
                return carry
            lax.fori_loop(0, gm, body, 0)
            wait_rows(y_prv, ssem.at[1 - p])
            wait_rows(y_cur, ssem.at[p])
            wait_rows(x_nxt, gsem.at[1 - p])

    @pl.when((i < n_used) & (i % 2 == 0))
    def _():
        tile(xa, xb, ya, yb, 0)

    @pl.when((i < n_used) & (i % 2 == 1))
    def _():
        tile(xb, xa, yb, ya, 1)


def _moe_ffn(h_tiles, w13, w2, tile_expert, n_used, src, dst, out_rows):
    nt = tile_expert.shape[0]
    gm = src.shape[-1]
    tab = lambda f: pl.BlockSpec((1, 1, gm), lambda i, te, nu: (f(i), 0, 0), memory_space=pltpu.SMEM)
    grid_spec = pltpu.PrefetchScalarGridSpec(
        num_scalar_prefetch=2,
        grid=(nt,),
        in_specs=[
            tab(lambda i: i),
            tab(lambda i: jnp.minimum(i + 1, nt - 1)),
            tab(lambda i: i),
            tab(lambda i: i + 1),
            pl.BlockSpec(memory_space=pl.ANY),
            pl.BlockSpec((None,) + w13.shape[1:], lambda i, te, nu: (te[i], 0, 0), pipeline_mode=pl.Buffered(1)),
            pl.BlockSpec((None,) + w2.shape[1:], lambda i, te, nu: (te[i], 0, 0), pipeline_mode=pl.Buffered(1)),
        ],
        out_specs=pl.BlockSpec(memory_space=pl.ANY),
        scratch_shapes=[pltpu.VMEM((gm * TILE_SUBLANES, LANES), F32)] * 4
        + [pltpu.VMEM((gm, w2.shape[1]), BF16), pltpu.SemaphoreType.DMA((2,)), pltpu.SemaphoreType.DMA((2,))],
    )
    return pl.pallas_call(
        _moe_kernel,
        grid_spec=grid_spec,
        out_shape=jax.ShapeDtypeStruct((out_rows * TILE_SUBLANES, LANES), F32),
        compiler_params=_cparams(("arbitrary",)),
        name="moe_ffn",
    )(tile_expert, n_used, src, src, dst, dst, h_tiles, w13, w2)


def _combine_kernel(x_ref, y0_ref, y1_ref, info_ref, o_ref):
    o_ref[...] = (x_ref[...] + info_ref[:, 2:3] * _rows_to_wide(y0_ref)
                  + info_ref[:, 3:4] * _rows_to_wide(y1_ref))


def _combine(x1, y, info):
    t_rows = x1.shape[0]
    tm = min(ROW_TILE, t_rows)
    row = lambda w: pl.BlockSpec((tm, w), lambda i: (i, 0))
    ysp = lambda kk: pl.BlockSpec((None, tm * TILE_SUBLANES, LANES), lambda i: (kk, i, 0))
    return pl.pallas_call(
        _combine_kernel,
        grid=(t_rows // tm,),
        in_specs=[row(D_MODEL), ysp(0), ysp(1), row(LANES)],
        out_specs=row(D_MODEL),
        out_shape=jax.ShapeDtypeStruct((t_rows, D_MODEL), F32),
        compiler_params=_cparams(("arbitrary",)),
        name="moe_combine",
    )(x1, y, y, info)


def _head_mean_matrix(width):
    idx = np.arange(width) // HEAD_DIM
    return jnp.asarray((idx[:, None] == idx[None, :]).astype(np.float32) / HEAD_DIM, dtype=BF16)


def _tril(n, strict):
    r = np.arange(n)
    m = (r[None, :] < r[:, None]) if strict else (r[None, :] <= r[:, None])
    return jnp.asarray(m.astype(np.float32), dtype=BF16)


def _rope_tables(seq):
    half = HEAD_DIM // 2
    inv_freq = ROPE_THETA ** (-jnp.arange(half, dtype=F32) / half)
    ang = jnp.arange(seq, dtype=F32)[:, None] * inv_freq[None, :]
    cos = jnp.cos(ang)
    sin = jnp.sin(ang)
    cos_t = jnp.concatenate([cos, cos, cos, cos], axis=1)
    sin_t = jnp.concatenate([-sin, sin, -sin, sin], axis=1)
    return cos_t, sin_t


def _band_bias():
    qi = np.arange(DIL_BLOCK)[:, None]
    ki = np.arange(-DIL_BLOCK, DIL_BLOCK)[None, :]
    rel = qi - ki
    return jnp.asarray(np.where((rel >= 0) & (rel <= DIL_BLOCK), 0.0, NEG_BIG).astype(np.float32))


def _head_expand():
    m = np.zeros((LANES, DIL_WIDTH), np.float32)
    for h in range(DIL_HEADS):
        m[h, h * HEAD_DIM:(h + 1) * HEAD_DIM] = 1.0
    return jnp.asarray(m, dtype=BF16)


def _bias_placement():
    pq = np.zeros((BIAS_PARTS, LANES, FOX_WIDTH), np.float32)
    pk = np.zeros((BIAS_PARTS, LANES, FOX_WIDTH), np.float32)
    qone = np.zeros((1, FOX_WIDTH), np.float32)
    kone = np.zeros((1, FOX_WIDTH), np.float32)
    for head in range(FOX_HEADS):
        base = (head // 2) * LANES + (head % 2) * BIAS_LANES
        for j in range(BIAS_PARTS):
            pq[j, head, base + j] = 1.0
            pk[j, head, base + BIAS_PARTS + j] = -1.0
            qone[0, base + BIAS_PARTS + j] = 1.0
            kone[0, base + j] = 1.0
    return (jnp.asarray(pq, dtype=BF16), jnp.asarray(pk, dtype=BF16), jnp.asarray(qone), jnp.asarray(kone))


def _even_layer(x2d, batch, seq, norm1, w_in, fox_qn, fox_kn, fox_fbias, conv_w, conv_b, dt_bias, a_log,
                d_skip, ssd_norm, w_out, norm2, ffn_w13, ffn_w2):
    w = FOX_WIDTH
    o_z = 3 * w + FOX_HEADS
    o_x = o_z + SSD_INNER
    o_dt = o_x + SSD_CONV_DIM
    wqk = w_in[:, :2 * w].astype(BF16)
    wvt = w_in[:, 2 * w:3 * w].T.astype(BF16)
    wzx = w_in[:, o_z:o_dt].astype(BF16)
    pad = LANES - FOX_HEADS - SSD_HEADS
    wfd = jnp.concatenate([w_in[:, 3 * w:o_z], w_in[:, o_dt:], jnp.zeros((D_MODEL, pad), F32)], axis=1).astype(BF16)
    fdb = jnp.concatenate([fox_fbias, dt_bias, jnp.zeros((pad,), F32)])[None, :]
    tm = min(ROW_TILE, seq)
    q, qb, k, kb, vt, z, xbc, fd = _even_in_proj(
        x2d, seq, norm1[None, :], wqk, wvt, wzx, wfd, fox_qn.reshape(1, w), fox_kn.reshape(1, w), fdb,
        _head_mean_matrix(w), _tril(tm, False), *_bias_placement())

    o_fox = _fox_attention(q, qb, k, kb, vt, batch, seq)

    dtt = jnp.transpose(fd[:, FOX_HEADS:FOX_HEADS + SSD_HEADS].reshape(batch, seq, SSD_HEADS), (0, 2, 1))
    d_skip_c = jnp.repeat(d_skip, HEAD_DIM)[None, :]
    y = _ssd(xbc, fd, dtt, z, conv_w, conv_b[None, :], a_log, d_skip_c, ssd_norm[None, :],
             _tril(SSD_CHUNK, False), batch, seq)

    return _even_out_ffn(o_fox, y, x2d, w_out.astype(BF16), norm2[None, :],
                         ffn_w13.astype(BF16), ffn_w2.astype(BF16))


def _moe_tables(info, counts, t_rows, gm):
    e1 = info[:, 0].astype(jnp.int32)
    e2 = info[:, 1].astype(jnp.int32)
    r1 = info[:, 4].astype(jnp.int32)
    r2 = info[:, 5].astype(jnp.int32)
    cnt = counts[0, :N_EXPERTS].astype(jnp.int32)
    tiles = (cnt + gm - 1) // gm
    tile_end = jnp.cumsum(tiles)
    offs = (tile_end - tiles) * gm
    nt = (2 * t_rows) // gm + N_EXPERTS
    tile_idx = jnp.arange(nt, dtype=jnp.int32)
    tile_expert = jnp.minimum(jnp.sum((tile_idx[:, None] >= tile_end[None, :]).astype(jnp.int32), axis=1),
                              N_EXPERTS - 1)
    n_used = tile_end[-1:].astype(jnp.int32)
    slots = jnp.concatenate([offs[e1] + r1, offs[e2] + r2])
    codes = jnp.full((nt * gm,), -1, jnp.int32).at[slots].set(
        jnp.arange(2 * t_rows, dtype=jnp.int32), unique_indices=True)
    out_half = t_rows + N_EXPERTS * gm
    second = codes >= t_rows
    src = jnp.where(codes < 0, 0, jnp.where(second, codes - t_rows, codes))
    spare = t_rows + jnp.repeat(tile_expert, gm) * gm + jnp.tile(jnp.arange(gm, dtype=jnp.int32), nt)
    dst = jnp.where(codes < 0, spare, jnp.where(second, codes - t_rows + out_half, codes))
    dst = jnp.concatenate([t_rows + jnp.arange(gm, dtype=jnp.int32), dst])
    return tile_expert, n_used, src.reshape(nt, 1, gm), dst.reshape(nt + 1, 1, gm), out_half


def _odd_layer(x2d, batch, seq, norm1, w_qkv, qn, kn, w_out, norm2, router, moe_w13, moe_w2):
    t_rows = batch * seq
    cos, sin = _rope_tables(seq)
    q, k, v = _odd_qkv(x2d, seq, norm1[None, :], w_qkv.astype(BF16), qn.reshape(1, DIL_WIDTH),
                       kn.reshape(1, DIL_WIDTH), _head_mean_matrix(DIL_WIDTH), cos, sin)
    shp = (batch, seq, DIL_WIDTH)
    band = _band_bias()
    outs, lses = zip(*[_dilated_branch(q.reshape(shp), k.reshape(shp), v.reshape(shp), band, d)
                       for (_, d) in DIL_PATTERNS])
    rw = jnp.concatenate([router, jnp.zeros((D_MODEL, LANES - N_EXPERTS), F32)], axis=1)
    tm = min(ROW_TILE, t_rows)
    x1, h, info, counts = _odd_out_router(*outs, *lses, x2d, w_out.astype(BF16), norm2[None, :], rw,
                                          _head_expand(), _tril(tm, True))
    gm = min(MOE_TILE, t_rows)
    tile_expert, n_used, src, dst, out_half = _moe_tables(info, counts, t_rows, gm)
    y = _moe_ffn(h, moe_w13.astype(BF16), moe_w2.astype(BF16), tile_expert, n_used, src, dst, 2 * out_half)
    return _combine(x1, y.reshape(2, out_half * TILE_SUBLANES, LANES), info)


def kernel(x, e_norm1, e_w_in, e_fox_qn, e_fox_kn, e_fox_fbias, e_conv_w, e_conv_b, e_dt_bias, e_a_log, e_d_skip, e_ssd_norm, e_w_out, e_norm2, e_ffn_w13, e_ffn_w2, o_norm1, o_w_qkv, o_qn, o_kn, o_w_out, o_norm2, o_router, o_moe_w13, o_moe_w2):
    batch, seq, _ = x.shape
    depth = e_norm1.shape[0] + o_norm1.shape[0]
    x2d = x.reshape(batch * seq, D_MODEL)
    for i in range(depth):
        j = i // 2
        if i % 2 == 0:
            x2d = _even_layer(x2d, batch, seq, e_norm1[j], e_w_in[j], e_fox_qn[j], e_fox_kn[j], e_fox_fbias[j],
                              e_conv_w[j], e_conv_b[j], e_dt_bias[j], e_a_log[j], e_d_skip[j], e_ssd_norm[j],
                              e_w_out[j], e_norm2[j], e_ffn_w13[j], e_ffn_w2[j])
        else:
            x2d = _odd_layer(x2d, batch, seq, o_norm1[j], o_w_qkv[j], o_qn[j], o_kn[j], o_w_out[j], o_norm2[j],
                             o_router[j], o_moe_w13[j], o_moe_w2[j])
    return x2d.reshape(batch, seq, D_MODEL)
```

```python
import functools
import math

import jax
import jax.numpy as jnp
import numpy as np
from jax import lax
from jax.experimental import pallas as pl
from jax.experimental.pallas import tpu as pltpu

F32 = jnp.float32
BF16 = jnp.bfloat16

D_MODEL = 1024
HEAD_DIM = 64
RMS_EPS = 1e-6
ROPE_THETA = 10000.0
FOX_HEADS = 8
FOX_WIDTH = FOX_HEADS * HEAD_DIM
SSD_HEADS = 8
SSD_INNER = 512
SSD_GROUPS = 2
SSD_STATE = 128
SSD_CONV = 4
SSD_CHUNK = 128
SSD_CONV_DIM = SSD_INNER + 2 * SSD_GROUPS * SSD_STATE
DIL_HEADS = 16
DIL_WIDTH = DIL_HEADS * HEAD_DIM
DIL_PATTERNS = ((128, 1), (512, 4), (2048, 16))
DIL_BLOCK = 128
FFN_DIM = 2816
N_EXPERTS = 8
EXPERT_DIM = 3584

LANES = 128
TILE_SUBLANES = D_MODEL // LANES
NEG_BIG = -1e30
LOG2E = math.log2(math.e)
DEN_ROWS = 16
BIAS_PARTS = 3
BIAS_LANES = 2 * BIAS_PARTS
VMEM_LIMIT = 56 * 1024 * 1024

ROW_TILE = 512
ATTN_TILE = 512
FOX_PAIRS_PER_STEP = 4
FFN_CHUNK = 256
MOE_CHUNK = 256
MOE_UNROLL = 2
MOE_TILE = 512


def _cparams(sem):
    return pltpu.CompilerParams(dimension_semantics=sem, vmem_limit_bytes=VMEM_LIMIT)


def _softplus_parts(x):
    return jnp.log(1.0 + jnp.exp(-jnp.abs(x)))


def _split_bf16(a, parts):
    out = []
    r = a
    for _ in range(parts):
        p = r.astype(BF16)
        out.append(p)
        r = r - p.astype(F32)
    return out


def _dot(a, b):
    return jnp.dot(a, b, preferred_element_type=F32)


def _dot_nt(a, b):
    return lax.dot_general(a, b, (((1,), (1,)), ((), ())), preferred_element_type=F32)


def _dot_exact_rhs(a_f32, b_bf16, parts):
    acc = None
    for p in _split_bf16(a_f32, parts):
        t = _dot(p, b_bf16)
        acc = t if acc is None else acc + t
    return acc


def _dot_exact_lhs(a_bf16, b_f32, parts):
    acc = None
    for p in _split_bf16(b_f32, parts):
        t = _dot(a_bf16, p)
        acc = t if acc is None else acc + t
    return acc


def _rms(x, g):
    return x * lax.rsqrt(jnp.mean(x * x, axis=-1, keepdims=True) + RMS_EPS) * g


def _head_norm(x, hm, gain):
    ms = _dot((x * x).astype(BF16), hm)
    return x * lax.rsqrt(ms + RMS_EPS) * gain


def _silu(x):
    return x * (1.0 / (1.0 + jnp.exp(-x)))


def _even_in_kernel(tiles_per_seq, x_ref, g_ref, wqk_ref, wvt_ref, wzx_ref, wfd_ref, qg_ref, kg_ref,
                    fdb_ref, hm_ref, tril_ref, pq_ref, pk_ref, qone_ref, kone_ref,
                    q_ref, qb_ref, k_ref, kb_ref, vt_ref, z_ref, xbc_ref, fd_ref, carry_ref):
    i = pl.program_id(0)

    @pl.when(i % tiles_per_seq == 0)
    def _():
        carry_ref[...] = jnp.zeros_like(carry_ref)

    h = _rms(x_ref[...], g_ref[...]).astype(BF16)
    qk = _dot(h, wqk_ref[...])
    hm = hm_ref[...]
    w = FOX_WIDTH
    q_ref[...] = (_head_norm(qk[:, :w], hm, qg_ref[...]) * (HEAD_DIM ** -0.5 * LOG2E)).astype(BF16)
    k_ref[...] = _head_norm(qk[:, w:], hm, kg_ref[...]).astype(BF16)
    vt_ref[...] = _dot_nt(wvt_ref[...], h).astype(BF16)
    zx = _dot(h, wzx_ref[...])
    z_ref[...] = zx[:, :SSD_INNER].astype(BF16)
    xbc_ref[...] = zx[:, SSD_INNER:]
    fd = _dot(h, wfd_ref[...]) + fdb_ref[...]
    t = _softplus_parts(fd)
    log_f = jnp.minimum(fd, 0.0) - t
    dt = jnp.maximum(fd, 0.0) + t
    c = _dot_exact_lhs(tril_ref[...], log_f, 2) + carry_ref[...]
    carry_ref[...] = c[-1:, :]
    lane = lax.broadcasted_iota(jnp.int32, fd.shape, 1)
    fd_ref[...] = jnp.where(lane < FOX_HEADS, c, dt)
    qb = qone_ref[...]
    kb = kone_ref[...]
    for j, part in enumerate(_split_bf16(c * LOG2E, 3)):
        qb = qb + _dot(part, pq_ref[j])
        kb = kb + _dot(part, pk_ref[j])
    qb_ref[...] = qb.astype(BF16)
    kb_ref[...] = kb.astype(BF16)


def _even_in_proj(x2d, seq, g, wqk, wvt, wzx, wfd, qg, kg, fdb, hm, tril, pq, pk, qone, kone):
    t_rows = x2d.shape[0]
    tm = min(ROW_TILE, seq)
    n = t_rows // tm
    row = lambda w: pl.BlockSpec((tm, w), lambda i: (i, 0))
    full = lambda a: pl.BlockSpec(a.shape, lambda i: (0,) * a.ndim)
    consts = (g, wqk, wvt, wzx, wfd, qg, kg, fdb, hm, tril, pq, pk, qone, kone)
    return pl.pallas_call(
        functools.partial(_even_in_kernel, seq // tm),
        grid=(n,),
        in_specs=[row(D_MODEL)] + [full(a) for a in consts],
        out_specs=[row(FOX_WIDTH)] * 4 + [pl.BlockSpec((FOX_WIDTH, tm), lambda i: (0, i)),
                                           row(SSD_INNER), row(SSD_CONV_DIM), row(LANES)],
        out_shape=[jax.ShapeDtypeStruct((t_rows, FOX_WIDTH), BF16)] * 4
        + [jax.ShapeDtypeStruct((FOX_WIDTH, t_rows), BF16),
           jax.ShapeDtypeStruct((t_rows, SSD_INNER), BF16),
           jax.ShapeDtypeStruct((t_rows, SSD_CONV_DIM), F32),
           jax.ShapeDtypeStruct((t_rows, LANES), F32)],
        scratch_shapes=[pltpu.VMEM((1, LANES), F32)],
        compiler_params=_cparams(("arbitrary",)),
        name="even_in_proj",
    )(x2d, *consts)


def _fox_kernel(qi_ref, ki_ref, q_ref, qb_ref, k_ref, kb_ref, vt_ref, o_ref, qs_ref, m_ref, l_ref, acc_ref):
    step_idx = pl.program_id(2)
    qi = qi_ref[step_idx]
    ki = ki_ref[step_idx]
    tq = q_ref.shape[0]
    tk = k_ref.shape[0]
    heads = range(2 * FOX_PAIRS_PER_STEP)
    lanes_of = lambda h: slice((h // 2) * LANES, (h // 2 + 1) * LANES)

    @pl.when(ki == 0)
    def _():
        m_ref[...] = jnp.full_like(m_ref, NEG_BIG)
        l_ref[...] = jnp.zeros_like(l_ref)
        acc_ref[...] = jnp.zeros_like(acc_ref)
        lane = lax.broadcasted_iota(jnp.int32, (tq, LANES), 1)
        zero = jnp.zeros((tq, LANES), q_ref.dtype)
        for h in heads:
            hh = h % 2
            qh = jnp.where((lane >= hh * HEAD_DIM) & (lane < (hh + 1) * HEAD_DIM), q_ref[:, lanes_of(h)], zero)
            bh = jnp.where((lane >= hh * BIAS_LANES) & (lane < (hh + 1) * BIAS_LANES), qb_ref[:, lanes_of(h)], zero)
            qs_ref[h] = jnp.concatenate([qh, bh], axis=1)

    def step(masked):
        if masked:
            keep = (lax.broadcasted_iota(jnp.int32, (tk, tq), 0)
                    <= lax.broadcasted_iota(jnp.int32, (tk, tq), 1))
        ka = [jnp.concatenate([k_ref[:, lanes_of(2 * p)], kb_ref[:, lanes_of(2 * p)]], axis=1)
              for p in range(FOX_PAIRS_PER_STEP)]
        scores = [_dot_nt(ka[h // 2], qs_ref[h]) for h in heads]
        if masked:
            scores = [jnp.where(keep, s, NEG_BIG) for s in scores]
        m_prev = [m_ref[h] for h in heads]
        m_new = [jnp.maximum(m_prev[h], jnp.max(scores[h], axis=0, keepdims=True)) for h in heads]
        probs = [jnp.exp2(scores[h] - m_new[h]) for h in heads]
        alpha = [jnp.exp2(m_prev[h] - m_new[h]) for h in heads]
        pv = [_dot(vt_ref[lanes_of(h), :], probs[h].astype(BF16)) for h in heads]
        for h in heads:
            l_ref[h] = alpha[h] * l_ref[h] + jnp.sum(probs[h], axis=0, keepdims=True)
            acc_ref[h] = alpha[h] * acc_ref[h] + pv[h]
            m_ref[h] = m_new[h]

    @pl.when(ki < qi)
    def _():
        step(False)

    @pl.when(ki == qi)
    def _():
        step(True)
        for p in range(FOX_PAIRS_PER_STEP):
            o0 = acc_ref[2 * p] / l_ref[2 * p]
            o1 = acc_ref[2 * p + 1] / l_ref[2 * p + 1]
            o_t = jnp.concatenate([o0[:HEAD_DIM], o1[HEAD_DIM:]], axis=0)
            o_ref[:, lanes_of(2 * p)] = o_t.T.astype(o_ref.dtype)


def _fox_attention(q, qb, k, kb, vt, batch, seq):
    t = min(ATTN_TILE, seq)
    n = seq // t
    w = FOX_PAIRS_PER_STEP * LANES
    groups = FOX_WIDTH // w
    tri = [(i, j) for i in range(n) for j in range(i + 1)]
    qi_tab = jnp.asarray([i for i, _ in tri], jnp.int32)
    ki_tab = jnp.asarray([j for _, j in tri], jnp.int32)
    qspec = pl.BlockSpec((t, w), lambda bi, g, s, qi, ki: (bi * n + qi[s], g))
    kspec = pl.BlockSpec((t, w), lambda bi, g, s, qi, ki: (bi * n + ki[s], g))
    vspec = pl.BlockSpec((w, t), lambda bi, g, s, qi, ki: (g, bi * n + ki[s]))
    heads = 2 * FOX_PAIRS_PER_STEP
    grid_spec = pltpu.PrefetchScalarGridSpec(
        num_scalar_prefetch=2,
        grid=(batch, groups, len(tri)),
        in_specs=[qspec, qspec, kspec, kspec, vspec],
        out_specs=qspec,
        scratch_shapes=[pltpu.VMEM((heads, t, 2 * LANES), BF16), pltpu.VMEM((heads, 1, t), F32),
                        pltpu.VMEM((heads, 1, t), F32), pltpu.VMEM((heads, LANES, t), F32)],
    )
    return pl.pallas_call(
        _fox_kernel,
        grid_spec=grid_spec,
        out_shape=jax.ShapeDtypeStruct((batch * seq, FOX_WIDTH), BF16),
        compiler_params=_cparams(("arbitrary",) * 3),
        name="fox_attention",
    )(qi_tab, ki_tab, q, qb, k, kb, vt)


def _ssd_kernel(xbc_ref, prev_ref, fd_ref, dtt_ref, z_ref, cw_ref, cb_ref, alog_ref, alogc_ref,
                dsk_ref, nrm_ref, tri_ref, y_ref, state_ref):
    c_idx = pl.program_id(1)
    L = SSD_CHUNK
    P = HEAD_DIM
    N = SSD_STATE

    @pl.when(c_idx == 0)
    def _():
        state_ref[...] = jnp.zeros_like(state_ref)

    cur = xbc_ref[...]
    prev = jnp.where(c_idx > 0, prev_ref[...], 0.0)
    ext = jnp.concatenate([prev, cur], axis=0)
    cw = cw_ref[...]
    conv = cb_ref[...] + cw[SSD_CONV - 1:SSD_CONV, :] * cur
    for kk in range(SSD_CONV - 1):
        sh = SSD_CONV - 1 - kk
        conv = conv + cw[kk:kk + 1, :] * ext[8 - sh:8 - sh + L, :]
    xc = _silu(conv)
    xs = xc[:, :SSD_INNER]
    bm = xc[:, SSD_INNER:SSD_INNER + SSD_GROUPS * N]
    cm = xc[:, SSD_INNER + SSD_GROUPS * N:]

    dt = fd_ref[:, FOX_HEADS:FOX_HEADS + SSD_HEADS]
    a_row = -jnp.exp(alog_ref[...])
    a_col = -jnp.exp(alogc_ref[...])
    tri = tri_ref[...]
    cum = _dot_exact_lhs(tri, dt * a_row, 2)
    cum_t = lax.dot_general(_split_bf16(dtt_ref[...] * a_col, 2)[0], tri, (((1,), (1,)), ((), ())),
                            preferred_element_type=F32)
    cum_t = cum_t + lax.dot_general(_split_bf16(dtt_ref[...] * a_col, 2)[1], tri,
                                    (((1,), (1,)), ((), ())), preferred_element_type=F32)
    row_i = lax.broadcasted_iota(jnp.int32, (L, L), 0)
    col_i = lax.broadcasted_iota(jnp.int32, (L, L), 1)
    causal = row_i >= col_i

    ys = []
    for g in range(SSD_GROUPS):
        bm_g = bm[:, g * N:(g + 1) * N]
        cm_g = cm[:, g * N:(g + 1) * N]
        cb_g = _dot_nt(cm_g.astype(BF16), bm_g.astype(BF16))
        bm_t = bm_g.T
        for hh in range(SSD_HEADS // SSD_GROUPS):
            h = g * (SSD_HEADS // SSD_GROUPS) + hh
            cum_c = cum[:, h:h + 1]
            cum_r = cum_t[h:h + 1, :]
            seg = jnp.where(causal, jnp.exp(jnp.minimum(cum_c - cum_r, 0.0)), 0.0)
            x_h = xs[:, h * P:(h + 1) * P]
            xdt = (x_h * dt[:, h:h + 1]).astype(BF16)
            y_h = _dot((cb_g * seg).astype(BF16), xdt)
            h_prev = state_ref[h]
            y_h = y_h + _dot((cm_g * jnp.exp(cum_c)).astype(BF16), h_prev.astype(BF16))
            last = cum_t[h:h + 1, L - 1:L]
            dte = jnp.exp(last - cum_r)
            st = _dot((bm_t * dte).astype(BF16), xdt)
            state_ref[h] = jnp.exp(last) * h_prev + st
            ys.append(y_h)
    y = jnp.concatenate(ys, axis=1) + xs * dsk_ref[...]
    y = y * _silu(z_ref[...].astype(F32))
    gw = SSD_INNER // SSD_GROUPS
    outs = []
    for g in range(SSD_GROUPS):
        yg = y[:, g * gw:(g + 1) * gw]
        outs.append(yg * lax.rsqrt(jnp.mean(yg * yg, axis=-1, keepdims=True) + RMS_EPS))
    y_ref[...] = (jnp.concatenate(outs, axis=1) * nrm_ref[...]).astype(y_ref.dtype)


def _ssd(xbc, fd, dtt, z, conv_w, conv_b, a_log, d_skip_c, ssd_norm, tri, batch, seq):
    L = SSD_CHUNK
    nc = seq // L
    t_rows = batch * seq
    row = lambda w: pl.BlockSpec((L, w), lambda b, c: (b * nc + c, 0))
    full = lambda a: pl.BlockSpec(a.shape, lambda b, c: (0,) * a.ndim)
    prev = pl.BlockSpec((8, SSD_CONV_DIM), lambda b, c: (jnp.maximum((b * nc + c) * (L // 8) - 1, 0), 0))
    return pl.pallas_call(
        _ssd_kernel,
        grid=(batch, nc),
        in_specs=[row(SSD_CONV_DIM), prev, row(LANES),
                  pl.BlockSpec((None, SSD_HEADS, L), lambda b, c: (b, 0, c)),
                  row(SSD_INNER), full(conv_w), full(conv_b), full(a_log[None, :]),
                  full(a_log[:, None]), full(d_skip_c), full(ssd_norm), full(tri)],
        out_specs=row(SSD_INNER),
        out_shape=jax.ShapeDtypeStruct((t_rows, SSD_INNER), BF16),
        scratch_shapes=[pltpu.VMEM((SSD_HEADS, SSD_STATE, HEAD_DIM), F32)],
        compiler_params=_cparams(("arbitrary", "arbitrary")),
        name="ssd_scan",
    )(xbc, xbc, fd, dtt, z, conv_w, conv_b, a_log[None, :], a_log[:, None], d_skip_c, ssd_norm, tri)


def _swiglu_hidden_chunk(h, w13_ref, act_ref, c, tf):
    f = w13_ref.shape[1] // 2
    off = pl.multiple_of(c * tf, tf)
    gate = _dot(h, w13_ref[:, pl.ds(off, tf)])
    up = _dot(h, w13_ref[:, pl.ds(f + off, tf)])
    act_ref[:, pl.ds(off, tf)] = (_silu(gate) * up).astype(BF16)


def _even_out_kernel(o_ref, y_ref, x_ref, wo_ref, g_ref, w13_ref, w2_ref, out_ref, act_ref):
    w = FOX_WIDTH
    x1 = x_ref[...] + _dot(o_ref[...], wo_ref[:w, :]) + _dot(y_ref[...], wo_ref[w:, :])
    h = _rms(x1, g_ref[...]).astype(BF16)

    def body(c, carry):
        _swiglu_hidden_chunk(h, w13_ref, act_ref, c, FFN_CHUNK)
        return carry

    lax.fori_loop(0, w2_ref.shape[0] // FFN_CHUNK, body, 0)
    out_ref[...] = x1 + _dot(act_ref[...], w2_ref[...])


def _even_out_ffn(o_fox, y_ssd, x2d, wo, g, w13, w2):
    t_rows = x2d.shape[0]
    tm = min(ROW_TILE, t_rows)
    row = lambda w: pl.BlockSpec((tm, w), lambda i: (i, 0))
    full = lambda a: pl.BlockSpec(a.shape, lambda i: (0,) * a.ndim, pipeline_mode=pl.Buffered(1))
    return pl.pallas_call(
        _even_out_kernel,
        grid=(t_rows // tm,),
        in_specs=[row(FOX_WIDTH), row(SSD_INNER), row(D_MODEL), full(wo), full(g), full(w13), full(w2)],
        out_specs=row(D_MODEL),
        out_shape=jax.ShapeDtypeStruct((t_rows, D_MODEL), F32),
        scratch_shapes=[pltpu.VMEM((tm, w2.shape[0]), BF16)],
        compiler_params=_cparams(("arbitrary",)),
        name="even_out_ffn",
    )(o_fox, y_ssd, x2d, wo, g, w13, w2)


def _rope(x, cos, sin_signed):
    n = x.shape[1]
    lane = lax.broadcasted_iota(jnp.int32, x.shape, 1)
    first = (lane % HEAD_DIM) < (HEAD_DIM // 2)
    rot = jnp.where(first, pltpu.roll(x, n - HEAD_DIM // 2, 1), pltpu.roll(x, HEAD_DIM // 2, 1))
    reps = n // LANES
    cos_t = jnp.concatenate([cos] * reps, axis=1)
    sin_t = jnp.concatenate([sin_signed] * reps, axis=1)
    return x * cos_t + rot * sin_t


def _odd_qkv_kernel(x_ref, g_ref, w_ref, qg_ref, kg_ref, hm_ref, cos_ref, sin_ref, q_ref, k_ref, v_ref):
    h = _rms(x_ref[...], g_ref[...]).astype(BF16)
    hm = hm_ref[...]
    w = DIL_WIDTH
    cos = cos_ref[...]
    sin = sin_ref[...]
    q = _head_norm(_dot(h, w_ref[:, :w]), hm, qg_ref[...])
    q_ref[...] = (_rope(q, cos, sin) * (HEAD_DIM ** -0.5 * LOG2E)).astype(BF16)
    k = _head_norm(_dot(h, w_ref[:, w:2 * w]), hm, kg_ref[...])
    k_ref[...] = _rope(k, cos, sin).astype(BF16)
    v_ref[...] = _dot(h, w_ref[:, 2 * w:]).astype(BF16)


def _odd_qkv(x2d, seq, g, w, qg, kg, hm, cos, sin):
    t_rows = x2d.shape[0]
    tm = min(ROW_TILE, seq)
    per_seq = seq // tm
    row = lambda wd: pl.BlockSpec((tm, wd), lambda i: (i, 0))
    full = lambda a: pl.BlockSpec(a.shape, lambda i: (0,) * a.ndim)
    tab = pl.BlockSpec((tm, LANES), lambda i: (i % per_seq, 0))
    return pl.pallas_call(
        _odd_qkv_kernel,
        grid=(t_rows // tm,),
        in_specs=[row(D_MODEL), full(g), full(w), full(qg), full(kg), full(hm), tab, tab],
        out_specs=[row(DIL_WIDTH)] * 3,
        out_shape=[jax.ShapeDtypeStruct((t_rows, DIL_WIDTH), BF16)] * 3,
        compiler_params=_cparams(("arbitrary",)),
        name="odd_qkv",
    )(x2d, g, w, qg, kg, hm, cos, sin)


def _dilated_kernel(q_ref, kp_ref, kc_ref, vp_ref, vc_ref, band_ref, o_ref, lse_ref):
    j = pl.program_id(2)
    blk = DIL_BLOCK
    key_row = lax.broadcasted_iota(jnp.int32, (2 * blk, blk), 0)
    bias = jnp.where((j == 0) & (key_row < blk), NEG_BIG, band_ref[...])
    lane = lax.broadcasted_iota(jnp.int32, (blk, LANES), 1)
    ones_rows = jnp.ones((DEN_ROWS, 2 * blk), BF16)
    pairs = range(DIL_HEADS // 2)
    heads = range(DIL_HEADS)
    lanes_of = lambda pr: slice(pr * LANES, (pr + 1) * LANES)
    kk = [jnp.concatenate([kp_ref[:, lanes_of(pr)], kc_ref[:, lanes_of(pr)]], axis=0) for pr in pairs]
    vt = [jnp.concatenate([jnp.concatenate([vp_ref[:, lanes_of(pr)], vc_ref[:, lanes_of(pr)]], axis=0)
                           .astype(F32).T.astype(BF16), ones_rows], axis=0) for pr in pairs]
    zero = jnp.zeros((blk, LANES), q_ref.dtype)
    qm = [jnp.where((lane >= (h % 2) * HEAD_DIM) & (lane < (h % 2 + 1) * HEAD_DIM), q_ref[:, lanes_of(h // 2)], zero)
          for h in heads]
    s = [_dot_nt(kk[h // 2], qm[h]) + bias for h in heads]
    m = [jnp.max(s[h], axis=0, keepdims=True) for h in heads]
    e = [jnp.exp2(s[h] - m[h]).astype(BF16) for h in heads]
    ov = [_dot(vt[h // 2], e[h]) for h in heads]
    den = [ov[h][LANES:LANES + 1, :] for h in heads]
    for pr in pairs:
        o_t = jnp.concatenate([(ov[2 * pr][:LANES] / den[2 * pr])[:HEAD_DIM],
                               (ov[2 * pr + 1][:LANES] / den[2 * pr + 1])[HEAD_DIM:]], axis=0)
        o_ref[:, lanes_of(pr)] = o_t.T.astype(o_ref.dtype)
    lse_t = jnp.concatenate([m[h] + jnp.log2(den[h]) for h in heads]
                            + [jnp.zeros((LANES - DIL_HEADS, blk), F32)], axis=0)
    lse_ref[...] = lse_t.T


def _dilated_branch(q, k, v, band, dilation):
    b, s, w = q.shape
    d = dilation
    n_sub = s // d
    nb = n_sub // DIL_BLOCK
    view = lambda t: t.reshape(b, n_sub, d * w)
    cur = pl.BlockSpec((None, DIL_BLOCK, w), lambda bi, r, j: (bi, j, r))
    prev = pl.BlockSpec((None, DIL_BLOCK, w), lambda bi, r, j: (bi, jnp.maximum(j - 1, 0), r))
    lse_spec = pl.BlockSpec((None, DIL_BLOCK, LANES), lambda bi, r, j: (bi, j, r))
    o, lse = pl.pallas_call(
        _dilated_kernel,
        grid=(b, d, nb),
        in_specs=[cur, prev, cur, prev, cur, pl.BlockSpec(band.shape, lambda bi, r, j: (0, 0))],
        out_specs=[cur, lse_spec],
        out_shape=[jax.ShapeDtypeStruct((b, n_sub, d * w), BF16),
                   jax.ShapeDtypeStruct((b, n_sub, d * LANES), F32)],
        compiler_params=_cparams(("arbitrary",) * 3),
        name=f"dilated_attn_d{d}",
    )(view(q), view(k), view(k), view(v), view(v), band)
    return o.reshape(b * s, w), lse.reshape(b * s, LANES)


def _odd_out_kernel(tiles_total, o1_ref, o2_ref, o3_ref, l1_ref, l2_ref, l3_ref, x_ref, wo_ref, g_ref,
                    rw_ref, ex_ref, tril_ref, x1_ref, h_ref, info_ref, cnt_ref, carry_ref):
    i = pl.program_id(0)

    @pl.when(i == 0)
    def _():
        carry_ref[...] = jnp.zeros_like(carry_ref)

    l1, l2, l3 = l1_ref[...], l2_ref[...], l3_ref[...]
    mx = jnp.maximum(jnp.maximum(l1, l2), l3)
    a1, a2, a3 = jnp.exp2(l1 - mx), jnp.exp2(l2 - mx), jnp.exp2(l3 - mx)
    inv = 1.0 / (a1 + a2 + a3)
    ex = ex_ref[...]
    o = (_dot_exact_rhs(a1 * inv, ex, 2) * o1_ref[...].astype(F32)
         + _dot_exact_rhs(a2 * inv, ex, 2) * o2_ref[...].astype(F32)
         + _dot_exact_rhs(a3 * inv, ex, 2) * o3_ref[...].astype(F32))
    x1 = x_ref[...] + _dot(o.astype(BF16), wo_ref[...])
    x1_ref[...] = x1
    h = _rms(x1, g_ref[...])
    _wide_to_rows(h_ref, h)

    logits = jnp.dot(h, rw_ref[...], preferred_element_type=F32, precision=lax.Precision.HIGHEST)
    lane = lax.broadcasted_iota(jnp.int32, logits.shape, 1)
    logits = jnp.where(lane < N_EXPERTS, logits, -jnp.inf)
    m1 = jnp.max(logits, axis=-1, keepdims=True)
    i1 = jnp.min(jnp.where(logits == m1, lane, LANES), axis=-1, keepdims=True)
    rest = jnp.where(lane == i1, -jnp.inf, logits)
    m2 = jnp.max(rest, axis=-1, keepdims=True)
    i2 = jnp.min(jnp.where(rest == m2, lane, LANES), axis=-1, keepdims=True)
    e2 = jnp.exp(m2 - m1)
    g1 = 1.0 / (1.0 + e2)
    g2 = e2 * g1
    hot1 = lane == i1
    hot2 = lane == i2
    onehot = jnp.where(hot1 | hot2, 1.0, 0.0).astype(BF16)
    before = _dot(tril_ref[...], onehot) + carry_ref[...]
    r1 = jnp.sum(jnp.where(hot1, before, 0.0), axis=-1, keepdims=True)
    r2 = jnp.sum(jnp.where(hot2, before, 0.0), axis=-1, keepdims=True)
    total = before[-1:, :] + onehot[-1:, :].astype(F32)
    carry_ref[...] = total
    cnt_ref[...] = jnp.broadcast_to(total, cnt_ref.shape)
    info = jnp.where(lane == 0, i1.astype(F32), 0.0)
    info = jnp.where(lane == 1, i2.astype(F32), info)
    info = jnp.where(lane == 2, g1, info)
    info = jnp.where(lane == 3, g2, info)
    info = jnp.where(lane == 4, r1, info)
    info = jnp.where(lane == 5, r2, info)
    info_ref[...] = info


def _odd_out_router(o1, o2, o3, l1, l2, l3, x2d, wo, g, rw, ex, tril_strict):
    t_rows = x2d.shape[0]
    tm = min(ROW_TILE, t_rows)
    n = t_rows // tm
    row = lambda w: pl.BlockSpec((tm, w), lambda i: (i, 0))
    full = lambda a: pl.BlockSpec(a.shape, lambda i: (0,) * a.ndim)
    return pl.pallas_call(
        functools.partial(_odd_out_kernel, n),
        grid=(n,),
        in_specs=[row(DIL_WIDTH)] * 3 + [row(LANES)] * 3 + [row(D_MODEL), full(wo), full(g), full(rw),
                                                             full(ex), full(tril_strict)],
        out_specs=[row(D_MODEL), pl.BlockSpec((tm * TILE_SUBLANES, LANES), lambda i: (i, 0)), row(LANES),
                   pl.BlockSpec((8, LANES), lambda i: (0, 0))],
        out_shape=[jax.ShapeDtypeStruct((t_rows, D_MODEL), F32),
                   jax.ShapeDtypeStruct((t_rows * TILE_SUBLANES, LANES), F32),
                   jax.ShapeDtypeStruct((t_rows, LANES), F32), jax.ShapeDtypeStruct((8, LANES), F32)],
        scratch_shapes=[pltpu.VMEM((1, LANES), F32)],
        compiler_params=_cparams(("arbitrary",)),
        name="odd_out_router",
    )(o1, o2, o3, l1, l2, l3, x2d, wo, g, rw, ex, tril_strict)


def _rows_to_wide(ref):
    n = ref.shape[0] // TILE_SUBLANES
    return jnp.concatenate([ref[pl.ds(c, n, stride=TILE_SUBLANES), :] for c in range(TILE_SUBLANES)], axis=1)


def _wide_to_rows(ref, val):
    n = ref.shape[0] // TILE_SUBLANES
    for c in range(TILE_SUBLANES):
        ref[pl.ds(c, n, stride=TILE_SUBLANES), :] = val[:, c * LANES:(c + 1) * LANES]


def _moe_kernel(te_ref, nu_ref, src_cur_ref, src_nxt_ref, dst_prv_ref, dst_cur_ref, h_hbm, w13_ref, w2_ref,
                y_hbm, xa, xb, ya, yb, act_ref, gsem, ssem):
    i = pl.program_id(0)
    n_used = nu_ref[0]
    ts = TILE_SUBLANES
    gm = xa.shape[0] // ts
    n_chunks = w2_ref.shape[0] // MOE_CHUNK
    per_chunk = gm // n_chunks
    head_rows = gm - n_chunks * per_chunk

    def tile_of(ref, row):
        return ref.at[pl.ds(pl.multiple_of(row * ts, ts), ts)]

    def gather_row(src_ref, r, xdst, sem):
        pltpu.make_async_copy(tile_of(h_hbm, src_ref[0, 0, r]), tile_of(xdst, r), sem).start()

    def scatter_row(dst_ref, r, ysrc, sem):
        pltpu.make_async_copy(tile_of(ysrc, r), tile_of(y_hbm, dst_ref[0, 0, r]), sem).start()

    def wait_rows(buf, sem):
        pltpu.make_async_copy(h_hbm.at[pl.ds(0, gm * ts)], buf, sem).wait()

    @pl.when(i == 0)
    def _():
        yb[...] = jnp.zeros_like(yb)

        def body(r, carry):
            gather_row(src_cur_ref, r, xa, gsem.at[0])
            return carry
        lax.fori_loop(0, gm, body, 0)

    def tile(x_cur, x_nxt, y_cur, y_prv, p):
        wait_rows(x_cur, gsem.at[p])

        @pl.when(i >= 1)
        def _():
            wait_rows(y_cur, ssem.at[p])

        def issue(r):
            gather_row(src_nxt_ref, r, x_nxt, gsem.at[1 - p])
            scatter_row(dst_prv_ref, r, y_prv, ssem.at[1 - p])

        for r in range(head_rows):
            issue(n_chunks * per_chunk + r)
        x = _rows_to_wide(x_cur).astype(BF16)

        def body(c, carry):
            for u in range(MOE_UNROLL):
                _swiglu_hidden_chunk(x, w13_ref, act_ref, c * MOE_UNROLL + u, MOE_CHUNK)
            for j in range(per_chunk * MOE_UNROLL):
                issue(c * (per_chunk * MOE_UNROLL) + j)
            return carry

        lax.fori_loop(0, n_chunks // MOE_UNROLL, body, 0)
        _wide_to_rows(y_cur, _dot(act_ref[...], w2_ref[...]))

        @pl.when(i == n_used - 1)
        def _():
            def body(r, carry):
                scatter_row(dst_cur_ref, r, y_cur, ssem.at[p])
                return carry
            lax.fori_loop(0, gm, body, 0)
            wait_rows(y_prv, ssem.at[1 - p])
            wait_rows(y_cur, ssem.at[p])
            wait_rows(x_nxt, gsem.at[1 - p])

    @pl.when((i < n_used) & (i % 2 == 0))
    def _():
        tile(xa, xb, ya, yb, 0)

    @pl.when((i < n_used) & (i % 2 == 1))
    def _():
        tile(xb, xa, yb, ya, 1)


def _moe_ffn(h_tiles, w13, w2, tile_expert, n_used, src, dst, out_rows):
    nt = tile_expert.shape[0]
    gm = src.shape[-1]
    tab = lambda f: pl.BlockSpec((1, 1, gm), lambda i, te, nu: (f(i), 0, 0), memory_space=pltpu.SMEM)
    grid_spec = pltpu.PrefetchScalarGridSpec(
        num_scalar_prefetch=2,
        grid=(nt,),
        in_specs=[
            tab(lambda i: i),
            tab(lambda i: jnp.minimum(i + 1, nt - 1)),
            tab(lambda i: i),
            tab(lambda i: i + 1),
            pl.BlockSpec(memory_space=pl.ANY),
            pl.BlockSpec((None,) + w13.shape[1:], lambda i, te, nu: (te[i], 0, 0), pipeline_mode=pl.Buffered(1)),
            pl.BlockSpec((None,) + w2.shape[1:], lambda i, te, nu: (te[i], 0, 0), pipeline_mode=pl.Buffered(1)),
        ],
        out_specs=pl.BlockSpec(memory_space=pl.ANY),
        scratch_shapes=[pltpu.VMEM((gm * TILE_SUBLANES, LANES), F32)] * 4
        + [pltpu.VMEM((gm, w2.shape[1]), BF16), pltpu.SemaphoreType.DMA((2,)), pltpu.SemaphoreType.DMA((2,))],
    )
    return pl.pallas_call(
        _moe_kernel,
        grid_spec=grid_spec,
        out_shape=jax.ShapeDtypeStruct((out_rows * TILE_SUBLANES, LANES), F32),
        compiler_params=_cparams(("arbitrary",)),
        name="moe_ffn",
    )(tile_expert, n_used, src, src, dst, dst, h_tiles, w13, w2)


def _combine_kernel(x_ref, y0_ref, y1_ref, info_ref, o_ref):
    o_ref[...] = (x_ref[...] + info_ref[:, 2:3] * _rows_to_wide(y0_ref)
                  + info_ref[:, 3:4] * _rows_to_wide(y1_ref))


def _combine(x1, y, info):
    t_rows = x1.shape[0]
    tm = min(ROW_TILE, t_rows)
    row = lambda w: pl.BlockSpec((tm, w), lambda i: (i, 0))
    ysp = lambda kk: pl.BlockSpec((None, tm * TILE_SUBLANES, LANES), lambda i: (kk, i, 0))
    return pl.pallas_call(
        _combine_kernel,
        grid=(t_rows // tm,),
        in_specs=[row(D_MODEL), ysp(0), ysp(1), row(LANES)],
        out_specs=row(D_MODEL),
        out_shape=jax.ShapeDtypeStruct((t_rows, D_MODEL), F32),
        compiler_params=_cparams(("arbitrary",)),
        name="moe_combine",
    )(x1, y, y, info)


def _head_mean_matrix(width):
    idx = np.arange(width) // HEAD_DIM
    return jnp.asarray((idx[:, None] == idx[None, :]).astype(np.float32) / HEAD_DIM, dtype=BF16)


def _tril(n, strict):
    r = np.arange(n)
    m = (r[None, :] < r[:, None]) if strict else (r[None, :] <= r[:, None])
    return jnp.asarray(m.astype(np.float32), dtype=BF16)


def _rope_tables(seq):
    half = HEAD_DIM // 2
    inv_freq = ROPE_THETA ** (-jnp.arange(half, dtype=F32) / half)
    ang = jnp.arange(seq, dtype=F32)[:, None] * inv_freq[None, :]
    cos = jnp.cos(ang)
    sin = jnp.sin(ang)
    cos_t = jnp.concatenate([cos, cos, cos, cos], axis=1)
    sin_t = jnp.concatenate([-sin, sin, -sin, sin], axis=1)
    return cos_t, sin_t


def _band_bias():
    qi = np.arange(DIL_BLOCK)[None, :]
    ki = np.arange(-DIL_BLOCK, DIL_BLOCK)[:, None]
    rel = qi - ki
    return jnp.asarray(np.where((rel >= 0) & (rel <= DIL_BLOCK), 0.0, NEG_BIG).astype(np.float32))


def _head_expand():
    m = np.zeros((LANES, DIL_WIDTH), np.float32)
    for h in range(DIL_HEADS):
        m[h, h * HEAD_DIM:(h + 1) * HEAD_DIM] = 1.0
    return jnp.asarray(m, dtype=BF16)


def _bias_placement():
    pq = np.zeros((BIAS_PARTS, LANES, FOX_WIDTH), np.float32)
    pk = np.zeros((BIAS_PARTS, LANES, FOX_WIDTH), np.float32)
    qone = np.zeros((1, FOX_WIDTH), np.float32)
    kone = np.zeros((1, FOX_WIDTH), np.float32)
    for head in range(FOX_HEADS):
        base = (head // 2) * LANES + (head % 2) * BIAS_LANES
        for j in range(BIAS_PARTS):
            pq[j, head, base + j] = 1.0
            pk[j, head, base + BIAS_PARTS + j] = -1.0
            qone[0, base + BIAS_PARTS + j] = 1.0
            kone[0, base + j] = 1.0
    return (jnp.asarray(pq, dtype=BF16), jnp.asarray(pk, dtype=BF16), jnp.asarray(qone), jnp.asarray(kone))


def _even_layer(x2d, batch, seq, norm1, w_in, fox_qn, fox_kn, fox_fbias, conv_w, conv_b, dt_bias, a_log,
                d_skip, ssd_norm, w_out, norm2, ffn_w13, ffn_w2):
    w = FOX_WIDTH
    o_z = 3 * w + FOX_HEADS
    o_x = o_z + SSD_INNER
    o_dt = o_x + SSD_CONV_DIM
    wqk = w_in[:, :2 * w].astype(BF16)
    wvt = w_in[:, 2 * w:3 * w].T.astype(BF16)
    wzx = w_in[:, o_z:o_dt].astype(BF16)
    pad = LANES - FOX_HEADS - SSD_HEADS
    wfd = jnp.concatenate([w_in[:, 3 * w:o_z], w_in[:, o_dt:], jnp.zeros((D_MODEL, pad), F32)], axis=1).astype(BF16)
    fdb = jnp.concatenate([fox_fbias, dt_bias, jnp.zeros((pad,), F32)])[None, :]
    tm = min(ROW_TILE, seq)
    q, qb, k, kb, vt, z, xbc, fd = _even_in_proj(
        x2d, seq, norm1[None, :], wqk, wvt, wzx, wfd, fox_qn.reshape(1, w), fox_kn.reshape(1, w), fdb,
        _head_mean_matrix(w), _tril(tm, False), *_bias_placement())

    o_fox = _fox_attention(q, qb, k, kb, vt, batch, seq)

    dtt = jnp.transpose(fd[:, FOX_HEADS:FOX_HEADS + SSD_HEADS].reshape(batch, seq, SSD_HEADS), (0, 2, 1))
    d_skip_c = jnp.repeat(d_skip, HEAD_DIM)[None, :]
    y = _ssd(xbc, fd, dtt, z, conv_w, conv_b[None, :], a_log, d_skip_c, ssd_norm[None, :],
             _tril(SSD_CHUNK, False), batch, seq)

    return _even_out_ffn(o_fox, y, x2d, w_out.astype(BF16), norm2[None, :],
                         ffn_w13.astype(BF16), ffn_w2.astype(BF16))


def _moe_tables(info, counts, t_rows, gm):
    e1 = info[:, 0].astype(jnp.int32)
    e2 = info[:, 1].astype(jnp.int32)
    r1 = info[:, 4].astype(jnp.int32)
    r2 = info[:, 5].astype(jnp.int32)
    cnt = counts[0, :N_EXPERTS].astype(jnp.int32)
    tiles = (cnt + gm - 1) // gm
    tile_end = jnp.cumsum(tiles)
    offs = (tile_end - tiles) * gm
    nt = (2 * t_rows) // gm + N_EXPERTS
    tile_idx = jnp.arange(nt, dtype=jnp.int32)
    tile_expert = jnp.minimum(jnp.sum((tile_idx[:, None] >= tile_end[None, :]).astype(jnp.int32), axis=1),
                              N_EXPERTS - 1)
    n_used = tile_end[-1:].astype(jnp.int32)
    slots = jnp.concatenate([offs[e1] + r1, offs[e2] + r2])
    codes = jnp.full((nt * gm,), -1, jnp.int32).at[slots].set(
        jnp.arange(2 * t_rows, dtype=jnp.int32), unique_indices=True)
    out_half = t_rows + N_EXPERTS * gm
    second = codes >= t_rows
    src = jnp.where(codes < 0, 0, jnp.where(second, codes - t_rows, codes))
    spare = t_rows + jnp.repeat(tile_expert, gm) * gm + jnp.tile(jnp.arange(gm, dtype=jnp.int32), nt)
    dst = jnp.where(codes < 0, spare, jnp.where(second, codes - t_rows + out_half, codes))
    dst = jnp.concatenate([t_rows + jnp.arange(gm, dtype=jnp.int32), dst])
    return tile_expert, n_used, src.reshape(nt, 1, gm), dst.reshape(nt + 1, 1, gm), out_half


def _odd_layer(x2d, batch, seq, norm1, w_qkv, qn, kn, w_out, norm2, router, moe_w13, moe_w2):
    t_rows = batch * seq
    cos, sin = _rope_tables(seq)
    q, k, v = _odd_qkv(x2d, seq, norm1[None, :], w_qkv.astype(BF16), qn.reshape(1, DIL_WIDTH),
                       kn.reshape(1, DIL_WIDTH), _head_mean_matrix(DIL_WIDTH), cos, sin)
    shp = (batch, seq, DIL_WIDTH)
    band = _band_bias()
    outs, lses = zip(*[_dilated_branch(q.reshape(shp), k.reshape(shp), v.reshape(shp), band, d)
                       for (_, d) in DIL_PATTERNS])
    rw = jnp.concatenate([router, jnp.zeros((D_MODEL, LANES - N_EXPERTS), F32)], axis=1)
    tm = min(ROW_TILE, t_rows)
    x1, h, info, counts = _odd_out_router(*outs, *lses, x2d, w_out.astype(BF16), norm2[None, :], rw,
                                          _head_expand(), _tril(tm, True))
    gm = min(MOE_TILE, t_rows)
    tile_expert, n_used, src, dst, out_half = _moe_tables(info, counts, t_rows, gm)
    y = _moe_ffn(h, moe_w13.astype(BF16), moe_w2.astype(BF16), tile_expert, n_used, src, dst, 2 * out_half)
    return _combine(x1, y.reshape(2, out_half * TILE_SUBLANES, LANES), info)


def kernel(x, e_norm1, e_w_in, e_fox_qn, e_fox_kn, e_fox_fbias, e_conv_w, e_conv_b, e_dt_bias, e_a_log, e_d_skip, e_ssd_norm, e_w_out, e_norm2, e_ffn_w13, e_ffn_w2, o_norm1, o_w_qkv, o_qn, o_kn, o_w_out, o_norm2, o_router, o_moe_w13, o_moe_w2):
    batch, seq, _ = x.shape
    depth = e_norm1.shape[0] + o_norm1.shape[0]
    x2d = x.reshape(batch * seq, D_MODEL)
    for i in range(depth):
        j = i // 2
        if i % 2 == 0:
            x2d = _even_layer(x2d, batch, seq, e_norm1[j], e_w_in[j], e_fox_qn[j], e_fox_kn[j], e_fox_fbias[j],
                              e_conv_w[j], e_conv_b[j], e_dt_bias[j], e_a_log[j], e_d_skip[j], e_ssd_norm[j],
                              e_w_out[j], e_norm2[j], e_ffn_w13[j], e_ffn_w2[j])
        else:
            x2d = _odd_layer(x2d, batch, seq, o_norm1[j], o_w_qkv[j], o_qn[j], o_kn[j], o_w_out[j], o_norm2[j],
                             o_router[j], o_moe_w13[j], o_moe_w2[j])
    return x2d.reshape(batch, seq, D_MODEL)
```

```python
import functools
import math

import jax
import jax.numpy as jnp
import numpy as np
from jax import lax
from jax.experimental import pallas as pl
from jax.experimental.pallas import tpu as pltpu

F32 = jnp.float32
BF16 = jnp.bfloat16

D_MODEL = 1024
HEAD_DIM = 64
RMS_EPS = 1e-6
ROPE_THETA = 10000.0
FOX_HEADS = 8
FOX_WIDTH = FOX_HEADS * HEAD_DIM
SSD_HEADS = 8
SSD_INNER = 512
SSD_GROUPS = 2
SSD_STATE = 128
SSD_CONV = 4
SSD_CHUNK = 128
SSD_CONV_DIM = SSD_INNER + 2 * SSD_GROUPS * SSD_STATE
DIL_HEADS = 16
DIL_WIDTH = DIL_HEADS * HEAD_DIM
DIL_PATTERNS = ((128, 1), (512, 4), (2048, 16))
DIL_BLOCK = 128
FFN_DIM = 2816
N_EXPERTS = 8
EXPERT_DIM = 3584

LANES = 128
TILE_SUBLANES = D_MODEL // LANES
NEG_BIG = -1e30
LOG2E = math.log2(math.e)
DEN_ROWS = 16
BIAS_PARTS = 3
BIAS_LANES = 2 * BIAS_PARTS
VMEM_LIMIT = 56 * 1024 * 1024

ROW_TILE = 512
ATTN_TILE = 512
FOX_PAIRS_PER_STEP = 4
FFN_CHUNK = 256
MOE_CHUNK = 256
MOE_UNROLL = 2
MOE_TILE = 512


def _cparams(sem):
    return pltpu.CompilerParams(dimension_semantics=sem, vmem_limit_bytes=VMEM_LIMIT)


def _softplus_parts(x):
    return jnp.log(1.0 + jnp.exp(-jnp.abs(x)))


def _split_bf16(a, parts):
    out = []
    r = a
    for _ in range(parts):
        p = r.astype(BF16)
        out.append(p)
        r = r - p.astype(F32)
    return out


def _dot(a, b):
    return jnp.dot(a, b, preferred_element_type=F32)


def _dot_nt(a, b):
    return lax.dot_general(a, b, (((1,), (1,)), ((), ())), preferred_element_type=F32)


def _dot_exact_rhs(a_f32, b_bf16, parts):
    acc = None
    for p in _split_bf16(a_f32, parts):
        t = _dot(p, b_bf16)
        acc = t if acc is None else acc + t
    return acc


def _dot_exact_lhs(a_bf16, b_f32, parts):
    acc = None
    for p in _split_bf16(b_f32, parts):
        t = _dot(a_bf16, p)
        acc = t if acc is None else acc + t
    return acc


def _rms(x, g):
    return x * lax.rsqrt(jnp.mean(x * x, axis=-1, keepdims=True) + RMS_EPS) * g


def _head_norm(x, hm, gain):
    ms = _dot((x * x).astype(BF16), hm)
    return x * lax.rsqrt(ms + RMS_EPS) * gain


def _silu(x):
    return x * (1.0 / (1.0 + jnp.exp(-x)))


def _even_in_kernel(tiles_per_seq, x_ref, g_ref, wqk_ref, wvt_ref, wzx_ref, wfd_ref, qg_ref, kg_ref,
                    fdb_ref, hm_ref, tril_ref, pq_ref, pk_ref, qone_ref, kone_ref,
                    q_ref, qb_ref, k_ref, kb_ref, vt_ref, z_ref, xbc_ref, fd_ref, carry_ref):
    i = pl.program_id(0)

    @pl.when(i % tiles_per_seq == 0)
    def _():
        carry_ref[...] = jnp.zeros_like(carry_ref)

    h = _rms(x_ref[...], g_ref[...]).astype(BF16)
    qk = _dot(h, wqk_ref[...])
    hm = hm_ref[...]
    w = FOX_WIDTH
    q_ref[...] = (_head_norm(qk[:, :w], hm, qg_ref[...]) * (HEAD_DIM ** -0.5 * LOG2E)).astype(BF16)
    k_ref[...] = _head_norm(qk[:, w:], hm, kg_ref[...]).astype(BF16)
    vt_ref[...] = _dot_nt(wvt_ref[...], h).astype(BF16)
    zx = _dot(h, wzx_ref[...])
    z_ref[...] = zx[:, :SSD_INNER].astype(BF16)
    xbc_ref[...] = zx[:, SSD_INNER:]
    fd = _dot(h, wfd_ref[...]) + fdb_ref[...]
    t = _softplus_parts(fd)
    log_f = jnp.minimum(fd, 0.0) - t
    dt = jnp.maximum(fd, 0.0) + t
    c = _dot_exact_lhs(tril_ref[...], log_f, 2) + carry_ref[...]
    carry_ref[...] = c[-1:, :]
    lane = lax.broadcasted_iota(jnp.int32, fd.shape, 1)
    fd_ref[...] = jnp.where(lane < FOX_HEADS, c, dt)
    qb = qone_ref[...]
    kb = kone_ref[...]
    for j, part in enumerate(_split_bf16(c * LOG2E, 3)):
        qb = qb + _dot(part, pq_ref[j])
        kb = kb + _dot(part, pk_ref[j])
    qb_ref[...] = qb.astype(BF16)
    kb_ref[...] = kb.astype(BF16)


def _even_in_proj(x2d, seq, g, wqk, wvt, wzx, wfd, qg, kg, fdb, hm, tril, pq, pk, qone, kone):
    t_rows = x2d.shape[0]
    tm = min(ROW_TILE, seq)
    n = t_rows // tm
    row = lambda w: pl.BlockSpec((tm, w), lambda i: (i, 0))
    full = lambda a: pl.BlockSpec(a.shape, lambda i: (0,) * a.ndim)
    consts = (g, wqk, wvt, wzx, wfd, qg, kg, fdb, hm, tril, pq, pk, qone, kone)
    return pl.pallas_call(
        functools.partial(_even_in_kernel, seq // tm),
        grid=(n,),
        in_specs=[row(D_MODEL)] + [full(a) for a in consts],
        out_specs=[row(FOX_WIDTH)] * 4 + [pl.BlockSpec((FOX_WIDTH, tm), lambda i: (0, i)),
                                           row(SSD_INNER), row(SSD_CONV_DIM), row(LANES)],
        out_shape=[jax.ShapeDtypeStruct((t_rows, FOX_WIDTH), BF16)] * 4
        + [jax.ShapeDtypeStruct((FOX_WIDTH, t_rows), BF16),
           jax.ShapeDtypeStruct((t_rows, SSD_INNER), BF16),
           jax.ShapeDtypeStruct((t_rows, SSD_CONV_DIM), F32),
           jax.ShapeDtypeStruct((t_rows, LANES), F32)],
        scratch_shapes=[pltpu.VMEM((1, LANES), F32)],
        compiler_params=_cparams(("arbitrary",)),
        name="even_in_proj",
    )(x2d, *consts)


def _fox_kernel(qi_ref, ki_ref, q_ref, qb_ref, k_ref, kb_ref, vt_ref, o_ref, qs_ref, m_ref, l_ref, acc_ref):
    step_idx = pl.program_id(2)
    qi = qi_ref[step_idx]
    ki = ki_ref[step_idx]
    tq = q_ref.shape[0]
    tk = k_ref.shape[0]
    heads = range(2 * FOX_PAIRS_PER_STEP)
    lanes_of = lambda h: slice((h // 2) * LANES, (h // 2 + 1) * LANES)

    @pl.when(ki == 0)
    def _():
        m_ref[...] = jnp.full_like(m_ref, NEG_BIG)
        l_ref[...] = jnp.zeros_like(l_ref)
        acc_ref[...] = jnp.zeros_like(acc_ref)
        lane = lax.broadcasted_iota(jnp.int32, (tq, LANES), 1)
        zero = jnp.zeros((tq, LANES), q_ref.dtype)
        for h in heads:
            hh = h % 2
            qh = jnp.where((lane >= hh * HEAD_DIM) & (lane < (hh + 1) * HEAD_DIM), q_ref[:, lanes_of(h)], zero)
            bh = jnp.where((lane >= hh * BIAS_LANES) & (lane < (hh + 1) * BIAS_LANES), qb_ref[:, lanes_of(h)], zero)
            qs_ref[h] = jnp.concatenate([qh, bh], axis=1)

    def step(masked):
        if masked:
            keep = (lax.broadcasted_iota(jnp.int32, (tk, tq), 0)
                    <= lax.broadcasted_iota(jnp.int32, (tk, tq), 1))
        ka = [jnp.concatenate([k_ref[:, lanes_of(2 * p)], kb_ref[:, lanes_of(2 * p)]], axis=1)
              for p in range(FOX_PAIRS_PER_STEP)]
        scores = [_dot_nt(ka[h // 2], qs_ref[h]) for h in heads]
        if masked:
            scores = [jnp.where(keep, s, NEG_BIG) for s in scores]
        m_prev = [m_ref[h] for h in heads]
        m_new = [jnp.maximum(m_prev[h], jnp.max(scores[h], axis=0, keepdims=True)) for h in heads]
        probs = [jnp.exp2(scores[h] - m_new[h]) for h in heads]
        alpha = [jnp.exp2(m_prev[h] - m_new[h]) for h in heads]
        pv = [_dot(vt_ref[lanes_of(h), :], probs[h].astype(BF16)) for h in heads]
        for h in heads:
            l_ref[h] = alpha[h] * l_ref[h] + jnp.sum(probs[h], axis=0, keepdims=True)
            acc_ref[h] = alpha[h] * acc_ref[h] + pv[h]
            m_ref[h] = m_new[h]

    @pl.when(ki < qi)
    def _():
        step(False)

    @pl.when(ki == qi)
    def _():
        step(True)
        for p in range(FOX_PAIRS_PER_STEP):
            o0 = acc_ref[2 * p] / l_ref[2 * p]
            o1 = acc_ref[2 * p + 1] / l_ref[2 * p + 1]
            o_t = jnp.concatenate([o0[:HEAD_DIM], o1[HEAD_DIM:]], axis=0)
            o_ref[:, lanes_of(2 * p)] = o_t.T.astype(o_ref.dtype)


def _fox_attention(q, qb, k, kb, vt, batch, seq):
    t = min(ATTN_TILE, seq)
    n = seq // t
    w = FOX_PAIRS_PER_STEP * LANES
    groups = FOX_WIDTH // w
    tri = [(i, j) for i in range(n) for j in range(i + 1)]
    qi_tab = jnp.asarray([i for i, _ in tri], jnp.int32)
    ki_tab = jnp.asarray([j for _, j in tri], jnp.int32)
    qspec = pl.BlockSpec((t, w), lambda bi, g, s, qi, ki: (bi * n + qi[s], g))
    kspec = pl.BlockSpec((t, w), lambda bi, g, s, qi, ki: (bi * n + ki[s], g))
    vspec = pl.BlockSpec((w, t), lambda bi, g, s, qi, ki: (g, bi * n + ki[s]))
    heads = 2 * FOX_PAIRS_PER_STEP
    grid_spec = pltpu.PrefetchScalarGridSpec(
        num_scalar_prefetch=2,
        grid=(batch, groups, len(tri)),
        in_specs=[qspec, qspec, kspec, kspec, vspec],
        out_specs=qspec,
        scratch_shapes=[pltpu.VMEM((heads, t, 2 * LANES), BF16), pltpu.VMEM((heads, 1, t), F32),
                        pltpu.VMEM((heads, 1, t), F32), pltpu.VMEM((heads, LANES, t), F32)],
    )
    return pl.pallas_call(
        _fox_kernel,
        grid_spec=grid_spec,
        out_shape=jax.ShapeDtypeStruct((batch * seq, FOX_WIDTH), BF16),
        compiler_params=_cparams(("arbitrary",) * 3),
        name="fox_attention",
    )(qi_tab, ki_tab, q, qb, k, kb, vt)


def _ssd_kernel(xbc_ref, prev_ref, fd_ref, dtt_ref, z_ref, cw_ref, cb_ref, alog_ref, alogc_ref,
                dsk_ref, nrm_ref, tri_ref, y_ref, state_ref):
    c_idx = pl.program_id(1)
    L = SSD_CHUNK
    P = HEAD_DIM
    N = SSD_STATE

    @pl.when(c_idx == 0)
    def _():
        state_ref[...] = jnp.zeros_like(state_ref)

    cur = xbc_ref[...]
    prev = jnp.where(c_idx > 0, prev_ref[...], 0.0)
    ext = jnp.concatenate([prev, cur], axis=0)
    cw = cw_ref[...]
    conv = cb_ref[...] + cw[SSD_CONV - 1:SSD_CONV, :] * cur
    for kk in range(SSD_CONV - 1):
        sh = SSD_CONV - 1 - kk
        conv = conv + cw[kk:kk + 1, :] * ext[8 - sh:8 - sh + L, :]
    xc = _silu(conv)
    xs = xc[:, :SSD_INNER]
    bm = xc[:, SSD_INNER:SSD_INNER + SSD_GROUPS * N]
    cm = xc[:, SSD_INNER + SSD_GROUPS * N:]

    dt = fd_ref[:, FOX_HEADS:FOX_HEADS + SSD_HEADS]
    a_row = -jnp.exp(alog_ref[...])
    a_col = -jnp.exp(alogc_ref[...])
    tri = tri_ref[...]
    cum = _dot_exact_lhs(tri, dt * a_row, 2)
    cum_t = lax.dot_general(_split_bf16(dtt_ref[...] * a_col, 2)[0], tri, (((1,), (1,)), ((), ())),
                            preferred_element_type=F32)
    cum_t = cum_t + lax.dot_general(_split_bf16(dtt_ref[...] * a_col, 2)[1], tri,
                                    (((1,), (1,)), ((), ())), preferred_element_type=F32)
    row_i = lax.broadcasted_iota(jnp.int32, (L, L), 0)
    col_i = lax.broadcasted_iota(jnp.int32, (L, L), 1)
    causal = row_i >= col_i

    ys = []
    for g in range(SSD_GROUPS):
        bm_g = bm[:, g * N:(g + 1) * N]
        cm_g = cm[:, g * N:(g + 1) * N]
        cb_g = _dot_nt(cm_g.astype(BF16), bm_g.astype(BF16))
        bm_t = bm_g.T
        for hh in range(SSD_HEADS // SSD_GROUPS):
            h = g * (SSD_HEADS // SSD_GROUPS) + hh
            cum_c = cum[:, h:h + 1]
            cum_r = cum_t[h:h + 1, :]
            seg = jnp.where(causal, jnp.exp(jnp.minimum(cum_c - cum_r, 0.0)), 0.0)
            x_h = xs[:, h * P:(h + 1) * P]
            xdt = (x_h * dt[:, h:h + 1]).astype(BF16)
            y_h = _dot((cb_g * seg).astype(BF16), xdt)
            h_prev = state_ref[h]
            y_h = y_h + _dot((cm_g * jnp.exp(cum_c)).astype(BF16), h_prev.astype(BF16))
            last = cum_t[h:h + 1, L - 1:L]
            dte = jnp.exp(last - cum_r)
            st = _dot((bm_t * dte).astype(BF16), xdt)
            state_ref[h] = jnp.exp(last) * h_prev + st
            ys.append(y_h)
    y = jnp.concatenate(ys, axis=1) + xs * dsk_ref[...]
    y = y * _silu(z_ref[...].astype(F32))
    gw = SSD_INNER // SSD_GROUPS
    outs = []
    for g in range(SSD_GROUPS):
        yg = y[:, g * gw:(g + 1) * gw]
        outs.append(yg * lax.rsqrt(jnp.mean(yg * yg, axis=-1, keepdims=True) + RMS_EPS))
    y_ref[...] = (jnp.concatenate(outs, axis=1) * nrm_ref[...]).astype(y_ref.dtype)


def _ssd(xbc, fd, dtt, z, conv_w, conv_b, a_log, d_skip_c, ssd_norm, tri, batch, seq):
    L = SSD_CHUNK
    nc = seq // L
    t_rows = batch * seq
    row = lambda w: pl.BlockSpec((L, w), lambda b, c: (b * nc + c, 0))
    full = lambda a: pl.BlockSpec(a.shape, lambda b, c: (0,) * a.ndim)
    prev = pl.BlockSpec((8, SSD_CONV_DIM), lambda b, c: (jnp.maximum((b * nc + c) * (L // 8) - 1, 0), 0))
    return pl.pallas_call(
        _ssd_kernel,
        grid=(batch, nc),
        in_specs=[row(SSD_CONV_DIM), prev, row(LANES),
                  pl.BlockSpec((None, SSD_HEADS, L), lambda b, c: (b, 0, c)),
                  row(SSD_INNER), full(conv_w), full(conv_b), full(a_log[None, :]),
                  full(a_log[:, None]), full(d_skip_c), full(ssd_norm), full(tri)],
        out_specs=row(SSD_INNER),
        out_shape=jax.ShapeDtypeStruct((t_rows, SSD_INNER), BF16),
        scratch_shapes=[pltpu.VMEM((SSD_HEADS, SSD_STATE, HEAD_DIM), F32)],
        compiler_params=_cparams(("arbitrary", "arbitrary")),
        name="ssd_scan",
    )(xbc, xbc, fd, dtt, z, conv_w, conv_b, a_log[None, :], a_log[:, None], d_skip_c, ssd_norm, tri)


def _swiglu_hidden_chunk(h, w13_ref, act_ref, c, tf):
    f = w13_ref.shape[1] // 2
    off = pl.multiple_of(c * tf, tf)
    gate = _dot(h, w13_ref[:, pl.ds(off, tf)])
    up = _dot(h, w13_ref[:, pl.ds(f + off, tf)])
    act_ref[:, pl.ds(off, tf)] = (_silu(gate) * up).astype(BF16)


def _even_out_kernel(o_ref, y_ref, x_ref, wo_ref, g_ref, w13_ref, w2_ref, out_ref, act_ref):
    w = FOX_WIDTH
    x1 = x_ref[...] + _dot(o_ref[...], wo_ref[:w, :]) + _dot(y_ref[...], wo_ref[w:, :])
    h = _rms(x1, g_ref[...]).astype(BF16)

    def body(c, carry):
        _swiglu_hidden_chunk(h, w13_ref, act_ref, c, FFN_CHUNK)
        return carry

    lax.fori_loop(0, w2_ref.shape[0] // FFN_CHUNK, body, 0)
    out_ref[...] = x1 + _dot(act_ref[...], w2_ref[...])


def _even_out_ffn(o_fox, y_ssd, x2d, wo, g, w13, w2):
    t_rows = x2d.shape[0]
    tm = min(ROW_TILE, t_rows)
    row = lambda w: pl.BlockSpec((tm, w), lambda i: (i, 0))
    full = lambda a: pl.BlockSpec(a.shape, lambda i: (0,) * a.ndim, pipeline_mode=pl.Buffered(1))
    return pl.pallas_call(
        _even_out_kernel,
        grid=(t_rows // tm,),
        in_specs=[row(FOX_WIDTH), row(SSD_INNER), row(D_MODEL), full(wo), full(g), full(w13), full(w2)],
        out_specs=row(D_MODEL),
        out_shape=jax.ShapeDtypeStruct((t_rows, D_MODEL), F32),
        scratch_shapes=[pltpu.VMEM((tm, w2.shape[0]), BF16)],
        compiler_params=_cparams(("arbitrary",)),
        name="even_out_ffn",
    )(o_fox, y_ssd, x2d, wo, g, w13, w2)


def _rope(x, cos, sin_signed):
    n = x.shape[1]
    lane = lax.broadcasted_iota(jnp.int32, x.shape, 1)
    first = (lane % HEAD_DIM) < (HEAD_DIM // 2)
    rot = jnp.where(first, pltpu.roll(x, n - HEAD_DIM // 2, 1), pltpu.roll(x, HEAD_DIM // 2, 1))
    reps = n // LANES
    cos_t = jnp.concatenate([cos] * reps, axis=1)
    sin_t = jnp.concatenate([sin_signed] * reps, axis=1)
    return x * cos_t + rot * sin_t


def _emit_dilated_views(val, refs, slab_ref):
    tm, width = val.shape
    chunks = width // LANES
    for c in range(chunks):
        slab_ref[c * tm:(c + 1) * tm, :] = val[:, c * LANES:(c + 1) * LANES]
    for (_, d), ref in zip(DIL_PATTERNS, refs):
        if d == 1:
            ref[...] = val.astype(ref.dtype)
            continue
        n = tm // d
        for r in range(d):
            for c in range(chunks):
                piece = slab_ref[pl.ds(c * tm + r, n, stride=d), :]
                ref[:, r * width + c * LANES:r * width + (c + 1) * LANES] = piece.astype(ref.dtype)


def _read_dilated_view(ref, d, tm, slab_ref):
    if d == 1:
        return ref[...].astype(F32)
    width = ref.shape[1] // d
    chunks = width // LANES
    n = tm // d
    for r in range(d):
        for c in range(chunks):
            piece = ref[:, r * width + c * LANES:r * width + (c + 1) * LANES].astype(F32)
            slab_ref[pl.ds(c * tm + r, n, stride=d), :] = piece
    return jnp.concatenate([slab_ref[c * tm:(c + 1) * tm, :] for c in range(chunks)], axis=1)


def _odd_qkv_kernel(x_ref, g_ref, w_ref, qg_ref, kg_ref, hm_ref, cos_ref, sin_ref, *rest):
    n_pat = len(DIL_PATTERNS)
    q_refs, k_refs, v_refs = rest[:n_pat], rest[n_pat:2 * n_pat], rest[2 * n_pat:3 * n_pat]
    slabs = rest[3 * n_pat:]
    h = _rms(x_ref[...], g_ref[...]).astype(BF16)
    hm = hm_ref[...]
    w = DIL_WIDTH
    cos = cos_ref[...]
    sin = sin_ref[...]
    q = _head_norm(_dot(h, w_ref[:, :w]), hm, qg_ref[...])
    _emit_dilated_views(_rope(q, cos, sin) * (HEAD_DIM ** -0.5 * LOG2E), q_refs, slabs[0])
    k = _head_norm(_dot(h, w_ref[:, w:2 * w]), hm, kg_ref[...])
    _emit_dilated_views(_rope(k, cos, sin), k_refs, slabs[1])
    _emit_dilated_views(_dot(h, w_ref[:, 2 * w:]), v_refs, slabs[2])


def _odd_qkv(x2d, seq, g, w, qg, kg, hm, cos, sin):
    t_rows = x2d.shape[0]
    tm = min(ROW_TILE, seq)
    per_seq = seq // tm
    row = lambda rows, wd: pl.BlockSpec((rows, wd), lambda i: (i, 0))
    full = lambda a: pl.BlockSpec(a.shape, lambda i: (0,) * a.ndim)
    tab = pl.BlockSpec((tm, LANES), lambda i: (i % per_seq, 0))
    views = [(tm // d, t_rows // d, d * DIL_WIDTH) for (_, d) in DIL_PATTERNS]
    outs = pl.pallas_call(
        _odd_qkv_kernel,
        grid=(t_rows // tm,),
        in_specs=[row(tm, D_MODEL), full(g), full(w), full(qg), full(kg), full(hm), tab, tab],
        out_specs=[row(r, wd) for (r, _, wd) in views] * 3,
        out_shape=[jax.ShapeDtypeStruct((n, wd), BF16) for (_, n, wd) in views] * 3,
        scratch_shapes=[pltpu.VMEM((tm * (DIL_WIDTH // LANES), LANES), F32)] * 3,
        compiler_params=_cparams(("arbitrary",)),
        name="odd_qkv",
    )(x2d, g, w, qg, kg, hm, cos, sin)
    n_pat = len(DIL_PATTERNS)
    return outs[:n_pat], outs[n_pat:2 * n_pat], outs[2 * n_pat:]


def _dilated_kernel(q_ref, kp_ref, kc_ref, vp_ref, vc_ref, band_ref, o_ref, lse_ref):
    j = pl.program_id(2)
    blk = DIL_BLOCK
    key_row = lax.broadcasted_iota(jnp.int32, (2 * blk, blk), 0)
    bias = jnp.where((j == 0) & (key_row < blk), NEG_BIG, band_ref[...])
    lane = lax.broadcasted_iota(jnp.int32, (blk, LANES), 1)
    ones_rows = jnp.ones((DEN_ROWS, 2 * blk), BF16)
    pairs = range(DIL_HEADS // 2)
    heads = range(DIL_HEADS)
    lanes_of = lambda pr: slice(pr * LANES, (pr + 1) * LANES)
    kk = [jnp.concatenate([kp_ref[:, lanes_of(pr)], kc_ref[:, lanes_of(pr)]], axis=0) for pr in pairs]
    vt = [jnp.concatenate([jnp.concatenate([vp_ref[:, lanes_of(pr)], vc_ref[:, lanes_of(pr)]], axis=0)
                           .astype(F32).T.astype(BF16), ones_rows], axis=0) for pr in pairs]
    zero = jnp.zeros((blk, LANES), q_ref.dtype)
    qm = [jnp.where((lane >= (h % 2) * HEAD_DIM) & (lane < (h % 2 + 1) * HEAD_DIM), q_ref[:, lanes_of(h // 2)], zero)
          for h in heads]
    s = [_dot_nt(kk[h // 2], qm[h]) + bias for h in heads]
    m = [jnp.max(s[h], axis=0, keepdims=True) for h in heads]
    e = [jnp.exp2(s[h] - m[h]).astype(BF16) for h in heads]
    ov = [_dot(vt[h // 2], e[h]) for h in heads]
    den = [ov[h][LANES:LANES + 1, :] for h in heads]
    for pr in pairs:
        o_t = jnp.concatenate([(ov[2 * pr][:LANES] / den[2 * pr])[:HEAD_DIM],
                               (ov[2 * pr + 1][:LANES] / den[2 * pr + 1])[HEAD_DIM:]], axis=0)
        o_ref[:, lanes_of(pr)] = o_t.T.astype(o_ref.dtype)
    lse_t = jnp.concatenate([m[h] + jnp.log2(den[h]) for h in heads]
                            + [jnp.zeros((LANES - DIL_HEADS, blk), F32)], axis=0)
    lse_ref[...] = lse_t.T


def _dilated_branch(q, k, v, band, dilation, batch):
    d = dilation
    w = DIL_WIDTH
    nb = q.shape[0] // batch // DIL_BLOCK
    cur = pl.BlockSpec((DIL_BLOCK, w), lambda bi, r, j: (bi * nb + j, r))
    prev = pl.BlockSpec((DIL_BLOCK, w), lambda bi, r, j: (bi * nb + jnp.maximum(j - 1, 0), r))
    lse_spec = pl.BlockSpec((DIL_BLOCK, LANES), lambda bi, r, j: (bi * nb + j, r))
    return pl.pallas_call(
        _dilated_kernel,
        grid=(batch, d, nb),
        in_specs=[cur, prev, cur, prev, cur, pl.BlockSpec(band.shape, lambda bi, r, j: (0, 0))],
        out_specs=[cur, lse_spec],
        out_shape=[jax.ShapeDtypeStruct(q.shape, BF16),
                   jax.ShapeDtypeStruct((q.shape[0], d * LANES), F32)],
        compiler_params=_cparams(("arbitrary",) * 3),
        name=f"dilated_attn_d{d}",
    )(q, k, k, v, v, band)


def _odd_out_kernel(o1_ref, o2_ref, o3_ref, l1_ref, l2_ref, l3_ref, x_ref, wo_ref, g_ref,
                    rw_ref, ex_ref, tril_ref, x1_ref, h_ref, info_ref, cnt_ref, carry_ref, slab_ref):
    i = pl.program_id(0)
    tm = x_ref.shape[0]

    @pl.when(i == 0)
    def _():
        carry_ref[...] = jnp.zeros_like(carry_ref)

    dils = [d for (_, d) in DIL_PATTERNS]
    l1, l2, l3 = [_read_dilated_view(r, d, tm, slab_ref) for r, d in zip((l1_ref, l2_ref, l3_ref), dils)]
    mx = jnp.maximum(jnp.maximum(l1, l2), l3)
    a1, a2, a3 = jnp.exp2(l1 - mx), jnp.exp2(l2 - mx), jnp.exp2(l3 - mx)
    inv = 1.0 / (a1 + a2 + a3)
    ex = ex_ref[...]
    o = None
    for a, o_ref, d in zip((a1, a2, a3), (o1_ref, o2_ref, o3_ref), dils):
        term = _dot_exact_rhs(a * inv, ex, 1) * _read_dilated_view(o_ref, d, tm, slab_ref)
        o = term if o is None else o + term
    x1 = x_ref[...] + _dot(o.astype(BF16), wo_ref[...])
    x1_ref[...] = x1
    h = _rms(x1, g_ref[...])
    _wide_to_rows(h_ref, h)

    h_hi, h_lo = _split_bf16(h, 2)
    w_hi, w_lo = _split_bf16(rw_ref[...], 2)
    logits = _dot(h_hi, w_hi) + (_dot(h_lo, w_hi) + _dot(h_hi, w_lo))
    lane = lax.broadcasted_iota(jnp.int32, logits.shape, 1)
    logits = jnp.where(lane < N_EXPERTS, logits, -jnp.inf)
    m1 = jnp.max(logits, axis=-1, keepdims=True)
    i1 = jnp.min(jnp.where(logits == m1, lane, LANES), axis=-1, keepdims=True)
    rest = jnp.where(lane == i1, -jnp.inf, logits)
    m2 = jnp.max(rest, axis=-1, keepdims=True)
    i2 = jnp.min(jnp.where(rest == m2, lane, LANES), axis=-1, keepdims=True)
    e2 = jnp.exp(m2 - m1)
    g1 = 1.0 / (1.0 + e2)
    g2 = e2 * g1
    hot1 = lane == i1
    hot2 = lane == i2
    onehot = jnp.where(hot1 | hot2, 1.0, 0.0).astype(BF16)
    before = _dot(tril_ref[...], onehot) + carry_ref[...]
    r1 = jnp.sum(jnp.where(hot1, before, 0.0), axis=-1, keepdims=True)
    r2 = jnp.sum(jnp.where(hot2, before, 0.0), axis=-1, keepdims=True)
    total = before[-1:, :] + onehot[-1:, :].astype(F32)
    carry_ref[...] = total
    cnt_ref[...] = jnp.broadcast_to(total, cnt_ref.shape)
    info = jnp.where(lane == 0, i1.astype(F32), 0.0)
    info = jnp.where(lane == 1, i2.astype(F32), info)
    info = jnp.where(lane == 2, g1, info)
    info = jnp.where(lane == 3, g2, info)
    info = jnp.where(lane == 4, r1, info)
    info = jnp.where(lane == 5, r2, info)
    info_ref[...] = info


def _odd_out_router(o1, o2, o3, l1, l2, l3, x2d, wo, g, rw, ex, tril_strict):
    t_rows = x2d.shape[0]
    tm = min(ROW_TILE, t_rows)
    n = t_rows // tm
    row = lambda w: pl.BlockSpec((tm, w), lambda i: (i, 0))
    full = lambda a: pl.BlockSpec(a.shape, lambda i: (0,) * a.ndim)
    view = lambda w: [pl.BlockSpec((tm // d, d * w), lambda i: (i, 0)) for (_, d) in DIL_PATTERNS]
    return pl.pallas_call(
        _odd_out_kernel,
        grid=(n,),
        in_specs=view(DIL_WIDTH) + view(LANES) + [row(D_MODEL), full(wo), full(g), full(rw),
                                                  full(ex), full(tril_strict)],
        out_specs=[row(D_MODEL), pl.BlockSpec((tm * TILE_SUBLANES, LANES), lambda i: (i, 0)), row(LANES),
                   pl.BlockSpec((8, LANES), lambda i: (0, 0))],
        out_shape=[jax.ShapeDtypeStruct((t_rows, D_MODEL), F32),
                   jax.ShapeDtypeStruct((t_rows * TILE_SUBLANES, LANES), F32),
                   jax.ShapeDtypeStruct((t_rows, LANES), F32), jax.ShapeDtypeStruct((8, LANES), F32)],
        scratch_shapes=[pltpu.VMEM((1, LANES), F32), pltpu.VMEM((tm * (DIL_WIDTH // LANES), LANES), F32)],
        compiler_params=_cparams(("arbitrary",)),
        name="odd_out_router",
    )(o1, o2, o3, l1, l2, l3, x2d, wo, g, rw, ex, tril_strict)


def _rows_to_wide(ref):
    n = ref.shape[0] // TILE_SUBLANES
    return jnp.concatenate([ref[pl.ds(c, n, stride=TILE_SUBLANES), :] for c in range(TILE_SUBLANES)], axis=1)


def _wide_to_rows(ref, val):
    n = ref.shape[0] // TILE_SUBLANES
    for c in range(TILE_SUBLANES):
        ref[pl.ds(c, n, stride=TILE_SUBLANES), :] = val[:, c * LANES:(c + 1) * LANES]


def _moe_kernel(te_ref, nu_ref, src_cur_ref, src_nxt_ref, dst_prv_ref, dst_cur_ref, h_hbm, w13_ref, w2_ref,
                y_hbm, xa, xb, ya, yb, act_ref, gsem, ssem):
    i = pl.program_id(0)
    n_used = nu_ref[0]
    ts = TILE_SUBLANES
    gm = xa.shape[0] // ts
    n_chunks = w2_ref.shape[0] // MOE_CHUNK
    per_chunk = gm // n_chunks
    head_rows = gm - n_chunks * per_chunk

    def tile_of(ref, row):
        return ref.at[pl.ds(pl.multiple_of(row * ts, ts), ts)]

    def gather_row(src_ref, r, xdst, sem):
        pltpu.make_async_copy(tile_of(h_hbm, src_ref[0, 0, r]), tile_of(xdst, r), sem).start()

    def scatter_row(dst_ref, r, ysrc, sem):
        pltpu.make_async_copy(tile_of(ysrc, r), tile_of(y_hbm, dst_ref[0, 0, r]), sem).start()

    def wait_rows(buf, sem):
        pltpu.make_async_copy(h_hbm.at[pl.ds(0, gm * ts)], buf, sem).wait()

    @pl.when(i == 0)
    def _():
        yb[...] = jnp.zeros_like(yb)

        def body(r, carry):
            gather_row(src_cur_ref, r, xa, gsem.at[0])
            return carry
        lax.fori_loop(0, gm, body, 0)

    def tile(x_cur, x_nxt, y_cur, y_prv, p):
        wait_rows(x_cur, gsem.at[p])

        @pl.when(i >= 1)
        def _():
            wait_rows(y_cur, ssem.at[p])

        def issue(r):
            gather_row(src_nxt_ref, r, x_nxt, gsem.at[1 - p])
            scatter_row(dst_prv_ref, r, y_prv, ssem.at[1 - p])

        for r in range(head_rows):
            issue(n_chunks * per_chunk + r)
        x = _rows_to_wide(x_cur).astype(BF16)

        def body(c, carry):
            for u in range(MOE_UNROLL):
                _swiglu_hidden_chunk(x, w13_ref, act_ref, c * MOE_UNROLL + u, MOE_CHUNK)
            for j in range(per_chunk * MOE_UNROLL):
                issue(c * (per_chunk * MOE_UNROLL) + j)
            return carry

        lax.fori_loop(0, n_chunks // MOE_UNROLL, body, 0)
        _wide_to_rows(y_cur, _dot(act_ref[...], w2_ref[...]))

        @pl.when(i == n_used - 1)
        def _():
            def body(r, carry):
                scatter_row(dst_cur_ref, r, y_cur, ssem.at[p])
                return carry
            lax.fori_loop(0, gm, body, 0)
            wait_rows(y_prv, ssem.at[1 - p])
            wait_rows(y_cur, ssem.at[p])
            wait_rows(x_nxt, gsem.at[1 - p])

    @pl.when((i < n_used) & (i % 2 == 0))
    def _():
        tile(xa, xb, ya, yb, 0)

    @pl.when((i < n_used) & (i % 2 == 1))
    def _():
        tile(xb, xa, yb, ya, 1)


def _moe_ffn(h_tiles, w13, w2, tile_expert, n_used, src, dst, out_rows):
    nt = tile_expert.shape[0]
    gm = src.shape[-1]
    tab = lambda f: pl.BlockSpec((1, 1, gm), lambda i, te, nu: (f(i), 0, 0), memory_space=pltpu.SMEM)
    grid_spec = pltpu.PrefetchScalarGridSpec(
        num_scalar_prefetch=2,
        grid=(nt,),
        in_specs=[
            tab(lambda i: i),
            tab(lambda i: jnp.minimum(i + 1, nt - 1)),
            tab(lambda i: i),
            tab(lambda i: i + 1),
            pl.BlockSpec(memory_space=pl.ANY),
            pl.BlockSpec((None,) + w13.shape[1:], lambda i, te, nu: (te[i], 0, 0), pipeline_mode=pl.Buffered(1)),
            pl.BlockSpec((None,) + w2.shape[1:], lambda i, te, nu: (te[i], 0, 0), pipeline_mode=pl.Buffered(1)),
        ],
        out_specs=pl.BlockSpec(memory_space=pl.ANY),
        scratch_shapes=[pltpu.VMEM((gm * TILE_SUBLANES, LANES), F32)] * 4
        + [pltpu.VMEM((gm, w2.shape[1]), BF16), pltpu.SemaphoreType.DMA((2,)), pltpu.SemaphoreType.DMA((2,))],
    )
    return pl.pallas_call(
        _moe_kernel,
        grid_spec=grid_spec,
        out_shape=jax.ShapeDtypeStruct((out_rows * TILE_SUBLANES, LANES), F32),
        compiler_params=_cparams(("arbitrary",)),
        name="moe_ffn",
    )(tile_expert, n_used, src, src, dst, dst, h_tiles, w13, w2)


def _combine_kernel(x_ref, y0_ref, y1_ref, info_ref, o_ref):
    o_ref[...] = (x_ref[...] + info_ref[:, 2:3] * _rows_to_wide(y0_ref)
                  + info_ref[:, 3:4] * _rows_to_wide(y1_ref))


def _combine(x1, y, info):
    t_rows = x1.shape[0]
    tm = min(ROW_TILE, t_rows)
    row = lambda w: pl.BlockSpec((tm, w), lambda i: (i, 0))
    ysp = lambda kk: pl.BlockSpec((None, tm * TILE_SUBLANES, LANES), lambda i: (kk, i, 0))
    return pl.pallas_call(
        _combine_kernel,
        grid=(t_rows // tm,),
        in_specs=[row(D_MODEL), ysp(0), ysp(1), row(LANES)],
        out_specs=row(D_MODEL),
        out_shape=jax.ShapeDtypeStruct((t_rows, D_MODEL), F32),
        compiler_params=_cparams(("arbitrary",)),
        name="moe_combine",
    )(x1, y, y, info)


def _head_mean_matrix(width):
    idx = np.arange(width) // HEAD_DIM
    return jnp.asarray((idx[:, None] == idx[None, :]).astype(np.float32) / HEAD_DIM, dtype=BF16)


def _tril(n, strict):
    r = np.arange(n)
    m = (r[None, :] < r[:, None]) if strict else (r[None, :] <= r[:, None])
    return jnp.asarray(m.astype(np.float32), dtype=BF16)


def _rope_tables(seq):
    half = HEAD_DIM // 2
    inv_freq = ROPE_THETA ** (-jnp.arange(half, dtype=F32) / half)
    ang = jnp.arange(seq, dtype=F32)[:, None] * inv_freq[None, :]
    cos = jnp.cos(ang)
    sin = jnp.sin(ang)
    cos_t = jnp.concatenate([cos, cos, cos, cos], axis=1)
    sin_t = jnp.concatenate([-sin, sin, -sin, sin], axis=1)
    return cos_t, sin_t


def _band_bias():
    qi = np.arange(DIL_BLOCK)[None, :]
    ki = np.arange(-DIL_BLOCK, DIL_BLOCK)[:, None]
    rel = qi - ki
    return jnp.asarray(np.where((rel >= 0) & (rel <= DIL_BLOCK), 0.0, NEG_BIG).astype(np.float32))


def _head_expand():
    m = np.zeros((LANES, DIL_WIDTH), np.float32)
    for h in range(DIL_HEADS):
        m[h, h * HEAD_DIM:(h + 1) * HEAD_DIM] = 1.0
    return jnp.asarray(m, dtype=BF16)


def _bias_placement():
    pq = np.zeros((BIAS_PARTS, LANES, FOX_WIDTH), np.float32)
    pk = np.zeros((BIAS_PARTS, LANES, FOX_WIDTH), np.float32)
    qone = np.zeros((1, FOX_WIDTH), np.float32)
    kone = np.zeros((1, FOX_WIDTH), np.float32)
    for head in range(FOX_HEADS):
        base = (head // 2) * LANES + (head % 2) * BIAS_LANES
        for j in range(BIAS_PARTS):
            pq[j, head, base + j] = 1.0
            pk[j, head, base + BIAS_PARTS + j] = -1.0
            qone[0, base + BIAS_PARTS + j] = 1.0
            kone[0, base + j] = 1.0
    return (jnp.asarray(pq, dtype=BF16), jnp.asarray(pk, dtype=BF16), jnp.asarray(qone), jnp.asarray(kone))


def _even_layer(x2d, batch, seq, norm1, w_in, fox_qn, fox_kn, fox_fbias, conv_w, conv_b, dt_bias, a_log,
                d_skip, ssd_norm, w_out, norm2, ffn_w13, ffn_w2):
    w = FOX_WIDTH
    o_z = 3 * w + FOX_HEADS
    o_x = o_z + SSD_INNER
    o_dt = o_x + SSD_CONV_DIM
    wqk = w_in[:, :2 * w].astype(BF16)
    wvt = w_in[:, 2 * w:3 * w].T.astype(BF16)
    wzx = w_in[:, o_z:o_dt].astype(BF16)
    pad = LANES - FOX_HEADS - SSD_HEADS
    wfd = jnp.concatenate([w_in[:, 3 * w:o_z], w_in[:, o_dt:], jnp.zeros((D_MODEL, pad), F32)], axis=1).astype(BF16)
    fdb = jnp.concatenate([fox_fbias, dt_bias, jnp.zeros((pad,), F32)])[None, :]
    tm = min(ROW_TILE, seq)
    q, qb, k, kb, vt, z, xbc, fd = _even_in_proj(
        x2d, seq, norm1[None, :], wqk, wvt, wzx, wfd, fox_qn.reshape(1, w), fox_kn.reshape(1, w), fdb,
        _head_mean_matrix(w), _tril(tm, False), *_bias_placement())

    o_fox = _fox_attention(q, qb, k, kb, vt, batch, seq)

    dtt = jnp.transpose(fd[:, FOX_HEADS:FOX_HEADS + SSD_HEADS].reshape(batch, seq, SSD_HEADS), (0, 2, 1))
    d_skip_c = jnp.repeat(d_skip, HEAD_DIM)[None, :]
    y = _ssd(xbc, fd, dtt, z, conv_w, conv_b[None, :], a_log, d_skip_c, ssd_norm[None, :],
             _tril(SSD_CHUNK, False), batch, seq)

    return _even_out_ffn(o_fox, y, x2d, w_out.astype(BF16), norm2[None, :],
                         ffn_w13.astype(BF16), ffn_w2.astype(BF16))


def _moe_tables(info, counts, t_rows, gm):
    e1 = info[:, 0].astype(jnp.int32)
    e2 = info[:, 1].astype(jnp.int32)
    r1 = info[:, 4].astype(jnp.int32)
    r2 = info[:, 5].astype(jnp.int32)
    cnt = counts[0, :N_EXPERTS].astype(jnp.int32)
    tiles = (cnt + gm - 1) // gm
    tile_end = jnp.cumsum(tiles)
    offs = (tile_end - tiles) * gm
    nt = (2 * t_rows) // gm + N_EXPERTS
    tile_idx = jnp.arange(nt, dtype=jnp.int32)
    tile_expert = jnp.minimum(jnp.sum((tile_idx[:, None] >= tile_end[None, :]).astype(jnp.int32), axis=1),
                              N_EXPERTS - 1)
    n_used = tile_end[-1:].astype(jnp.int32)
    slots = jnp.concatenate([offs[e1] + r1, offs[e2] + r2])
    codes = jnp.full((nt * gm,), -1, jnp.int32).at[slots].set(
        jnp.arange(2 * t_rows, dtype=jnp.int32), unique_indices=True)
    out_half = t_rows + N_EXPERTS * gm
    second = codes >= t_rows
    src = jnp.where(codes < 0, 0, jnp.where(second, codes - t_rows, codes))
    spare = t_rows + jnp.repeat(tile_expert, gm) * gm + jnp.tile(jnp.arange(gm, dtype=jnp.int32), nt)
    dst = jnp.where(codes < 0, spare, jnp.where(second, codes - t_rows + out_half, codes))
    dst = jnp.concatenate([t_rows + jnp.arange(gm, dtype=jnp.int32), dst])
    return tile_expert, n_used, src.reshape(nt, 1, gm), dst.reshape(nt + 1, 1, gm), out_half


def _odd_layer(x2d, batch, seq, norm1, w_qkv, qn, kn, w_out, norm2, router, moe_w13, moe_w2):
    t_rows = batch * seq
    cos, sin = _rope_tables(seq)
    qs, ks, vs = _odd_qkv(x2d, seq, norm1[None, :], w_qkv.astype(BF16), qn.reshape(1, DIL_WIDTH),
                          kn.reshape(1, DIL_WIDTH), _head_mean_matrix(DIL_WIDTH), cos, sin)
    band = _band_bias()
    outs, lses = zip(*[_dilated_branch(q, k, v, band, d, batch)
                       for q, k, v, (_, d) in zip(qs, ks, vs, DIL_PATTERNS)])
    rw = jnp.concatenate([router, jnp.zeros((D_MODEL, LANES - N_EXPERTS), F32)], axis=1)
    tm = min(ROW_TILE, t_rows)
    x1, h, info, counts = _odd_out_router(*outs, *lses, x2d, w_out.astype(BF16), norm2[None, :], rw,
                                          _head_expand(), _tril(tm, True))
    gm = min(MOE_TILE, t_rows)
    tile_expert, n_used, src, dst, out_half = _moe_tables(info, counts, t_rows, gm)
    y = _moe_ffn(h, moe_w13.astype(BF16), moe_w2.astype(BF16), tile_expert, n_used, src, dst, 2 * out_half)
    return _combine(x1, y.reshape(2, out_half * TILE_SUBLANES, LANES), info)


def kernel(x, e_norm1, e_w_in, e_fox_qn, e_fox_kn, e_fox_fbias, e_conv_w, e_conv_b, e_dt_bias, e_a_log, e_d_skip, e_ssd_norm, e_w_out, e_norm2, e_ffn_w13, e_ffn_w2, o_norm1, o_w_qkv, o_qn, o_kn, o_w_out, o_norm2, o_router, o_moe_w13, o_moe_w2):
    batch, seq, _ = x.shape
    depth = e_norm1.shape[0] + o_norm1.shape[0]
    x2d = x.reshape(batch * seq, D_MODEL)
    for i in range(depth):
        j = i // 2
        if i % 2 == 0:
            x2d = _even_layer(x2d, batch, seq, e_norm1[j], e_w_in[j], e_fox_qn[j], e_fox_kn[j], e_fox_fbias[j],
                              e_conv_w[j], e_conv_b[j], e_dt_bias[j], e_a_log[j], e_d_skip[j], e_ssd_norm[j],
                              e_w_out[j], e_norm2[j], e_ffn_w13[j], e_ffn_w2[j])
        else:
            x2d = _odd_layer(x2d, batch, seq, o_norm1[j], o_w_qkv[j], o_qn[j], o_kn[j], o_w_out[j], o_norm2[j],
                             o_router[j], o_moe_w13[j], o_moe_w2[j])
    return x2d.reshape(batch, seq, D_MODEL)
```

```python
import functools
import math

import jax
import jax.numpy as jnp
import numpy as np
from jax import lax
from jax.experimental import pallas as pl
from jax.experimental.pallas import tpu as pltpu

F32 = jnp.float32
BF16 = jnp.bfloat16

D_MODEL = 1024
HEAD_DIM = 64
RMS_EPS = 1e-6
ROPE_THETA = 10000.0
FOX_HEADS = 8
FOX_WIDTH = FOX_HEADS * HEAD_DIM
SSD_HEADS = 8
SSD_INNER = 512
SSD_GROUPS = 2
SSD_STATE = 128
SSD_CONV = 4
SSD_CHUNK = 128
SSD_CONV_DIM = SSD_INNER + 2 * SSD_GROUPS * SSD_STATE
DIL_HEADS = 16
DIL_WIDTH = DIL_HEADS * HEAD_DIM
DIL_PATTERNS = ((128, 1), (512, 4), (2048, 16))
DIL_BLOCK = 128
FFN_DIM = 2816
N_EXPERTS = 8
EXPERT_DIM = 3584

LANES = 128
TILE_SUBLANES = D_MODEL // LANES
NEG_BIG = -1e30
LOG2E = math.log2(math.e)
DEN_ROWS = 16
BIAS_PARTS = 3
BIAS_LANES = 2 * BIAS_PARTS
VMEM_LIMIT = 56 * 1024 * 1024

ROW_TILE = 512
ATTN_TILE = 512
FOX_PAIRS_PER_STEP = 4
FFN_CHUNK = 256
MOE_CHUNK = 256
MOE_UNROLL = 14
MOE_TILE = 512


def _cparams(sem):
    return pltpu.CompilerParams(dimension_semantics=sem, vmem_limit_bytes=VMEM_LIMIT)


def _softplus_parts(x):
    return jnp.log(1.0 + jnp.exp(-jnp.abs(x)))


def _split_bf16(a, parts):
    out = []
    r = a
    for _ in range(parts):
        p = r.astype(BF16)
        out.append(p)
        r = r - p.astype(F32)
    return out


def _dot(a, b):
    return jnp.dot(a, b, preferred_element_type=F32)


def _dot_nt(a, b):
    return lax.dot_general(a, b, (((1,), (1,)), ((), ())), preferred_element_type=F32)


def _dot_exact_rhs(a_f32, b_bf16, parts):
    acc = None
    for p in _split_bf16(a_f32, parts):
        t = _dot(p, b_bf16)
        acc = t if acc is None else acc + t
    return acc


def _dot_exact_lhs(a_bf16, b_f32, parts):
    acc = None
    for p in _split_bf16(b_f32, parts):
        t = _dot(a_bf16, p)
        acc = t if acc is None else acc + t
    return acc


def _rms(x, g):
    return x * lax.rsqrt(jnp.mean(x * x, axis=-1, keepdims=True) + RMS_EPS) * g


def _head_norm(x, hm, gain):
    ms = _dot((x * x).astype(BF16), hm)
    return x * lax.rsqrt(ms + RMS_EPS) * gain


def _head_norm_wide(x, expand, gain):
    sums = _dot_nt((x * x).astype(BF16), expand) * (1.0 / HEAD_DIM)
    ms = _dot_exact_rhs(sums, expand, 2)
    return x * lax.rsqrt(ms + RMS_EPS) * gain


def _silu(x):
    return x * (1.0 / (1.0 + jnp.exp(-x)))


def _even_in_kernel(tiles_per_seq, x_ref, g_ref, wqk_ref, wvt_ref, wzx_ref, wfd_ref, qg_ref, kg_ref,
                    fdb_ref, hm_ref, tril_ref, pq_ref, pk_ref, qone_ref, kone_ref,
                    q_ref, qb_ref, k_ref, kb_ref, vt_ref, z_ref, xbc_ref, fd_ref, carry_ref):
    i = pl.program_id(0)

    @pl.when(i % tiles_per_seq == 0)
    def _():
        carry_ref[...] = jnp.zeros_like(carry_ref)

    h = _rms(x_ref[...], g_ref[...]).astype(BF16)
    qk = _dot(h, wqk_ref[...])
    hm = hm_ref[...]
    w = FOX_WIDTH
    q_ref[...] = (_head_norm(qk[:, :w], hm, qg_ref[...]) * (HEAD_DIM ** -0.5 * LOG2E)).astype(BF16)
    k_ref[...] = _head_norm(qk[:, w:], hm, kg_ref[...]).astype(BF16)
    vt_ref[...] = _dot_nt(wvt_ref[...], h).astype(BF16)
    zx = _dot(h, wzx_ref[...])
    z_ref[...] = zx[:, :SSD_INNER].astype(BF16)
    xbc_ref[...] = zx[:, SSD_INNER:]
    fd = _dot(h, wfd_ref[...]) + fdb_ref[...]
    t = _softplus_parts(fd)
    log_f = jnp.minimum(fd, 0.0) - t
    dt = jnp.maximum(fd, 0.0) + t
    c = _dot_exact_lhs(tril_ref[...], log_f, 2) + carry_ref[...]
    carry_ref[...] = c[-1:, :]
    lane = lax.broadcasted_iota(jnp.int32, fd.shape, 1)
    fd_ref[...] = jnp.where(lane < FOX_HEADS, c, dt)
    qb = qone_ref[...]
    kb = kone_ref[...]
    for j, part in enumerate(_split_bf16(c * LOG2E, 3)):
        qb = qb + _dot(part, pq_ref[j])
        kb = kb + _dot(part, pk_ref[j])
    qb_ref[...] = qb.astype(BF16)
    kb_ref[...] = kb.astype(BF16)


def _even_in_proj(x2d, seq, g, wqk, wvt, wzx, wfd, qg, kg, fdb, hm, tril, pq, pk, qone, kone):
    t_rows = x2d.shape[0]
    tm = min(ROW_TILE, seq)
    n = t_rows // tm
    row = lambda w: pl.BlockSpec((tm, w), lambda i: (i, 0))
    full = lambda a: pl.BlockSpec(a.shape, lambda i: (0,) * a.ndim)
    consts = (g, wqk, wvt, wzx, wfd, qg, kg, fdb, hm, tril, pq, pk, qone, kone)
    return pl.pallas_call(
        functools.partial(_even_in_kernel, seq // tm),
        grid=(n,),
        in_specs=[row(D_MODEL)] + [full(a) for a in consts],
        out_specs=[row(FOX_WIDTH)] * 4 + [pl.BlockSpec((FOX_WIDTH, tm), lambda i: (0, i)),
                                           row(SSD_INNER), row(SSD_CONV_DIM), row(LANES)],
        out_shape=[jax.ShapeDtypeStruct((t_rows, FOX_WIDTH), BF16)] * 4
        + [jax.ShapeDtypeStruct((FOX_WIDTH, t_rows), BF16),
           jax.ShapeDtypeStruct((t_rows, SSD_INNER), BF16),
           jax.ShapeDtypeStruct((t_rows, SSD_CONV_DIM), F32),
           jax.ShapeDtypeStruct((t_rows, LANES), F32)],
        scratch_shapes=[pltpu.VMEM((1, LANES), F32)],
        compiler_params=_cparams(("arbitrary",)),
        name="even_in_proj",
    )(x2d, *consts)


def _fox_kernel(qi_ref, ki_ref, q_ref, qb_ref, k_ref, kb_ref, vt_ref, o_ref, qs_ref, m_ref, l_ref, acc_ref):
    step_idx = pl.program_id(2)
    qi = qi_ref[step_idx]
    ki = ki_ref[step_idx]
    tq = q_ref.shape[0]
    tk = k_ref.shape[0]
    heads = range(2 * FOX_PAIRS_PER_STEP)
    lanes_of = lambda h: slice((h // 2) * LANES, (h // 2 + 1) * LANES)

    @pl.when(ki == 0)
    def _():
        m_ref[...] = jnp.full_like(m_ref, NEG_BIG)
        l_ref[...] = jnp.zeros_like(l_ref)
        acc_ref[...] = jnp.zeros_like(acc_ref)
        lane = lax.broadcasted_iota(jnp.int32, (tq, LANES), 1)
        zero = jnp.zeros((tq, LANES), q_ref.dtype)
        for h in heads:
            hh = h % 2
            qh = jnp.where((lane >= hh * HEAD_DIM) & (lane < (hh + 1) * HEAD_DIM), q_ref[:, lanes_of(h)], zero)
            bh = jnp.where((lane >= hh * BIAS_LANES) & (lane < (hh + 1) * BIAS_LANES), qb_ref[:, lanes_of(h)], zero)
            qs_ref[h] = jnp.concatenate([qh, bh], axis=1)

    def step(masked):
        if masked:
            keep = (lax.broadcasted_iota(jnp.int32, (tk, tq), 0)
                    <= lax.broadcasted_iota(jnp.int32, (tk, tq), 1))
        ka = [jnp.concatenate([k_ref[:, lanes_of(2 * p)], kb_ref[:, lanes_of(2 * p)]], axis=1)
              for p in range(FOX_PAIRS_PER_STEP)]
        scores = [_dot_nt(ka[h // 2], qs_ref[h]) for h in heads]
        if masked:
            scores = [jnp.where(keep, s, NEG_BIG) for s in scores]
        m_prev = [m_ref[h] for h in heads]
        m_new = [jnp.maximum(m_prev[h], jnp.max(scores[h], axis=0, keepdims=True)) for h in heads]
        probs = [jnp.exp2(scores[h] - m_new[h]) for h in heads]
        alpha = [jnp.exp2(m_prev[h] - m_new[h]) for h in heads]
        pv = [_dot(vt_ref[lanes_of(h), :], probs[h].astype(BF16)) for h in heads]
        for h in heads:
            l_ref[h] = alpha[h] * l_ref[h] + jnp.sum(probs[h], axis=0, keepdims=True)
            acc_ref[h] = alpha[h] * acc_ref[h] + pv[h]
            m_ref[h] = m_new[h]

    @pl.when(ki < qi)
    def _():
        step(False)

    @pl.when(ki == qi)
    def _():
        step(True)
        for p in range(FOX_PAIRS_PER_STEP):
            o0 = acc_ref[2 * p] / l_ref[2 * p]
            o1 = acc_ref[2 * p + 1] / l_ref[2 * p + 1]
            o_t = jnp.concatenate([o0[:HEAD_DIM], o1[HEAD_DIM:]], axis=0)
            o_ref[:, lanes_of(2 * p)] = o_t.T.astype(o_ref.dtype)


def _fox_attention(q, qb, k, kb, vt, batch, seq):
    t = min(ATTN_TILE, seq)
    n = seq // t
    w = FOX_PAIRS_PER_STEP * LANES
    groups = FOX_WIDTH // w
    tri = [(i, j) for i in range(n) for j in range(i + 1)]
    qi_tab = jnp.asarray([i for i, _ in tri], jnp.int32)
    ki_tab = jnp.asarray([j for _, j in tri], jnp.int32)
    qspec = pl.BlockSpec((t, w), lambda bi, g, s, qi, ki: (bi * n + qi[s], g))
    kspec = pl.BlockSpec((t, w), lambda bi, g, s, qi, ki: (bi * n + ki[s], g))
    vspec = pl.BlockSpec((w, t), lambda bi, g, s, qi, ki: (g, bi * n + ki[s]))
    heads = 2 * FOX_PAIRS_PER_STEP
    grid_spec = pltpu.PrefetchScalarGridSpec(
        num_scalar_prefetch=2,
        grid=(batch, groups, len(tri)),
        in_specs=[qspec, qspec, kspec, kspec, vspec],
        out_specs=qspec,
        scratch_shapes=[pltpu.VMEM((heads, t, 2 * LANES), BF16), pltpu.VMEM((heads, 1, t), F32),
                        pltpu.VMEM((heads, 1, t), F32), pltpu.VMEM((heads, LANES, t), F32)],
    )
    return pl.pallas_call(
        _fox_kernel,
        grid_spec=grid_spec,
        out_shape=jax.ShapeDtypeStruct((batch * seq, FOX_WIDTH), BF16),
        compiler_params=_cparams(("arbitrary",) * 3),
        name="fox_attention",
    )(qi_tab, ki_tab, q, qb, k, kb, vt)


def _ssd_kernel(xbc_ref, prev_ref, fd_ref, dtt_ref, z_ref, cw_ref, cb_ref, alog_ref, alogc_ref,
                dsk_ref, nrm_ref, tri_ref, y_ref, state_ref):
    c_idx = pl.program_id(1)
    L = SSD_CHUNK
    P = HEAD_DIM
    N = SSD_STATE

    @pl.when(c_idx == 0)
    def _():
        state_ref[...] = jnp.zeros_like(state_ref)

    cur = xbc_ref[...]
    prev = jnp.where(c_idx > 0, prev_ref[...], 0.0)
    ext = jnp.concatenate([prev, cur], axis=0)
    cw = cw_ref[...]
    conv = cb_ref[...] + cw[SSD_CONV - 1:SSD_CONV, :] * cur
    for kk in range(SSD_CONV - 1):
        sh = SSD_CONV - 1 - kk
        conv = conv + cw[kk:kk + 1, :] * ext[8 - sh:8 - sh + L, :]
    xc = _silu(conv)
    xs = xc[:, :SSD_INNER]
    bm = xc[:, SSD_INNER:SSD_INNER + SSD_GROUPS * N]
    cm = xc[:, SSD_INNER + SSD_GROUPS * N:]

    dt = fd_ref[:, FOX_HEADS:FOX_HEADS + SSD_HEADS]
    a_row = -jnp.exp(alog_ref[...])
    a_col = -jnp.exp(alogc_ref[...])
    tri = tri_ref[...]
    cum = _dot_exact_lhs(tri, dt * a_row, 2)
    cum_t = lax.dot_general(_split_bf16(dtt_ref[...] * a_col, 2)[0], tri, (((1,), (1,)), ((), ())),
                            preferred_element_type=F32)
    cum_t = cum_t + lax.dot_general(_split_bf16(dtt_ref[...] * a_col, 2)[1], tri,
                                    (((1,), (1,)), ((), ())), preferred_element_type=F32)
    row_i = lax.broadcasted_iota(jnp.int32, (L, L), 0)
    col_i = lax.broadcasted_iota(jnp.int32, (L, L), 1)
    causal = row_i >= col_i

    ys = []
    for g in range(SSD_GROUPS):
        bm_g = bm[:, g * N:(g + 1) * N]
        cm_g = cm[:, g * N:(g + 1) * N]
        cb_g = _dot_nt(cm_g.astype(BF16), bm_g.astype(BF16))
        bm_t = bm_g.T
        for hh in range(SSD_HEADS // SSD_GROUPS):
            h = g * (SSD_HEADS // SSD_GROUPS) + hh
            cum_c = cum[:, h:h + 1]
            cum_r = cum_t[h:h + 1, :]
            seg = jnp.where(causal, jnp.exp(jnp.minimum(cum_c - cum_r, 0.0)), 0.0)
            x_h = xs[:, h * P:(h + 1) * P]
            xdt = (x_h * dt[:, h:h + 1]).astype(BF16)
            y_h = _dot((cb_g * seg).astype(BF16), xdt)
            h_prev = state_ref[h]
            y_h = y_h + _dot((cm_g * jnp.exp(cum_c)).astype(BF16), h_prev.astype(BF16))
            last = cum_t[h:h + 1, L - 1:L]
            dte = jnp.exp(last - cum_r)
            st = _dot((bm_t * dte).astype(BF16), xdt)
            state_ref[h] = jnp.exp(last) * h_prev + st
            ys.append(y_h)
    y = jnp.concatenate(ys, axis=1) + xs * dsk_ref[...]
    y = y * _silu(z_ref[...].astype(F32))
    gw = SSD_INNER // SSD_GROUPS
    outs = []
    for g in range(SSD_GROUPS):
        yg = y[:, g * gw:(g + 1) * gw]
        outs.append(yg * lax.rsqrt(jnp.mean(yg * yg, axis=-1, keepdims=True) + RMS_EPS))
    y_ref[...] = (jnp.concatenate(outs, axis=1) * nrm_ref[...]).astype(y_ref.dtype)


def _ssd(xbc, fd, dtt, z, conv_w, conv_b, a_log, d_skip_c, ssd_norm, tri, batch, seq):
    L = SSD_CHUNK
    nc = seq // L
    t_rows = batch * seq
    row = lambda w: pl.BlockSpec((L, w), lambda b, c: (b * nc + c, 0))
    full = lambda a: pl.BlockSpec(a.shape, lambda b, c: (0,) * a.ndim)
    prev = pl.BlockSpec((8, SSD_CONV_DIM), lambda b, c: (jnp.maximum((b * nc + c) * (L // 8) - 1, 0), 0))
    return pl.pallas_call(
        _ssd_kernel,
        grid=(batch, nc),
        in_specs=[row(SSD_CONV_DIM), prev, row(LANES),
                  pl.BlockSpec((None, SSD_HEADS, L), lambda b, c: (b, 0, c)),
                  row(SSD_INNER), full(conv_w), full(conv_b), full(a_log[None, :]),
                  full(a_log[:, None]), full(d_skip_c), full(ssd_norm), full(tri)],
        out_specs=row(SSD_INNER),
        out_shape=jax.ShapeDtypeStruct((t_rows, SSD_INNER), BF16),
        scratch_shapes=[pltpu.VMEM((SSD_HEADS, SSD_STATE, HEAD_DIM), F32)],
        compiler_params=_cparams(("arbitrary", "arbitrary")),
        name="ssd_scan",
    )(xbc, xbc, fd, dtt, z, conv_w, conv_b, a_log[None, :], a_log[:, None], d_skip_c, ssd_norm, tri)


def _swiglu_hidden_chunk(h, w13_ref, act_ref, c, tf):
    f = w13_ref.shape[1] // 2
    off = c * tf if isinstance(c, int) else pl.multiple_of(c * tf, tf)
    gate = _dot(h, w13_ref[:, pl.ds(off, tf)])
    up = _dot(h, w13_ref[:, pl.ds(f + off, tf)])
    act_ref[:, pl.ds(off, tf)] = (_silu(gate) * up).astype(BF16)


def _even_out_kernel(o_ref, y_ref, x_ref, wo_ref, g_ref, w13_ref, w2_ref, out_ref, act_ref):
    w = FOX_WIDTH
    x1 = x_ref[...] + _dot(o_ref[...], wo_ref[:w, :]) + _dot(y_ref[...], wo_ref[w:, :])
    h = _rms(x1, g_ref[...]).astype(BF16)

    for c in range(w2_ref.shape[0] // FFN_CHUNK):
        _swiglu_hidden_chunk(h, w13_ref, act_ref, c, FFN_CHUNK)
    out_ref[...] = x1 + _dot(act_ref[...], w2_ref[...])


def _even_out_ffn(o_fox, y_ssd, x2d, wo, g, w13, w2):
    t_rows = x2d.shape[0]
    tm = min(ROW_TILE, t_rows)
    row = lambda w: pl.BlockSpec((tm, w), lambda i: (i, 0))
    full = lambda a: pl.BlockSpec(a.shape, lambda i: (0,) * a.ndim, pipeline_mode=pl.Buffered(1))
    return pl.pallas_call(
        _even_out_kernel,
        grid=(t_rows // tm,),
        in_specs=[row(FOX_WIDTH), row(SSD_INNER), row(D_MODEL), full(wo), full(g), full(w13), full(w2)],
        out_specs=row(D_MODEL),
        out_shape=jax.ShapeDtypeStruct((t_rows, D_MODEL), F32),
        scratch_shapes=[pltpu.VMEM((tm, w2.shape[0]), BF16)],
        compiler_params=_cparams(("arbitrary",)),
        name="even_out_ffn",
    )(o_fox, y_ssd, x2d, wo, g, w13, w2)


def _rope(x, cos, sin_signed):
    n = x.shape[1]
    lane = lax.broadcasted_iota(jnp.int32, x.shape, 1)
    first = (lane % HEAD_DIM) < (HEAD_DIM // 2)
    rot = jnp.where(first, pltpu.roll(x, n - HEAD_DIM // 2, 1), pltpu.roll(x, HEAD_DIM // 2, 1))
    reps = n // LANES
    cos_t = jnp.concatenate([cos] * reps, axis=1)
    sin_t = jnp.concatenate([sin_signed] * reps, axis=1)
    return x * cos_t + rot * sin_t


def _emit_dilated_views(val, refs, slab_ref):
    tm, width = val.shape
    chunks = width // LANES
    for c in range(chunks):
        slab_ref[c * tm:(c + 1) * tm, :] = val[:, c * LANES:(c + 1) * LANES]
    for (_, d), ref in zip(DIL_PATTERNS, refs):
        if d == 1:
            ref[...] = val.astype(ref.dtype)
            continue
        n = tm // d
        for r in range(d):
            for c in range(chunks):
                piece = slab_ref[pl.ds(c * tm + r, n, stride=d), :]
                ref[:, r * width + c * LANES:r * width + (c + 1) * LANES] = piece.astype(ref.dtype)


def _read_dilated_view(ref, d, tm, slab_ref):
    if d == 1:
        return ref[...].astype(F32)
    width = ref.shape[1] // d
    chunks = width // LANES
    n = tm // d
    for r in range(d):
        for c in range(chunks):
            piece = ref[:, r * width + c * LANES:r * width + (c + 1) * LANES].astype(F32)
            slab_ref[pl.ds(c * tm + r, n, stride=d), :] = piece
    return jnp.concatenate([slab_ref[c * tm:(c + 1) * tm, :] for c in range(chunks)], axis=1)


def _odd_qkv_kernel(x_ref, g_ref, w_ref, qg_ref, kg_ref, hm_ref, cos_ref, sin_ref, *rest):
    n_pat = len(DIL_PATTERNS)
    q_refs, k_refs, v_refs = rest[:n_pat], rest[n_pat:2 * n_pat], rest[2 * n_pat:3 * n_pat]
    slabs = rest[3 * n_pat:]
    h = _rms(x_ref[...], g_ref[...]).astype(BF16)
    hm = hm_ref[...]
    w = DIL_WIDTH
    cos = cos_ref[...]
    sin = sin_ref[...]
    q = _head_norm_wide(_dot(h, w_ref[:, :w]), hm, qg_ref[...])
    _emit_dilated_views(_rope(q, cos, sin) * (HEAD_DIM ** -0.5 * LOG2E), q_refs, slabs[0])
    k = _head_norm_wide(_dot(h, w_ref[:, w:2 * w]), hm, kg_ref[...])
    _emit_dilated_views(_rope(k, cos, sin), k_refs, slabs[1])
    _emit_dilated_views(_dot(h, w_ref[:, 2 * w:]), v_refs, slabs[2])


def _odd_qkv(x2d, seq, g, w, qg, kg, hm, cos, sin):
    t_rows = x2d.shape[0]
    tm = min(ROW_TILE, seq)
    per_seq = seq // tm
    row = lambda rows, wd: pl.BlockSpec((rows, wd), lambda i: (i, 0))
    full = lambda a: pl.BlockSpec(a.shape, lambda i: (0,) * a.ndim)
    tab = pl.BlockSpec((tm, LANES), lambda i: (i % per_seq, 0))
    views = [(tm // d, t_rows // d, d * DIL_WIDTH) for (_, d) in DIL_PATTERNS]
    outs = pl.pallas_call(
        _odd_qkv_kernel,
        grid=(t_rows // tm,),
        in_specs=[row(tm, D_MODEL), full(g), full(w), full(qg), full(kg), full(hm), tab, tab],
        out_specs=[row(r, wd) for (r, _, wd) in views] * 3,
        out_shape=[jax.ShapeDtypeStruct((n, wd), BF16) for (_, n, wd) in views] * 3,
        scratch_shapes=[pltpu.VMEM((tm * (DIL_WIDTH // LANES), LANES), F32)] * 3,
        compiler_params=_cparams(("arbitrary",)),
        name="odd_qkv",
    )(x2d, g, w, qg, kg, hm, cos, sin)
    n_pat = len(DIL_PATTERNS)
    return outs[:n_pat], outs[n_pat:2 * n_pat], outs[2 * n_pat:]


def _dilated_kernel(q_ref, kp_ref, kc_ref, vp_ref, vc_ref, band_ref, o_ref, lse_ref):
    j = pl.program_id(2)
    blk = DIL_BLOCK
    key_row = lax.broadcasted_iota(jnp.int32, (2 * blk, blk), 0)
    bias = jnp.where((j == 0) & (key_row < blk), NEG_BIG, band_ref[...])
    lane = lax.broadcasted_iota(jnp.int32, (blk, LANES), 1)
    ones_rows = jnp.ones((DEN_ROWS, 2 * blk), BF16)
    pairs = range(DIL_HEADS // 2)
    heads = range(DIL_HEADS)
    lanes_of = lambda pr: slice(pr * LANES, (pr + 1) * LANES)
    kk = [jnp.concatenate([kp_ref[:, lanes_of(pr)], kc_ref[:, lanes_of(pr)]], axis=0) for pr in pairs]
    vt = [jnp.concatenate([jnp.concatenate([vp_ref[:, lanes_of(pr)], vc_ref[:, lanes_of(pr)]], axis=0)
                           .astype(F32).T.astype(BF16), ones_rows], axis=0) for pr in pairs]
    zero = jnp.zeros((blk, LANES), q_ref.dtype)
    qm = [jnp.where((lane >= (h % 2) * HEAD_DIM) & (lane < (h % 2 + 1) * HEAD_DIM), q_ref[:, lanes_of(h // 2)], zero)
          for h in heads]
    s = [_dot_nt(kk[h // 2], qm[h]) + bias for h in heads]
    m = [jnp.max(s[h], axis=0, keepdims=True) for h in heads]
    e = [jnp.exp2(s[h] - m[h]).astype(BF16) for h in heads]
    ov = [_dot(vt[h // 2], e[h]) for h in heads]
    den = [ov[h][LANES:LANES + 1, :] for h in heads]
    for pr in pairs:
        o_t = jnp.concatenate([(ov[2 * pr][:LANES] / den[2 * pr])[:HEAD_DIM],
                               (ov[2 * pr + 1][:LANES] / den[2 * pr + 1])[HEAD_DIM:]], axis=0)
        o_ref[:, lanes_of(pr)] = o_t.T.astype(o_ref.dtype)
    lse_t = jnp.concatenate([m[h] + jnp.log2(den[h]) for h in heads]
                            + [jnp.zeros((LANES - DIL_HEADS, blk), F32)], axis=0)
    lse_ref[...] = lse_t.T


def _dilated_branch(q, k, v, band, dilation, batch):
    d = dilation
    w = DIL_WIDTH
    nb = q.shape[0] // batch // DIL_BLOCK
    cur = pl.BlockSpec((DIL_BLOCK, w), lambda bi, r, j: (bi * nb + j, r))
    prev = pl.BlockSpec((DIL_BLOCK, w), lambda bi, r, j: (bi * nb + jnp.maximum(j - 1, 0), r))
    lse_spec = pl.BlockSpec((DIL_BLOCK, LANES), lambda bi, r, j: (bi * nb + j, r))
    return pl.pallas_call(
        _dilated_kernel,
        grid=(batch, d, nb),
        in_specs=[cur, prev, cur, prev, cur, pl.BlockSpec(band.shape, lambda bi, r, j: (0, 0))],
        out_specs=[cur, lse_spec],
        out_shape=[jax.ShapeDtypeStruct(q.shape, BF16),
                   jax.ShapeDtypeStruct((q.shape[0], d * LANES), F32)],
        compiler_params=_cparams(("arbitrary",) * 3),
        name=f"dilated_attn_d{d}",
    )(q, k, k, v, v, band)


def _odd_out_kernel(o1_ref, o2_ref, o3_ref, l1_ref, l2_ref, l3_ref, x_ref, wo_ref, g_ref,
                    rw_ref, ex_ref, tril_ref, x1_ref, h_ref, info_ref, cnt_ref, carry_ref, slab_ref):
    i = pl.program_id(0)
    tm = x_ref.shape[0]

    @pl.when(i == 0)
    def _():
        carry_ref[...] = jnp.zeros_like(carry_ref)

    dils = [d for (_, d) in DIL_PATTERNS]
    l1, l2, l3 = [_read_dilated_view(r, d, tm, slab_ref) for r, d in zip((l1_ref, l2_ref, l3_ref), dils)]
    mx = jnp.maximum(jnp.maximum(l1, l2), l3)
    a1, a2, a3 = jnp.exp2(l1 - mx), jnp.exp2(l2 - mx), jnp.exp2(l3 - mx)
    inv = 1.0 / (a1 + a2 + a3)
    ex = ex_ref[...]
    o = None
    for a, o_ref, d in zip((a1, a2, a3), (o1_ref, o2_ref, o3_ref), dils):
        term = _dot_exact_rhs(a * inv, ex, 1) * _read_dilated_view(o_ref, d, tm, slab_ref)
        o = term if o is None else o + term
    x1 = x_ref[...] + _dot(o.astype(BF16), wo_ref[...])
    x1_ref[...] = x1
    h = _rms(x1, g_ref[...])
    _wide_to_rows(h_ref, h)

    h_hi, h_lo = _split_bf16(h, 2)
    w_hi, w_lo = _split_bf16(rw_ref[...], 2)
    logits = _dot(h_hi, w_hi) + (_dot(h_lo, w_hi) + _dot(h_hi, w_lo))
    lane = lax.broadcasted_iota(jnp.int32, logits.shape, 1)
    logits = jnp.where(lane < N_EXPERTS, logits, -jnp.inf)
    m1 = jnp.max(logits, axis=-1, keepdims=True)
    i1 = jnp.min(jnp.where(logits == m1, lane, LANES), axis=-1, keepdims=True)
    rest = jnp.where(lane == i1, -jnp.inf, logits)
    m2 = jnp.max(rest, axis=-1, keepdims=True)
    i2 = jnp.min(jnp.where(rest == m2, lane, LANES), axis=-1, keepdims=True)
    e2 = jnp.exp(m2 - m1)
    g1 = 1.0 / (1.0 + e2)
    g2 = e2 * g1
    hot1 = lane == i1
    hot2 = lane == i2
    onehot = jnp.where(hot1 | hot2, 1.0, 0.0).astype(BF16)
    before = _dot(tril_ref[...], onehot) + carry_ref[...]
    r1 = jnp.sum(jnp.where(hot1, before, 0.0), axis=-1, keepdims=True)
    r2 = jnp.sum(jnp.where(hot2, before, 0.0), axis=-1, keepdims=True)
    total = before[-1:, :] + onehot[-1:, :].astype(F32)
    carry_ref[...] = total
    cnt_ref[...] = jnp.broadcast_to(total, cnt_ref.shape)
    info = jnp.where(lane == 0, i1.astype(F32), 0.0)
    info = jnp.where(lane == 1, i2.astype(F32), info)
    info = jnp.where(lane == 2, g1, info)
    info = jnp.where(lane == 3, g2, info)
    info = jnp.where(lane == 4, r1, info)
    info = jnp.where(lane == 5, r2, info)
    info_ref[...] = info


def _odd_out_router(o1, o2, o3, l1, l2, l3, x2d, wo, g, rw, ex, tril_strict):
    t_rows = x2d.shape[0]
    tm = min(ROW_TILE, t_rows)
    n = t_rows // tm
    row = lambda w: pl.BlockSpec((tm, w), lambda i: (i, 0))
    full = lambda a: pl.BlockSpec(a.shape, lambda i: (0,) * a.ndim)
    view = lambda w: [pl.BlockSpec((tm // d, d * w), lambda i: (i, 0)) for (_, d) in DIL_PATTERNS]
    return pl.pallas_call(
        _odd_out_kernel,
        grid=(n,),
        in_specs=view(DIL_WIDTH) + view(LANES) + [row(D_MODEL), full(wo), full(g), full(rw),
                                                  full(ex), full(tril_strict)],
        out_specs=[row(D_MODEL), pl.BlockSpec((tm * TILE_SUBLANES, LANES), lambda i: (i, 0)), row(LANES),
                   pl.BlockSpec((8, LANES), lambda i: (0, 0))],
        out_shape=[jax.ShapeDtypeStruct((t_rows, D_MODEL), F32),
                   jax.ShapeDtypeStruct((t_rows * TILE_SUBLANES, LANES), F32),
                   jax.ShapeDtypeStruct((t_rows, LANES), F32), jax.ShapeDtypeStruct((8, LANES), F32)],
        scratch_shapes=[pltpu.VMEM((1, LANES), F32), pltpu.VMEM((tm * (DIL_WIDTH // LANES), LANES), F32)],
        compiler_params=_cparams(("arbitrary",)),
        name="odd_out_router",
    )(o1, o2, o3, l1, l2, l3, x2d, wo, g, rw, ex, tril_strict)


def _rows_to_wide(ref):
    n = ref.shape[0] // TILE_SUBLANES
    return jnp.concatenate([ref[pl.ds(c, n, stride=TILE_SUBLANES), :] for c in range(TILE_SUBLANES)], axis=1)


def _wide_to_rows(ref, val):
    n = ref.shape[0] // TILE_SUBLANES
    for c in range(TILE_SUBLANES):
        ref[pl.ds(c, n, stride=TILE_SUBLANES), :] = val[:, c * LANES:(c + 1) * LANES]


def _moe_kernel(te_ref, nu_ref, src_cur_ref, src_nxt_ref, dst_prv_ref, dst_cur_ref, h_hbm, w13_ref, w2_ref,
                y_hbm, xa, xb, ya, yb, act_ref, gsem, ssem):
    i = pl.program_id(0)
    n_used = nu_ref[0]
    ts = TILE_SUBLANES
    gm = xa.shape[0] // ts
    n_chunks = w2_ref.shape[0] // MOE_CHUNK
    per_chunk = gm // n_chunks
    head_rows = gm - n_chunks * per_chunk

    def tile_of(ref, row):
        return ref.at[pl.ds(pl.multiple_of(row * ts, ts), ts)]

    def gather_row(src_ref, r, xdst, sem):
        pltpu.make_async_copy(tile_of(h_hbm, src_ref[0, 0, r]), tile_of(xdst, r), sem).start()

    def scatter_row(dst_ref, r, ysrc, sem):
        pltpu.make_async_copy(tile_of(ysrc, r), tile_of(y_hbm, dst_ref[0, 0, r]), sem).start()

    def wait_rows(buf, sem):
        pltpu.make_async_copy(h_hbm.at[pl.ds(0, gm * ts)], buf, sem).wait()

    @pl.when(i == 0)
    def _():
        yb[...] = jnp.zeros_like(yb)

        def body(r, carry):
            gather_row(src_cur_ref, r, xa, gsem.at[0])
            return carry
        lax.fori_loop(0, gm, body, 0)

    def tile(x_cur, x_nxt, y_cur, y_prv, p):
        wait_rows(x_cur, gsem.at[p])

        @pl.when(i >= 1)
        def _():
            wait_rows(y_cur, ssem.at[p])

        def issue(r):
            gather_row(src_nxt_ref, r, x_nxt, gsem.at[1 - p])
            scatter_row(dst_prv_ref, r, y_prv, ssem.at[1 - p])

        for r in range(head_rows):
            issue(n_chunks * per_chunk + r)
        x = _rows_to_wide(x_cur).astype(BF16)

        def body(c, carry):
            for u in range(MOE_UNROLL):
                _swiglu_hidden_chunk(x, w13_ref, act_ref, c * MOE_UNROLL + u, MOE_CHUNK)
            for j in range(per_chunk * MOE_UNROLL):
                issue(c * (per_chunk * MOE_UNROLL) + j)
            return carry

        lax.fori_loop(0, n_chunks // MOE_UNROLL, body, 0)
        _wide_to_rows(y_cur, _dot(act_ref[...], w2_ref[...]))

        @pl.when(i == n_used - 1)
        def _():
            def body(r, carry):
                scatter_row(dst_cur_ref, r, y_cur, ssem.at[p])
                return carry
            lax.fori_loop(0, gm, body, 0)
            wait_rows(y_prv, ssem.at[1 - p])
            wait_rows(y_cur, ssem.at[p])
            wait_rows(x_nxt, gsem.at[1 - p])

    @pl.when((i < n_used) & (i % 2 == 0))
    def _():
        tile(xa, xb, ya, yb, 0)

    @pl.when((i < n_used) & (i % 2 == 1))
    def _():
        tile(xb, xa, yb, ya, 1)


def _moe_ffn(h_tiles, w13, w2, tile_expert, n_used, src, dst, out_rows):
    nt = tile_expert.shape[0]
    gm = src.shape[-1]
    tab = lambda f: pl.BlockSpec((1, 1, gm), lambda i, te, nu: (f(i), 0, 0), memory_space=pltpu.SMEM)
    grid_spec = pltpu.PrefetchScalarGridSpec(
        num_scalar_prefetch=2,
        grid=(nt,),
        in_specs=[
            tab(lambda i: i),
            tab(lambda i: jnp.minimum(i + 1, nt - 1)),
            tab(lambda i: i),
            tab(lambda i: i + 1),
            pl.BlockSpec(memory_space=pl.ANY),
            pl.BlockSpec((None,) + w13.shape[1:], lambda i, te, nu: (te[i], 0, 0), pipeline_mode=pl.Buffered(1)),
            pl.BlockSpec((None,) + w2.shape[1:], lambda i, te, nu: (te[i], 0, 0), pipeline_mode=pl.Buffered(1)),
        ],
        out_specs=pl.BlockSpec(memory_space=pl.ANY),
        scratch_shapes=[pltpu.VMEM((gm * TILE_SUBLANES, LANES), F32)] * 4
        + [pltpu.VMEM((gm, w2.shape[1]), BF16), pltpu.SemaphoreType.DMA((2,)), pltpu.SemaphoreType.DMA((2,))],
    )
    return pl.pallas_call(
        _moe_kernel,
        grid_spec=grid_spec,
        out_shape=jax.ShapeDtypeStruct((out_rows * TILE_SUBLANES, LANES), F32),
        compiler_params=_cparams(("arbitrary",)),
        name="moe_ffn",
    )(tile_expert, n_used, src, src, dst, dst, h_tiles, w13, w2)


def _combine_kernel(x_ref, y0_ref, y1_ref, info_ref, o_ref):
    o_ref[...] = (x_ref[...] + info_ref[:, 2:3] * _rows_to_wide(y0_ref)
                  + info_ref[:, 3:4] * _rows_to_wide(y1_ref))


def _combine(x1, y, info):
    t_rows = x1.shape[0]
    tm = min(ROW_TILE, t_rows)
    row = lambda w: pl.BlockSpec((tm, w), lambda i: (i, 0))
    ysp = lambda kk: pl.BlockSpec((None, tm * TILE_SUBLANES, LANES), lambda i: (kk, i, 0))
    return pl.pallas_call(
        _combine_kernel,
        grid=(t_rows // tm,),
        in_specs=[row(D_MODEL), ysp(0), ysp(1), row(LANES)],
        out_specs=row(D_MODEL),
        out_shape=jax.ShapeDtypeStruct((t_rows, D_MODEL), F32),
        compiler_params=_cparams(("arbitrary",)),
        name="moe_combine",
    )(x1, y, y, info)


def _head_mean_matrix(width):
    idx = np.arange(width) // HEAD_DIM
    return jnp.asarray((idx[:, None] == idx[None, :]).astype(np.float32) / HEAD_DIM, dtype=BF16)


def _tril(n, strict):
    r = np.arange(n)
    m = (r[None, :] < r[:, None]) if strict else (r[None, :] <= r[:, None])
    return jnp.asarray(m.astype(np.float32), dtype=BF16)


def _rope_tables(seq):
    half = HEAD_DIM // 2
    inv_freq = ROPE_THETA ** (-jnp.arange(half, dtype=F32) / half)
    ang = jnp.arange(seq, dtype=F32)[:, None] * inv_freq[None, :]
    cos = jnp.cos(ang)
    sin = jnp.sin(ang)
    cos_t = jnp.concatenate([cos, cos, cos, cos], axis=1)
    sin_t = jnp.concatenate([-sin, sin, -sin, sin], axis=1)
    return cos_t, sin_t


def _band_bias():
    qi = np.arange(DIL_BLOCK)[None, :]
    ki = np.arange(-DIL_BLOCK, DIL_BLOCK)[:, None]
    rel = qi - ki
    return jnp.asarray(np.where((rel >= 0) & (rel <= DIL_BLOCK), 0.0, NEG_BIG).astype(np.float32))


def _head_expand():
    m = np.zeros((LANES, DIL_WIDTH), np.float32)
    for h in range(DIL_HEADS):
        m[h, h * HEAD_DIM:(h + 1) * HEAD_DIM] = 1.0
    return jnp.asarray(m, dtype=BF16)


def _bias_placement():
    pq = np.zeros((BIAS_PARTS, LANES, FOX_WIDTH), np.float32)
    pk = np.zeros((BIAS_PARTS, LANES, FOX_WIDTH), np.float32)
    qone = np.zeros((1, FOX_WIDTH), np.float32)
    kone = np.zeros((1, FOX_WIDTH), np.float32)
    for head in range(FOX_HEADS):
        base = (head // 2) * LANES + (head % 2) * BIAS_LANES
        for j in range(BIAS_PARTS):
            pq[j, head, base + j] = 1.0
            pk[j, head, base + BIAS_PARTS + j] = -1.0
            qone[0, base + BIAS_PARTS + j] = 1.0
            kone[0, base + j] = 1.0
    return (jnp.asarray(pq, dtype=BF16), jnp.asarray(pk, dtype=BF16), jnp.asarray(qone), jnp.asarray(kone))


def _even_layer(x2d, batch, seq, norm1, w_in, fox_qn, fox_kn, fox_fbias, conv_w, conv_b, dt_bias, a_log,
                d_skip, ssd_norm, w_out, norm2, ffn_w13, ffn_w2):
    w = FOX_WIDTH
    o_z = 3 * w + FOX_HEADS
    o_x = o_z + SSD_INNER
    o_dt = o_x + SSD_CONV_DIM
    wqk = w_in[:, :2 * w].astype(BF16)
    wvt = w_in[:, 2 * w:3 * w].T.astype(BF16)
    wzx = w_in[:, o_z:o_dt].astype(BF16)
    pad = LANES - FOX_HEADS - SSD_HEADS
    wfd = jnp.concatenate([w_in[:, 3 * w:o_z], w_in[:, o_dt:], jnp.zeros((D_MODEL, pad), F32)], axis=1).astype(BF16)
    fdb = jnp.concatenate([fox_fbias, dt_bias, jnp.zeros((pad,), F32)])[None, :]
    tm = min(ROW_TILE, seq)
    q, qb, k, kb, vt, z, xbc, fd = _even_in_proj(
        x2d, seq, norm1[None, :], wqk, wvt, wzx, wfd, fox_qn.reshape(1, w), fox_kn.reshape(1, w), fdb,
        _head_mean_matrix(w), _tril(tm, False), *_bias_placement())

    o_fox = _fox_attention(q, qb, k, kb, vt, batch, seq)

    dtt = jnp.transpose(fd[:, FOX_HEADS:FOX_HEADS + SSD_HEADS].reshape(batch, seq, SSD_HEADS), (0, 2, 1))
    d_skip_c = jnp.repeat(d_skip, HEAD_DIM)[None, :]
    y = _ssd(xbc, fd, dtt, z, conv_w, conv_b[None, :], a_log, d_skip_c, ssd_norm[None, :],
             _tril(SSD_CHUNK, False), batch, seq)

    return _even_out_ffn(o_fox, y, x2d, w_out.astype(BF16), norm2[None, :],
                         ffn_w13.astype(BF16), ffn_w2.astype(BF16))


def _moe_tables(info, counts, t_rows, gm):
    e1 = info[:, 0].astype(jnp.int32)
    e2 = info[:, 1].astype(jnp.int32)
    r1 = info[:, 4].astype(jnp.int32)
    r2 = info[:, 5].astype(jnp.int32)
    cnt = counts[0, :N_EXPERTS].astype(jnp.int32)
    tiles = (cnt + gm - 1) // gm
    tile_end = jnp.cumsum(tiles)
    offs = (tile_end - tiles) * gm
    nt = (2 * t_rows) // gm + N_EXPERTS
    tile_idx = jnp.arange(nt, dtype=jnp.int32)
    tile_expert = jnp.minimum(jnp.sum((tile_idx[:, None] >= tile_end[None, :]).astype(jnp.int32), axis=1),
                              N_EXPERTS - 1)
    n_used = tile_end[-1:].astype(jnp.int32)
    slots = jnp.concatenate([offs[e1] + r1, offs[e2] + r2])
    codes = jnp.full((nt * gm,), -1, jnp.int32).at[slots].set(
        jnp.arange(2 * t_rows, dtype=jnp.int32), unique_indices=True)
    out_half = t_rows + N_EXPERTS * gm
    second = codes >= t_rows
    src = jnp.where(codes < 0, 0, jnp.where(second, codes - t_rows, codes))
    spare = t_rows + jnp.repeat(tile_expert, gm) * gm + jnp.tile(jnp.arange(gm, dtype=jnp.int32), nt)
    dst = jnp.where(codes < 0, spare, jnp.where(second, codes - t_rows + out_half, codes))
    dst = jnp.concatenate([t_rows + jnp.arange(gm, dtype=jnp.int32), dst])
    return tile_expert, n_used, src.reshape(nt, 1, gm), dst.reshape(nt + 1, 1, gm), out_half


def _odd_layer(x2d, batch, seq, norm1, w_qkv, qn, kn, w_out, norm2, router, moe_w13, moe_w2):
    t_rows = batch * seq
    cos, sin = _rope_tables(seq)
    qs, ks, vs = _odd_qkv(x2d, seq, norm1[None, :], w_qkv.astype(BF16), qn.reshape(1, DIL_WIDTH),
                          kn.reshape(1, DIL_WIDTH), _head_expand(), cos, sin)
    band = _band_bias()
    outs, lses = zip(*[_dilated_branch(q, k, v, band, d, batch)
                       for q, k, v, (_, d) in zip(qs, ks, vs, DIL_PATTERNS)])
    rw = jnp.concatenate([router, jnp.zeros((D_MODEL, LANES - N_EXPERTS), F32)], axis=1)
    tm = min(ROW_TILE, t_rows)
    x1, h, info, counts = _odd_out_router(*outs, *lses, x2d, w_out.astype(BF16), norm2[None, :], rw,
                                          _head_expand(), _tril(tm, True))
    gm = min(MOE_TILE, t_rows)
    tile_expert, n_used, src, dst, out_half = _moe_tables(info, counts, t_rows, gm)
    y = _moe_ffn(h, moe_w13.astype(BF16), moe_w2.astype(BF16), tile_expert, n_used, src, dst, 2 * out_half)
    return _combine(x1, y.reshape(2, out_half * TILE_SUBLANES, LANES), info)


def kernel(x, e_norm1, e_w_in, e_fox_qn, e_fox_kn, e_fox_fbias, e_conv_w, e_conv_b, e_dt_bias, e_a_log, e_d_skip, e_ssd_norm, e_w_out, e_norm2, e_ffn_w13, e_ffn_w2, o_norm1, o_w_qkv, o_qn, o_kn, o_w_out, o_norm2, o_router, o_moe_w13, o_moe_w2):
    batch, seq, _ = x.shape
    depth = e_norm1.shape[0] + o_norm1.shape[0]
    x2d = x.reshape(batch * seq, D_MODEL)
    for i in range(depth):
        j = i // 2
        if i % 2 == 0:
            x2d = _even_layer(x2d, batch, seq, e_norm1[j], e_w_in[j], e_fox_qn[j], e_fox_kn[j], e_fox_fbias[j],
                              e_conv_w[j], e_conv_b[j], e_dt_bias[j], e_a_log[j], e_d_skip[j], e_ssd_norm[j],
                              e_w_out[j], e_norm2[j], e_ffn_w13[j], e_ffn_w2[j])
        else:
            x2d = _odd_layer(x2d, batch, seq, o_norm1[j], o_w_qkv[j], o_qn[j], o_kn[j], o_w_out[j], o_norm2[j],
                             o_router[j], o_moe_w13[j], o_moe_w2[j])
    return x2d.reshape(batch, seq, D_MODEL)
```

```python
import functools
import math

import jax
import jax.numpy as jnp
import numpy as np
from jax import lax
from jax.experimental import pallas as pl
from jax.experimental.pallas import tpu as pltpu

F32 = jnp.float32
BF16 = jnp.bfloat16

D_MODEL = 1024
HEAD_DIM = 64
RMS_EPS = 1e-6
ROPE_THETA = 10000.0
FOX_HEADS = 8
FOX_WIDTH = FOX_HEADS * HEAD_DIM
SSD_HEADS = 8
SSD_INNER = 512
SSD_GROUPS = 2
SSD_STATE = 128
SSD_CONV = 4
SSD_CHUNK = 128
SSD_CONV_DIM = SSD_INNER + 2 * SSD_GROUPS * SSD_STATE
DIL_HEADS = 16
DIL_WIDTH = DIL_HEADS * HEAD_DIM
DIL_PATTERNS = ((128, 1), (512, 4), (2048, 16))
DIL_BLOCK = 128
DIL_SUB = 2
FFN_DIM = 2816
N_EXPERTS = 8
EXPERT_DIM = 3584

LANES = 128
TILE_SUBLANES = D_MODEL // LANES
NEG_BIG = -1e30
LOG2E = math.log2(math.e)
DEN_ROWS = 16
BIAS_PARTS = 3
BIAS_LANES = 2 * BIAS_PARTS
VMEM_LIMIT = 56 * 1024 * 1024

ROW_TILE = 512
ATTN_TILE = 512
FOX_PAIRS_PER_STEP = 4
FFN_CHUNK = 256
MOE_CHUNK = 256
MOE_UNROLL = 14
MOE_TILE = 512


def _cparams(sem):
    return pltpu.CompilerParams(dimension_semantics=sem, vmem_limit_bytes=VMEM_LIMIT)


def _softplus_parts(x):
    return jnp.log(1.0 + jnp.exp(-jnp.abs(x)))


def _split_bf16(a, parts):
    out = []
    r = a
    for _ in range(parts):
        p = r.astype(BF16)
        out.append(p)
        r = r - p.astype(F32)
    return out


def _dot(a, b):
    return jnp.dot(a, b, preferred_element_type=F32)


def _dot_nt(a, b):
    return lax.dot_general(a, b, (((1,), (1,)), ((), ())), preferred_element_type=F32)


def _dot_exact_rhs(a_f32, b_bf16, parts):
    acc = None
    for p in _split_bf16(a_f32, parts):
        t = _dot(p, b_bf16)
        acc = t if acc is None else acc + t
    return acc


def _dot_exact_lhs(a_bf16, b_f32, parts):
    acc = None
    for p in _split_bf16(b_f32, parts):
        t = _dot(a_bf16, p)
        acc = t if acc is None else acc + t
    return acc


def _rms(x, g):
    return x * lax.rsqrt(jnp.mean(x * x, axis=-1, keepdims=True) + RMS_EPS) * g


def _head_norm(x, hm, gain):
    ms = _dot((x * x).astype(BF16), hm)
    return x * lax.rsqrt(ms + RMS_EPS) * gain


def _head_norm_wide(x, expand, gain):
    sums = _dot_nt((x * x).astype(BF16), expand) * (1.0 / HEAD_DIM)
    ms = _dot_exact_rhs(sums, expand, 2)
    return x * lax.rsqrt(ms + RMS_EPS) * gain


def _silu(x):
    return x * (1.0 / (1.0 + jnp.exp(-x)))


def _even_in_kernel(tiles_per_seq, x_ref, g_ref, wqk_ref, wvt_ref, wzx_ref, wfd_ref, qg_ref, kg_ref,
                    fdb_ref, hm_ref, tril_ref, pq_ref, pk_ref, qone_ref, kone_ref,
                    q_ref, qb_ref, k_ref, kb_ref, vt_ref, z_ref, xbc_ref, fd_ref, carry_ref):
    i = pl.program_id(0)

    @pl.when(i % tiles_per_seq == 0)
    def _():
        carry_ref[...] = jnp.zeros_like(carry_ref)

    h = _rms(x_ref[...], g_ref[...]).astype(BF16)
    qk = _dot(h, wqk_ref[...])
    hm = hm_ref[...]
    w = FOX_WIDTH
    q_ref[...] = (_head_norm(qk[:, :w], hm, qg_ref[...]) * (HEAD_DIM ** -0.5 * LOG2E)).astype(BF16)
    k_ref[...] = _head_norm(qk[:, w:], hm, kg_ref[...]).astype(BF16)
    vt_ref[...] = _dot_nt(wvt_ref[...], h).astype(BF16)
    zx = _dot(h, wzx_ref[...])
    z_ref[...] = zx[:, :SSD_INNER].astype(BF16)
    xbc_ref[...] = zx[:, SSD_INNER:]
    fd = _dot(h, wfd_ref[...]) + fdb_ref[...]
    t = _softplus_parts(fd)
    log_f = jnp.minimum(fd, 0.0) - t
    dt = jnp.maximum(fd, 0.0) + t
    c = _dot_exact_lhs(tril_ref[...], log_f, 2) + carry_ref[...]
    carry_ref[...] = c[-1:, :]
    lane = lax.broadcasted_iota(jnp.int32, fd.shape, 1)
    fd_ref[...] = jnp.where(lane < FOX_HEADS, c, dt)
    qb = qone_ref[...]
    kb = kone_ref[...]
    for j, part in enumerate(_split_bf16(c * LOG2E, 3)):
        qb = qb + _dot(part, pq_ref[j])
        kb = kb + _dot(part, pk_ref[j])
    qb_ref[...] = qb.astype(BF16)
    kb_ref[...] = kb.astype(BF16)


def _even_in_proj(x2d, seq, g, wqk, wvt, wzx, wfd, qg, kg, fdb, hm, tril, pq, pk, qone, kone):
    t_rows = x2d.shape[0]
    tm = min(ROW_TILE, seq)
    n = t_rows // tm
    row = lambda w: pl.BlockSpec((tm, w), lambda i: (i, 0))
    full = lambda a: pl.BlockSpec(a.shape, lambda i: (0,) * a.ndim)
    consts = (g, wqk, wvt, wzx, wfd, qg, kg, fdb, hm, tril, pq, pk, qone, kone)
    return pl.pallas_call(
        functools.partial(_even_in_kernel, seq // tm),
        grid=(n,),
        in_specs=[row(D_MODEL)] + [full(a) for a in consts],
        out_specs=[row(FOX_WIDTH)] * 4 + [pl.BlockSpec((FOX_WIDTH, tm), lambda i: (0, i)),
                                           row(SSD_INNER), row(SSD_CONV_DIM), row(LANES)],
        out_shape=[jax.ShapeDtypeStruct((t_rows, FOX_WIDTH), BF16)] * 4
        + [jax.ShapeDtypeStruct((FOX_WIDTH, t_rows), BF16),
           jax.ShapeDtypeStruct((t_rows, SSD_INNER), BF16),
           jax.ShapeDtypeStruct((t_rows, SSD_CONV_DIM), F32),
           jax.ShapeDtypeStruct((t_rows, LANES), F32)],
        scratch_shapes=[pltpu.VMEM((1, LANES), F32)],
        compiler_params=_cparams(("arbitrary",)),
        name="even_in_proj",
    )(x2d, *consts)


def _fox_kernel(qi_ref, ki_ref, q_ref, qb_ref, k_ref, kb_ref, vt_ref, o_ref, qs_ref, m_ref, l_ref, acc_ref):
    step_idx = pl.program_id(2)
    qi = qi_ref[step_idx]
    ki = ki_ref[step_idx]
    tq = q_ref.shape[0]
    tk = k_ref.shape[0]
    heads = range(2 * FOX_PAIRS_PER_STEP)
    lanes_of = lambda h: slice((h // 2) * LANES, (h // 2 + 1) * LANES)

    @pl.when(ki == 0)
    def _():
        m_ref[...] = jnp.full_like(m_ref, NEG_BIG)
        l_ref[...] = jnp.zeros_like(l_ref)
        acc_ref[...] = jnp.zeros_like(acc_ref)
        lane = lax.broadcasted_iota(jnp.int32, (tq, LANES), 1)
        zero = jnp.zeros((tq, LANES), q_ref.dtype)
        for h in heads:
            hh = h % 2
            qh = jnp.where((lane >= hh * HEAD_DIM) & (lane < (hh + 1) * HEAD_DIM), q_ref[:, lanes_of(h)], zero)
            bh = jnp.where((lane >= hh * BIAS_LANES) & (lane < (hh + 1) * BIAS_LANES), qb_ref[:, lanes_of(h)], zero)
            qs_ref[h] = jnp.concatenate([qh, bh], axis=1)

    def step(masked):
        if masked:
            keep = (lax.broadcasted_iota(jnp.int32, (tk, tq), 0)
                    <= lax.broadcasted_iota(jnp.int32, (tk, tq), 1))
        ka = [jnp.concatenate([k_ref[:, lanes_of(2 * p)], kb_ref[:, lanes_of(2 * p)]], axis=1)
              for p in range(FOX_PAIRS_PER_STEP)]
        scores = [_dot_nt(ka[h // 2], qs_ref[h]) for h in heads]
        if masked:
            scores = [jnp.where(keep, s, NEG_BIG) for s in scores]
        m_prev = [m_ref[h] for h in heads]
        m_new = [jnp.maximum(m_prev[h], jnp.max(scores[h], axis=0, keepdims=True)) for h in heads]
        probs = [jnp.exp2(scores[h] - m_new[h]) for h in heads]
        alpha = [jnp.exp2(m_prev[h] - m_new[h]) for h in heads]
        pv = [_dot(vt_ref[lanes_of(h), :], probs[h].astype(BF16)) for h in heads]
        for h in heads:
            l_ref[h] = alpha[h] * l_ref[h] + jnp.sum(probs[h], axis=0, keepdims=True)
            acc_ref[h] = alpha[h] * acc_ref[h] + pv[h]
            m_ref[h] = m_new[h]

    @pl.when(ki < qi)
    def _():
        step(False)

    @pl.when(ki == qi)
    def _():
        step(True)
        for p in range(FOX_PAIRS_PER_STEP):
            o0 = acc_ref[2 * p] / l_ref[2 * p]
            o1 = acc_ref[2 * p + 1] / l_ref[2 * p + 1]
            o_t = jnp.concatenate([o0[:HEAD_DIM], o1[HEAD_DIM:]], axis=0)
            o_ref[:, lanes_of(2 * p)] = o_t.T.astype(o_ref.dtype)


def _fox_attention(q, qb, k, kb, vt, batch, seq):
    t = min(ATTN_TILE, seq)
    n = seq // t
    w = FOX_PAIRS_PER_STEP * LANES
    groups = FOX_WIDTH // w
    tri = [(i, j) for i in range(n) for j in range(i + 1)]
    qi_tab = jnp.asarray([i for i, _ in tri], jnp.int32)
    ki_tab = jnp.asarray([j for _, j in tri], jnp.int32)
    qspec = pl.BlockSpec((t, w), lambda bi, g, s, qi, ki: (bi * n + qi[s], g))
    kspec = pl.BlockSpec((t, w), lambda bi, g, s, qi, ki: (bi * n + ki[s], g))
    vspec = pl.BlockSpec((w, t), lambda bi, g, s, qi, ki: (g, bi * n + ki[s]))
    heads = 2 * FOX_PAIRS_PER_STEP
    grid_spec = pltpu.PrefetchScalarGridSpec(
        num_scalar_prefetch=2,
        grid=(batch, groups, len(tri)),
        in_specs=[qspec, qspec, kspec, kspec, vspec],
        out_specs=qspec,
        scratch_shapes=[pltpu.VMEM((heads, t, 2 * LANES), BF16), pltpu.VMEM((heads, 1, t), F32),
                        pltpu.VMEM((heads, 1, t), F32), pltpu.VMEM((heads, LANES, t), F32)],
    )
    return pl.pallas_call(
        _fox_kernel,
        grid_spec=grid_spec,
        out_shape=jax.ShapeDtypeStruct((batch * seq, FOX_WIDTH), BF16),
        compiler_params=_cparams(("arbitrary",) * 3),
        name="fox_attention",
    )(qi_tab, ki_tab, q, qb, k, kb, vt)


def _ssd_kernel(xbc_ref, prev_ref, fd_ref, dtt_ref, z_ref, cw_ref, cb_ref, alog_ref, alogc_ref,
                dsk_ref, nrm_ref, tri_ref, y_ref, state_ref):
    c_idx = pl.program_id(1)
    L = SSD_CHUNK
    P = HEAD_DIM
    N = SSD_STATE

    @pl.when(c_idx == 0)
    def _():
        state_ref[...] = jnp.zeros_like(state_ref)

    cur = xbc_ref[...]
    prev = jnp.where(c_idx > 0, prev_ref[...], 0.0)
    ext = jnp.concatenate([prev, cur], axis=0)
    cw = cw_ref[...]
    conv = cb_ref[...] + cw[SSD_CONV - 1:SSD_CONV, :] * cur
    for kk in range(SSD_CONV - 1):
        sh = SSD_CONV - 1 - kk
        conv = conv + cw[kk:kk + 1, :] * ext[8 - sh:8 - sh + L, :]
    xc = _silu(conv)
    xs = xc[:, :SSD_INNER]
    bm = xc[:, SSD_INNER:SSD_INNER + SSD_GROUPS * N]
    cm = xc[:, SSD_INNER + SSD_GROUPS * N:]

    dt = fd_ref[:, FOX_HEADS:FOX_HEADS + SSD_HEADS]
    a_row = -jnp.exp(alog_ref[...])
    a_col = -jnp.exp(alogc_ref[...])
    tri = tri_ref[...]
    cum = _dot_exact_lhs(tri, dt * a_row, 2)
    cum_t = lax.dot_general(_split_bf16(dtt_ref[...] * a_col, 2)[0], tri, (((1,), (1,)), ((), ())),
                            preferred_element_type=F32)
    cum_t = cum_t + lax.dot_general(_split_bf16(dtt_ref[...] * a_col, 2)[1], tri,
                                    (((1,), (1,)), ((), ())), preferred_element_type=F32)
    row_i = lax.broadcasted_iota(jnp.int32, (L, L), 0)
    col_i = lax.broadcasted_iota(jnp.int32, (L, L), 1)
    causal = row_i >= col_i

    ys = []
    for g in range(SSD_GROUPS):
        bm_g = bm[:, g * N:(g + 1) * N]
        cm_g = cm[:, g * N:(g + 1) * N]
        cb_g = _dot_nt(cm_g.astype(BF16), bm_g.astype(BF16))
        bm_t = bm_g.T
        for hh in range(SSD_HEADS // SSD_GROUPS):
            h = g * (SSD_HEADS // SSD_GROUPS) + hh
            cum_c = cum[:, h:h + 1]
            cum_r = cum_t[h:h + 1, :]
            seg = jnp.where(causal, jnp.exp(jnp.minimum(cum_c - cum_r, 0.0)), 0.0)
            x_h = xs[:, h * P:(h + 1) * P]
            xdt = (x_h * dt[:, h:h + 1]).astype(BF16)
            y_h = _dot((cb_g * seg).astype(BF16), xdt)
            h_prev = state_ref[h]
            y_h = y_h + _dot((cm_g * jnp.exp(cum_c)).astype(BF16), h_prev.astype(BF16))
            last = cum_t[h:h + 1, L - 1:L]
            dte = jnp.exp(last - cum_r)
            st = _dot((bm_t * dte).astype(BF16), xdt)
            state_ref[h] = jnp.exp(last) * h_prev + st
            ys.append(y_h)
    y = jnp.concatenate(ys, axis=1) + xs * dsk_ref[...]
    y = y * _silu(z_ref[...].astype(F32))
    gw = SSD_INNER // SSD_GROUPS
    outs = []
    for g in range(SSD_GROUPS):
        yg = y[:, g * gw:(g + 1) * gw]
        outs.append(yg * lax.rsqrt(jnp.mean(yg * yg, axis=-1, keepdims=True) + RMS_EPS))
    y_ref[...] = (jnp.concatenate(outs, axis=1) * nrm_ref[...]).astype(y_ref.dtype)


def _ssd(xbc, fd, dtt, z, conv_w, conv_b, a_log, d_skip_c, ssd_norm, tri, batch, seq):
    L = SSD_CHUNK
    nc = seq // L
    t_rows = batch * seq
    row = lambda w: pl.BlockSpec((L, w), lambda b, c: (b * nc + c, 0))
    full = lambda a: pl.BlockSpec(a.shape, lambda b, c: (0,) * a.ndim)
    prev = pl.BlockSpec((8, SSD_CONV_DIM), lambda b, c: (jnp.maximum((b * nc + c) * (L // 8) - 1, 0), 0))
    return pl.pallas_call(
        _ssd_kernel,
        grid=(batch, nc),
        in_specs=[row(SSD_CONV_DIM), prev, row(LANES),
                  pl.BlockSpec((None, SSD_HEADS, L), lambda b, c: (b, 0, c)),
                  row(SSD_INNER), full(conv_w), full(conv_b), full(a_log[None, :]),
                  full(a_log[:, None]), full(d_skip_c), full(ssd_norm), full(tri)],
        out_specs=row(SSD_INNER),
        out_shape=jax.ShapeDtypeStruct((t_rows, SSD_INNER), BF16),
        scratch_shapes=[pltpu.VMEM((SSD_HEADS, SSD_STATE, HEAD_DIM), F32)],
        compiler_params=_cparams(("arbitrary", "arbitrary")),
        name="ssd_scan",
    )(xbc, xbc, fd, dtt, z, conv_w, conv_b, a_log[None, :], a_log[:, None], d_skip_c, ssd_norm, tri)


def _swiglu_hidden_chunk(h, w13_ref, act_ref, c, tf):
    f = w13_ref.shape[1] // 2
    off = c * tf if isinstance(c, int) else pl.multiple_of(c * tf, tf)
    gate = _dot(h, w13_ref[:, pl.ds(off, tf)])
    up = _dot(h, w13_ref[:, pl.ds(f + off, tf)])
    act_ref[:, pl.ds(off, tf)] = (_silu(gate) * up).astype(BF16)


def _even_out_kernel(o_ref, y_ref, x_ref, wo_ref, g_ref, w13_ref, w2_ref, out_ref, act_ref):
    w = FOX_WIDTH
    x1 = x_ref[...] + _dot(o_ref[...], wo_ref[:w, :]) + _dot(y_ref[...], wo_ref[w:, :])
    h = _rms(x1, g_ref[...]).astype(BF16)

    for c in range(w2_ref.shape[0] // FFN_CHUNK):
        _swiglu_hidden_chunk(h, w13_ref, act_ref, c, FFN_CHUNK)
    out_ref[...] = x1 + _dot(act_ref[...], w2_ref[...])


def _even_out_ffn(o_fox, y_ssd, x2d, wo, g, w13, w2):
    t_rows = x2d.shape[0]
    tm = min(ROW_TILE, t_rows)
    row = lambda w: pl.BlockSpec((tm, w), lambda i: (i, 0))
    full = lambda a: pl.BlockSpec(a.shape, lambda i: (0,) * a.ndim, pipeline_mode=pl.Buffered(1))
    return pl.pallas_call(
        _even_out_kernel,
        grid=(t_rows // tm,),
        in_specs=[row(FOX_WIDTH), row(SSD_INNER), row(D_MODEL), full(wo), full(g), full(w13), full(w2)],
        out_specs=row(D_MODEL),
        out_shape=jax.ShapeDtypeStruct((t_rows, D_MODEL), F32),
        scratch_shapes=[pltpu.VMEM((tm, w2.shape[0]), BF16)],
        compiler_params=_cparams(("arbitrary",)),
        name="even_out_ffn",
    )(o_fox, y_ssd, x2d, wo, g, w13, w2)


def _rope(x, cos, sin_signed):
    n = x.shape[1]
    lane = lax.broadcasted_iota(jnp.int32, x.shape, 1)
    first = (lane % HEAD_DIM) < (HEAD_DIM // 2)
    rot = jnp.where(first, pltpu.roll(x, n - HEAD_DIM // 2, 1), pltpu.roll(x, HEAD_DIM // 2, 1))
    reps = n // LANES
    cos_t = jnp.concatenate([cos] * reps, axis=1)
    sin_t = jnp.concatenate([sin_signed] * reps, axis=1)
    return x * cos_t + rot * sin_t


def _emit_dilated_views(val, refs, slab_ref):
    tm, width = val.shape
    chunks = width // LANES
    for c in range(chunks):
        slab_ref[c * tm:(c + 1) * tm, :] = val[:, c * LANES:(c + 1) * LANES]
    for (_, d), ref in zip(DIL_PATTERNS, refs):
        if d == 1:
            ref[...] = val.astype(ref.dtype)
            continue
        n = tm // d
        for r in range(d):
            for c in range(chunks):
                piece = slab_ref[pl.ds(c * tm + r, n, stride=d), :]
                ref[:, r * width + c * LANES:r * width + (c + 1) * LANES] = piece.astype(ref.dtype)


def _read_dilated_view(ref, d, tm, slab_ref):
    if d == 1:
        return ref[...].astype(F32)
    width = ref.shape[1] // d
    chunks = width // LANES
    n = tm // d
    for r in range(d):
        for c in range(chunks):
            piece = ref[:, r * width + c * LANES:r * width + (c + 1) * LANES].astype(F32)
            slab_ref[pl.ds(c * tm + r, n, stride=d), :] = piece
    return jnp.concatenate([slab_ref[c * tm:(c + 1) * tm, :] for c in range(chunks)], axis=1)


def _odd_qkv_kernel(x_ref, g_ref, w_ref, qg_ref, kg_ref, hm_ref, cos_ref, sin_ref, *rest):
    n_pat = len(DIL_PATTERNS)
    q_refs, k_refs, v_refs = rest[:n_pat], rest[n_pat:2 * n_pat], rest[2 * n_pat:3 * n_pat]
    slabs = rest[3 * n_pat:]
    h = _rms(x_ref[...], g_ref[...]).astype(BF16)
    hm = hm_ref[...]
    w = DIL_WIDTH
    cos = cos_ref[...]
    sin = sin_ref[...]
    q = _head_norm_wide(_dot(h, w_ref[:, :w]), hm, qg_ref[...])
    _emit_dilated_views(_rope(q, cos, sin) * (HEAD_DIM ** -0.5 * LOG2E), q_refs, slabs[0])
    k = _head_norm_wide(_dot(h, w_ref[:, w:2 * w]), hm, kg_ref[...])
    _emit_dilated_views(_rope(k, cos, sin), k_refs, slabs[1])
    _emit_dilated_views(_dot(h, w_ref[:, 2 * w:]), v_refs, slabs[2])


def _odd_qkv(x2d, seq, g, w, qg, kg, hm, cos, sin):
    t_rows = x2d.shape[0]
    tm = min(ROW_TILE, seq)
    per_seq = seq // tm
    row = lambda rows, wd: pl.BlockSpec((rows, wd), lambda i: (i, 0))
    full = lambda a: pl.BlockSpec(a.shape, lambda i: (0,) * a.ndim)
    tab = pl.BlockSpec((tm, LANES), lambda i: (i % per_seq, 0))
    views = [(tm // d, t_rows // d, d * DIL_WIDTH) for (_, d) in DIL_PATTERNS]
    outs = pl.pallas_call(
        _odd_qkv_kernel,
        grid=(t_rows // tm,),
        in_specs=[row(tm, D_MODEL), full(g), full(w), full(qg), full(kg), full(hm), tab, tab],
        out_specs=[row(r, wd) for (r, _, wd) in views] * 3,
        out_shape=[jax.ShapeDtypeStruct((n, wd), BF16) for (_, n, wd) in views] * 3,
        scratch_shapes=[pltpu.VMEM((tm * (DIL_WIDTH // LANES), LANES), F32)] * 3,
        compiler_params=_cparams(("arbitrary",)),
        name="odd_qkv",
    )(x2d, g, w, qg, kg, hm, cos, sin)
    n_pat = len(DIL_PATTERNS)
    return outs[:n_pat], outs[n_pat:2 * n_pat], outs[2 * n_pat:]


def _dilated_kernel(q_ref, kp_ref, kc_ref, vp_ref, vc_ref, band_ref, o_ref, lse_ref):
    j = pl.program_id(2)
    blk = DIL_BLOCK
    key_row = lax.broadcasted_iota(jnp.int32, (2 * blk, blk), 0)
    band = band_ref[...]
    bias = [jnp.where((j == 0) & (key_row < blk), NEG_BIG, band)] + [band] * (DIL_SUB - 1)
    lane = lax.broadcasted_iota(jnp.int32, (blk, LANES), 1)
    ones_rows = jnp.ones((DEN_ROWS, 2 * blk), BF16)
    pairs = range(DIL_HEADS // 2)
    heads = range(DIL_HEADS)
    subs = range(DIL_SUB)
    lanes_of = lambda pr: slice(pr * LANES, (pr + 1) * LANES)

    def keys_of(prev_ref, cur_ref, u, pr):
        if u == 0:
            return jnp.concatenate([prev_ref[:, lanes_of(pr)], cur_ref[:blk, lanes_of(pr)]], axis=0)
        return cur_ref[(u - 1) * blk:(u + 1) * blk, lanes_of(pr)]

    kk = [[keys_of(kp_ref, kc_ref, u, pr) for pr in pairs] for u in subs]
    vt = [[jnp.concatenate([keys_of(vp_ref, vc_ref, u, pr).astype(F32).T.astype(BF16), ones_rows], axis=0)
           for pr in pairs] for u in subs]
    zero = jnp.zeros((blk, LANES), q_ref.dtype)
    qm = [[jnp.where((lane >= (h % 2) * HEAD_DIM) & (lane < (h % 2 + 1) * HEAD_DIM),
                     q_ref[u * blk:(u + 1) * blk, lanes_of(h // 2)], zero) for h in heads] for u in subs]
    s = [[_dot_nt(kk[u][h // 2], qm[u][h]) + bias[u] for h in heads] for u in subs]
    m = [[jnp.max(s[u][h], axis=0, keepdims=True) for h in heads] for u in subs]
    e = [[jnp.exp2(s[u][h] - m[u][h]).astype(BF16) for h in heads] for u in subs]
    ov = [[_dot(vt[u][h // 2], e[u][h]) for h in heads] for u in subs]
    for u in subs:
        den = [ov[u][h][LANES:LANES + 1, :] for h in heads]
        for pr in pairs:
            o_t = jnp.concatenate([(ov[u][2 * pr][:LANES] / den[2 * pr])[:HEAD_DIM],
                                   (ov[u][2 * pr + 1][:LANES] / den[2 * pr + 1])[HEAD_DIM:]], axis=0)
            o_ref[u * blk:(u + 1) * blk, lanes_of(pr)] = o_t.T.astype(o_ref.dtype)
        lse_t = jnp.concatenate([m[u][h] + jnp.log2(den[h]) for h in heads]
                                + [jnp.zeros((LANES - DIL_HEADS, blk), F32)], axis=0)
        lse_ref[u * blk:(u + 1) * blk, :] = lse_t.T


def _dilated_branch(q, k, v, band, dilation, batch):
    d = dilation
    w = DIL_WIDTH
    rows = DIL_SUB * DIL_BLOCK
    nb = q.shape[0] // batch // rows
    cur = pl.BlockSpec((rows, w), lambda bi, r, j: (bi * nb + j, r))
    prev = pl.BlockSpec((DIL_BLOCK, w), lambda bi, r, j: ((bi * nb + jnp.maximum(j, 1)) * DIL_SUB - 1, r))
    lse_spec = pl.BlockSpec((rows, LANES), lambda bi, r, j: (bi * nb + j, r))
    return pl.pallas_call(
        _dilated_kernel,
        grid=(batch, d, nb),
        in_specs=[cur, prev, cur, prev, cur, pl.BlockSpec(band.shape, lambda bi, r, j: (0, 0))],
        out_specs=[cur, lse_spec],
        out_shape=[jax.ShapeDtypeStruct(q.shape, BF16),
                   jax.ShapeDtypeStruct((q.shape[0], d * LANES), F32)],
        compiler_params=_cparams(("arbitrary",) * 3),
        name=f"dilated_attn_d{d}",
    )(q, k, k, v, v, band)


def _odd_out_kernel(o1_ref, o2_ref, o3_ref, l1_ref, l2_ref, l3_ref, x_ref, wo_ref, g_ref,
                    rw_ref, ex_ref, tril_ref, x1_ref, h_ref, info_ref, cnt_ref, carry_ref, slab_ref):
    i = pl.program_id(0)
    tm = x_ref.shape[0]

    @pl.when(i == 0)
    def _():
        carry_ref[...] = jnp.zeros_like(carry_ref)

    dils = [d for (_, d) in DIL_PATTERNS]
    l1, l2, l3 = [_read_dilated_view(r, d, tm, slab_ref) for r, d in zip((l1_ref, l2_ref, l3_ref), dils)]
    mx = jnp.maximum(jnp.maximum(l1, l2), l3)
    a1, a2, a3 = jnp.exp2(l1 - mx), jnp.exp2(l2 - mx), jnp.exp2(l3 - mx)
    inv = 1.0 / (a1 + a2 + a3)
    ex = ex_ref[...]
    o = None
    for a, o_ref, d in zip((a1, a2, a3), (o1_ref, o2_ref, o3_ref), dils):
        term = _dot_exact_rhs(a * inv, ex, 1) * _read_dilated_view(o_ref, d, tm, slab_ref)
        o = term if o is None else o + term
    x1 = x_ref[...] + _dot(o.astype(BF16), wo_ref[...])
    x1_ref[...] = x1
    h = _rms(x1, g_ref[...])
    _wide_to_rows(h_ref, h)

    h_hi, h_lo = _split_bf16(h, 2)
    w_hi, w_lo = _split_bf16(rw_ref[...], 2)
    logits = _dot(h_hi, w_hi) + (_dot(h_lo, w_hi) + _dot(h_hi, w_lo))
    lane = lax.broadcasted_iota(jnp.int32, logits.shape, 1)
    logits = jnp.where(lane < N_EXPERTS, logits, -jnp.inf)
    m1 = jnp.max(logits, axis=-1, keepdims=True)
    i1 = jnp.min(jnp.where(logits == m1, lane, LANES), axis=-1, keepdims=True)
    rest = jnp.where(lane == i1, -jnp.inf, logits)
    m2 = jnp.max(rest, axis=-1, keepdims=True)
    i2 = jnp.min(jnp.where(rest == m2, lane, LANES), axis=-1, keepdims=True)
    e2 = jnp.exp(m2 - m1)
    g1 = 1.0 / (1.0 + e2)
    g2 = e2 * g1
    hot1 = lane == i1
    hot2 = lane == i2
    onehot = jnp.where(hot1 | hot2, 1.0, 0.0).astype(BF16)
    before = _dot(tril_ref[...], onehot) + carry_ref[...]
    r1 = jnp.sum(jnp.where(hot1, before, 0.0), axis=-1, keepdims=True)
    r2 = jnp.sum(jnp.where(hot2, before, 0.0), axis=-1, keepdims=True)
    total = before[-1:, :] + onehot[-1:, :].astype(F32)
    carry_ref[...] = total
    cnt_ref[...] = jnp.broadcast_to(total, cnt_ref.shape)
    info = jnp.where(lane == 0, i1.astype(F32), 0.0)
    info = jnp.where(lane == 1, i2.astype(F32), info)
    info = jnp.where(lane == 2, g1, info)
    info = jnp.where(lane == 3, g2, info)
    info = jnp.where(lane == 4, r1, info)
    info = jnp.where(lane == 5, r2, info)
    info_ref[...] = info


def _odd_out_router(o1, o2, o3, l1, l2, l3, x2d, wo, g, rw, ex, tril_strict):
    t_rows = x2d.shape[0]
    tm = min(ROW_TILE, t_rows)
    n = t_rows // tm
    row = lambda w: pl.BlockSpec((tm, w), lambda i: (i, 0))
    full = lambda a: pl.BlockSpec(a.shape, lambda i: (0,) * a.ndim)
    view = lambda w: [pl.BlockSpec((tm // d, d * w), lambda i: (i, 0)) for (_, d) in DIL_PATTERNS]
    return pl.pallas_call(
        _odd_out_kernel,
        grid=(n,),
        in_specs=view(DIL_WIDTH) + view(LANES) + [row(D_MODEL), full(wo), full(g), full(rw),
                                                  full(ex), full(tril_strict)],
        out_specs=[row(D_MODEL), pl.BlockSpec((tm * TILE_SUBLANES, LANES), lambda i: (i, 0)), row(LANES),
                   pl.BlockSpec((8, LANES), lambda i: (0, 0))],
        out_shape=[jax.ShapeDtypeStruct((t_rows, D_MODEL), F32),
                   jax.ShapeDtypeStruct((t_rows * TILE_SUBLANES, LANES), F32),
                   jax.ShapeDtypeStruct((t_rows, LANES), F32), jax.ShapeDtypeStruct((8, LANES), F32)],
        scratch_shapes=[pltpu.VMEM((1, LANES), F32), pltpu.VMEM((tm * (DIL_WIDTH // LANES), LANES), F32)],
        compiler_params=_cparams(("arbitrary",)),
        name="odd_out_router",
    )(o1, o2, o3, l1, l2, l3, x2d, wo, g, rw, ex, tril_strict)


def _rows_to_wide(ref):
    n = ref.shape[0] // TILE_SUBLANES
    return jnp.concatenate([ref[pl.ds(c, n, stride=TILE_SUBLANES), :] for c in range(TILE_SUBLANES)], axis=1)


def _wide_to_rows(ref, val):
    n = ref.shape[0] // TILE_SUBLANES
    for c in range(TILE_SUBLANES):
        ref[pl.ds(c, n, stride=TILE_SUBLANES), :] = val[:, c * LANES:(c + 1) * LANES]


def _moe_kernel(te_ref, nu_ref, src_cur_ref, src_nxt_ref, dst_prv_ref, dst_cur_ref, h_hbm, w13_ref, w2_ref,
                y_hbm, xa, xb, ya, yb, act_ref, gsem, ssem):
    i = pl.program_id(0)
    n_used = nu_ref[0]
    ts = TILE_SUBLANES
    gm = xa.shape[0] // ts
    n_chunks = w2_ref.shape[0] // MOE_CHUNK
    per_chunk = gm // n_chunks
    head_rows = gm - n_chunks * per_chunk

    def tile_of(ref, row):
        return ref.at[pl.ds(pl.multiple_of(row * ts, ts), ts)]

    def gather_row(src_ref, r, xdst, sem):
        pltpu.make_async_copy(tile_of(h_hbm, src_ref[0, 0, r]), tile_of(xdst, r), sem).start()

    def scatter_row(dst_ref, r, ysrc, sem):
        pltpu.make_async_copy(tile_of(ysrc, r), tile_of(y_hbm, dst_ref[0, 0, r]), sem).start()

    def wait_rows(buf, sem):
        pltpu.make_async_copy(h_hbm.at[pl.ds(0, gm * ts)], buf, sem).wait()

    @pl.when(i == 0)
    def _():
        yb[...] = jnp.zeros_like(yb)

        def body(r, carry):
            gather_row(src_cur_ref, r, xa, gsem.at[0])
            return carry
        lax.fori_loop(0, gm, body, 0)

    def tile(x_cur, x_nxt, y_cur, y_prv, p):
        wait_rows(x_cur, gsem.at[p])

        @pl.when(i >= 1)
        def _():
            wait_rows(y_cur, ssem.at[p])

        def issue(r):
            gather_row(src_nxt_ref, r, x_nxt, gsem.at[1 - p])
            scatter_row(dst_prv_ref, r, y_prv, ssem.at[1 - p])

        for r in range(head_rows):
            issue(n_chunks * per_chunk + r)
        x = _rows_to_wide(x_cur).astype(BF16)

        def body(c, carry):
            for u in range(MOE_UNROLL):
                _swiglu_hidden_chunk(x, w13_ref, act_ref, c * MOE_UNROLL + u, MOE_CHUNK)
            for j in range(per_chunk * MOE_UNROLL):
                issue(c * (per_chunk * MOE_UNROLL) + j)
            return carry

        lax.fori_loop(0, n_chunks // MOE_UNROLL, body, 0)
        _wide_to_rows(y_cur, _dot(act_ref[...], w2_ref[...]))

        @pl.when(i == n_used - 1)
        def _():
            def body(r, carry):
                scatter_row(dst_cur_ref, r, y_cur, ssem.at[p])
                return carry
            lax.fori_loop(0, gm, body, 0)
            wait_rows(y_prv, ssem.at[1 - p])
            wait_rows(y_cur, ssem.at[p])
            wait_rows(x_nxt, gsem.at[1 - p])

    @pl.when((i < n_used) & (i % 2 == 0))
    def _():
        tile(xa, xb, ya, yb, 0)

    @pl.when((i < n_used) & (i % 2 == 1))
    def _():
        tile(xb, xa, yb, ya, 1)


def _moe_ffn(h_tiles, w13, w2, layer, tile_expert, n_used, src, dst, out_rows):
    nt = tile_expert.shape[0]
    gm = src.shape[-1]
    tab = lambda f: pl.BlockSpec((1, 1, gm), lambda i, te, nu: (f(i), 0, 0), memory_space=pltpu.SMEM)
    grid_spec = pltpu.PrefetchScalarGridSpec(
        num_scalar_prefetch=2,
        grid=(nt,),
        in_specs=[
            tab(lambda i: i),
            tab(lambda i: jnp.minimum(i + 1, nt - 1)),
            tab(lambda i: i),
            tab(lambda i: i + 1),
            pl.BlockSpec(memory_space=pl.ANY),
            pl.BlockSpec((None, None) + w13.shape[2:], lambda i, te, nu: (layer, te[i], 0, 0),
                         pipeline_mode=pl.Buffered(1)),
            pl.BlockSpec((None, None) + w2.shape[2:], lambda i, te, nu: (layer, te[i], 0, 0),
                         pipeline_mode=pl.Buffered(1)),
        ],
        out_specs=pl.BlockSpec(memory_space=pl.ANY),
        scratch_shapes=[pltpu.VMEM((gm * TILE_SUBLANES, LANES), F32)] * 4
        + [pltpu.VMEM((gm, w2.shape[2]), BF16), pltpu.SemaphoreType.DMA((2,)), pltpu.SemaphoreType.DMA((2,))],
    )
    return pl.pallas_call(
        _moe_kernel,
        grid_spec=grid_spec,
        out_shape=jax.ShapeDtypeStruct((out_rows * TILE_SUBLANES, LANES), F32),
        compiler_params=_cparams(("arbitrary",)),
        name="moe_ffn",
    )(tile_expert, n_used, src, src, dst, dst, h_tiles, w13, w2)


def _combine_kernel(x_ref, y0_ref, y1_ref, info_ref, o_ref):
    o_ref[...] = (x_ref[...] + info_ref[:, 2:3] * _rows_to_wide(y0_ref)
                  + info_ref[:, 3:4] * _rows_to_wide(y1_ref))


def _combine(x1, y, info):
    t_rows = x1.shape[0]
    tm = min(ROW_TILE, t_rows)
    row = lambda w: pl.BlockSpec((tm, w), lambda i: (i, 0))
    ysp = lambda kk: pl.BlockSpec((None, tm * TILE_SUBLANES, LANES), lambda i: (kk, i, 0))
    return pl.pallas_call(
        _combine_kernel,
        grid=(t_rows // tm,),
        in_specs=[row(D_MODEL), ysp(0), ysp(1), row(LANES)],
        out_specs=row(D_MODEL),
        out_shape=jax.ShapeDtypeStruct((t_rows, D_MODEL), F32),
        compiler_params=_cparams(("arbitrary",)),
        name="moe_combine",
    )(x1, y, y, info)


def _head_mean_matrix(width):
    idx = np.arange(width) // HEAD_DIM
    return jnp.asarray((idx[:, None] == idx[None, :]).astype(np.float32) / HEAD_DIM, dtype=BF16)


def _tril(n, strict):
    r = np.arange(n)
    m = (r[None, :] < r[:, None]) if strict else (r[None, :] <= r[:, None])
    return jnp.asarray(m.astype(np.float32), dtype=BF16)


def _rope_tables(seq):
    half = HEAD_DIM // 2
    inv_freq = ROPE_THETA ** (-jnp.arange(half, dtype=F32) / half)
    ang = jnp.arange(seq, dtype=F32)[:, None] * inv_freq[None, :]
    cos = jnp.cos(ang)
    sin = jnp.sin(ang)
    cos_t = jnp.concatenate([cos, cos, cos, cos], axis=1)
    sin_t = jnp.concatenate([-sin, sin, -sin, sin], axis=1)
    return cos_t, sin_t


def _band_bias():
    qi = np.arange(DIL_BLOCK)[None, :]
    ki = np.arange(-DIL_BLOCK, DIL_BLOCK)[:, None]
    rel = qi - ki
    return jnp.asarray(np.where((rel >= 0) & (rel <= DIL_BLOCK), 0.0, NEG_BIG).astype(np.float32))


def _head_expand():
    m = np.zeros((LANES, DIL_WIDTH), np.float32)
    for h in range(DIL_HEADS):
        m[h, h * HEAD_DIM:(h + 1) * HEAD_DIM] = 1.0
    return jnp.asarray(m, dtype=BF16)


def _bias_placement():
    pq = np.zeros((BIAS_PARTS, LANES, FOX_WIDTH), np.float32)
    pk = np.zeros((BIAS_PARTS, LANES, FOX_WIDTH), np.float32)
    qone = np.zeros((1, FOX_WIDTH), np.float32)
    kone = np.zeros((1, FOX_WIDTH), np.float32)
    for head in range(FOX_HEADS):
        base = (head // 2) * LANES + (head % 2) * BIAS_LANES
        for j in range(BIAS_PARTS):
            pq[j, head, base + j] = 1.0
            pk[j, head, base + BIAS_PARTS + j] = -1.0
            qone[0, base + BIAS_PARTS + j] = 1.0
            kone[0, base + j] = 1.0
    return (jnp.asarray(pq, dtype=BF16), jnp.asarray(pk, dtype=BF16), jnp.asarray(qone), jnp.asarray(kone))


def _even_layer(x2d, batch, seq, norm1, w_in, fox_qn, fox_kn, fox_fbias, conv_w, conv_b, dt_bias, a_log,
                d_skip, ssd_norm, w_out, norm2, ffn_w13, ffn_w2):
    w = FOX_WIDTH
    o_z = 3 * w + FOX_HEADS
    o_x = o_z + SSD_INNER
    o_dt = o_x + SSD_CONV_DIM
    wqk = w_in[:, :2 * w].astype(BF16)
    wvt = w_in[:, 2 * w:3 * w].T.astype(BF16)
    wzx = w_in[:, o_z:o_dt].astype(BF16)
    pad = LANES - FOX_HEADS - SSD_HEADS
    wfd = jnp.concatenate([w_in[:, 3 * w:o_z], w_in[:, o_dt:], jnp.zeros((D_MODEL, pad), F32)], axis=1).astype(BF16)
    fdb = jnp.concatenate([fox_fbias, dt_bias, jnp.zeros((pad,), F32)])[None, :]
    tm = min(ROW_TILE, seq)
    q, qb, k, kb, vt, z, xbc, fd = _even_in_proj(
        x2d, seq, norm1[None, :], wqk, wvt, wzx, wfd, fox_qn.reshape(1, w), fox_kn.reshape(1, w), fdb,
        _head_mean_matrix(w), _tril(tm, False), *_bias_placement())

    o_fox = _fox_attention(q, qb, k, kb, vt, batch, seq)

    dtt = jnp.transpose(fd[:, FOX_HEADS:FOX_HEADS + SSD_HEADS].reshape(batch, seq, SSD_HEADS), (0, 2, 1))
    d_skip_c = jnp.repeat(d_skip, HEAD_DIM)[None, :]
    y = _ssd(xbc, fd, dtt, z, conv_w, conv_b[None, :], a_log, d_skip_c, ssd_norm[None, :],
             _tril(SSD_CHUNK, False), batch, seq)

    return _even_out_ffn(o_fox, y, x2d, w_out.astype(BF16), norm2[None, :],
                         ffn_w13.astype(BF16), ffn_w2.astype(BF16))


def _moe_tables(info, counts, t_rows, gm):
    e1 = info[:, 0].astype(jnp.int32)
    e2 = info[:, 1].astype(jnp.int32)
    r1 = info[:, 4].astype(jnp.int32)
    r2 = info[:, 5].astype(jnp.int32)
    cnt = counts[0, :N_EXPERTS].astype(jnp.int32)
    tiles = (cnt + gm - 1) // gm
    tile_end = jnp.cumsum(tiles)
    offs = (tile_end - tiles) * gm
    nt = (2 * t_rows) // gm + N_EXPERTS
    tile_idx = jnp.arange(nt, dtype=jnp.int32)
    tile_expert = jnp.minimum(jnp.sum((tile_idx[:, None] >= tile_end[None, :]).astype(jnp.int32), axis=1),
                              N_EXPERTS - 1)
    n_used = tile_end[-1:].astype(jnp.int32)
    slots = jnp.concatenate([offs[e1] + r1, offs[e2] + r2])
    codes = jnp.full((nt * gm,), -1, jnp.int32).at[slots].set(
        jnp.arange(2 * t_rows, dtype=jnp.int32), unique_indices=True)
    out_half = t_rows + N_EXPERTS * gm
    second = codes >= t_rows
    src = jnp.where(codes < 0, 0, jnp.where(second, codes - t_rows, codes))
    spare = t_rows + jnp.repeat(tile_expert, gm) * gm + jnp.tile(jnp.arange(gm, dtype=jnp.int32), nt)
    dst = jnp.where(codes < 0, spare, jnp.where(second, codes - t_rows + out_half, codes))
    dst = jnp.concatenate([t_rows + jnp.arange(gm, dtype=jnp.int32), dst])
    return tile_expert, n_used, src.reshape(nt, 1, gm), dst.reshape(nt + 1, 1, gm), out_half


def _odd_layer(x2d, batch, seq, norm1, w_qkv, qn, kn, w_out, norm2, router, moe_w13, moe_w2, layer):
    t_rows = batch * seq
    cos, sin = _rope_tables(seq)
    qs, ks, vs = _odd_qkv(x2d, seq, norm1[None, :], w_qkv.astype(BF16), qn.reshape(1, DIL_WIDTH),
                          kn.reshape(1, DIL_WIDTH), _head_expand(), cos, sin)
    band = _band_bias()
    outs, lses = zip(*[_dilated_branch(q, k, v, band, d, batch)
                       for q, k, v, (_, d) in zip(qs, ks, vs, DIL_PATTERNS)])
    rw = jnp.concatenate([router, jnp.zeros((D_MODEL, LANES - N_EXPERTS), F32)], axis=1)
    tm = min(ROW_TILE, t_rows)
    x1, h, info, counts = _odd_out_router(*outs, *lses, x2d, w_out.astype(BF16), norm2[None, :], rw,
                                          _head_expand(), _tril(tm, True))
    gm = min(MOE_TILE, t_rows)
    tile_expert, n_used, src, dst, out_half = _moe_tables(info, counts, t_rows, gm)
    y = _moe_ffn(h, moe_w13, moe_w2, layer, tile_expert, n_used, src, dst, 2 * out_half)
    return _combine(x1, y.reshape(2, out_half * TILE_SUBLANES, LANES), info)


def kernel(x, e_norm1, e_w_in, e_fox_qn, e_fox_kn, e_fox_fbias, e_conv_w, e_conv_b, e_dt_bias, e_a_log, e_d_skip, e_ssd_norm, e_w_out, e_norm2, e_ffn_w13, e_ffn_w2, o_norm1, o_w_qkv, o_qn, o_kn, o_w_out, o_norm2, o_router, o_moe_w13, o_moe_w2):
    batch, seq, _ = x.shape
    depth = e_norm1.shape[0] + o_norm1.shape[0]
    x2d = x.reshape(batch * seq, D_MODEL)
    moe_w13 = o_moe_w13.astype(BF16)
    moe_w2 = o_moe_w2.astype(BF16)
    for i in range(depth):
        j = i // 2
        if i % 2 == 0:
            x2d = _even_layer(x2d, batch, seq, e_norm1[j], e_w_in[j], e_fox_qn[j], e_fox_kn[j], e_fox_fbias[j],
                              e_conv_w[j], e_conv_b[j], e_dt_bias[j], e_a_log[j], e_d_skip[j], e_ssd_norm[j],
                              e_w_out[j], e_norm2[j], e_ffn_w13[j], e_ffn_w2[j])
        else:
            x2d = _odd_layer(x2d, batch, seq, o_norm1[j], o_w_qkv[j], o_qn[j], o_kn[j], o_w_out[j], o_norm2[j],
                             o_router[j], moe_w13, moe_w2, j)
    return x2d.reshape(batch, seq, D_MODEL)
```

```python
import functools
import math

import jax
import jax.numpy as jnp
import numpy as np
from jax import lax
from jax.experimental import pallas as pl
from jax.experimental.pallas import tpu as pltpu

F32 = jnp.float32
BF16 = jnp.bfloat16

D_MODEL = 1024
HEAD_DIM = 64
RMS_EPS = 1e-6
ROPE_THETA = 10000.0
FOX_HEADS = 8
FOX_WIDTH = FOX_HEADS * HEAD_DIM
SSD_HEADS = 8
SSD_INNER = 512
SSD_GROUPS = 2
SSD_STATE = 128
SSD_CONV = 4
SSD_CHUNK = 128
SSD_CONV_DIM = SSD_INNER + 2 * SSD_GROUPS * SSD_STATE
DIL_HEADS = 16
DIL_WIDTH = DIL_HEADS * HEAD_DIM
DIL_PATTERNS = ((128, 1), (512, 4), (2048, 16))
DIL_BLOCK = 128
DIL_SUB = 4
FFN_DIM = 2816
N_EXPERTS = 8
EXPERT_DIM = 3584

LANES = 128
TILE_SUBLANES = D_MODEL // LANES
NEG_BIG = -1e30
LOG2E = math.log2(math.e)
DEN_ROWS = 16
BIAS_PARTS = 3
BIAS_LANES = 2 * BIAS_PARTS
VMEM_LIMIT = 56 * 1024 * 1024

ROW_TILE = 512
ATTN_TILE = 512
FOX_PAIRS_PER_STEP = 4
FFN_CHUNK = 256
MOE_CHUNK = 256
MOE_UNROLL = 14
MOE_TILE = 512


def _cparams(sem):
    return pltpu.CompilerParams(dimension_semantics=sem, vmem_limit_bytes=VMEM_LIMIT)


def _softplus_parts(x):
    return jnp.log(1.0 + jnp.exp(-jnp.abs(x)))


def _split_bf16(a, parts):
    out = []
    r = a
    for _ in range(parts):
        p = r.astype(BF16)
        out.append(p)
        r = r - p.astype(F32)
    return out


def _dot(a, b):
    return jnp.dot(a, b, preferred_element_type=F32)


def _dot_nt(a, b):
    return lax.dot_general(a, b, (((1,), (1,)), ((), ())), preferred_element_type=F32)


def _dot_exact_rhs(a_f32, b_bf16, parts):
    acc = None
    for p in _split_bf16(a_f32, parts):
        t = _dot(p, b_bf16)
        acc = t if acc is None else acc + t
    return acc


def _dot_exact_lhs(a_bf16, b_f32, parts):
    acc = None
    for p in _split_bf16(b_f32, parts):
        t = _dot(a_bf16, p)
        acc = t if acc is None else acc + t
    return acc


def _rms(x, g):
    return x * lax.rsqrt(jnp.mean(x * x, axis=-1, keepdims=True) + RMS_EPS) * g


def _head_norm(x, hm, gain):
    ms = _dot((x * x).astype(BF16), hm)
    return x * lax.rsqrt(ms + RMS_EPS) * gain


def _silu(x):
    return x * (1.0 / (1.0 + jnp.exp(-x)))


def _even_in_kernel(tiles_per_seq, after_moe, *refs):
    n_lead = 4 if after_moe else 1
    lead, refs = refs[:n_lead], refs[n_lead:]
    (g_ref, wqk_ref, wvt_ref, wzx_ref, wfd_ref, qg_ref, kg_ref, fdb_ref, hm_ref, tril_ref, pq_ref, pk_ref,
     qone_ref, kone_ref, q_ref, qb_ref, k_ref, kb_ref, vt_ref, z_ref, xbc_ref, fd_ref) = refs[:22]
    carry_ref = refs[-1]
    i = pl.program_id(0)

    @pl.when(i % tiles_per_seq == 0)
    def _():
        carry_ref[...] = jnp.zeros_like(carry_ref)

    if after_moe:
        x1_ref, y0_ref, y1_ref, info_ref = lead
        x = (x1_ref[...] + info_ref[:, 2:3] * _rows_to_wide(y0_ref)
             + info_ref[:, 3:4] * _rows_to_wide(y1_ref))
        refs[22][...] = x
    else:
        x = lead[0][...]
    h = _rms(x, g_ref[...]).astype(BF16)
    qk = _dot(h, wqk_ref[...])
    hm = hm_ref[...]
    w = FOX_WIDTH
    q_ref[...] = (_head_norm(qk[:, :w], hm, qg_ref[...]) * (HEAD_DIM ** -0.5 * LOG2E)).astype(BF16)
    k_ref[...] = _head_norm(qk[:, w:], hm, kg_ref[...]).astype(BF16)
    vt_ref[...] = _dot_nt(wvt_ref[...], h).astype(BF16)
    zx = _dot(h, wzx_ref[...])
    z_ref[...] = zx[:, :SSD_INNER].astype(BF16)
    xbc_ref[...] = zx[:, SSD_INNER:]
    fd = _dot(h, wfd_ref[...]) + fdb_ref[...]
    t = _softplus_parts(fd)
    log_f = jnp.minimum(fd, 0.0) - t
    dt = jnp.maximum(fd, 0.0) + t
    c = _dot_exact_lhs(tril_ref[...], log_f, 2) + carry_ref[...]
    carry_ref[...] = c[-1:, :]
    lane = lax.broadcasted_iota(jnp.int32, fd.shape, 1)
    fd_ref[...] = jnp.where(lane < FOX_HEADS, c, dt)
    qb = qone_ref[...]
    kb = kone_ref[...]
    for j, part in enumerate(_split_bf16(c * LOG2E, 3)):
        qb = qb + _dot(part, pq_ref[j])
        kb = kb + _dot(part, pk_ref[j])
    qb_ref[...] = qb.astype(BF16)
    kb_ref[...] = kb.astype(BF16)


def _even_in_proj(stream, seq, g, wqk, wvt, wzx, wfd, qg, kg, fdb, hm, tril, pq, pk, qone, kone):
    after_moe = isinstance(stream, tuple)
    t_rows = stream[0].shape[0] if after_moe else stream.shape[0]
    tm = min(ROW_TILE, seq)
    n = t_rows // tm
    row = lambda w: pl.BlockSpec((tm, w), lambda i: (i, 0))
    full = lambda a: pl.BlockSpec(a.shape, lambda i: (0,) * a.ndim)
    consts = (g, wqk, wvt, wzx, wfd, qg, kg, fdb, hm, tril, pq, pk, qone, kone)
    lead, lead_specs = (stream,), [row(D_MODEL)]
    out_specs = [row(FOX_WIDTH)] * 4 + [pl.BlockSpec((FOX_WIDTH, tm), lambda i: (0, i)),
                                        row(SSD_INNER), row(SSD_CONV_DIM), row(LANES)]
    out_shape = ([jax.ShapeDtypeStruct((t_rows, FOX_WIDTH), BF16)] * 4
                 + [jax.ShapeDtypeStruct((FOX_WIDTH, t_rows), BF16),
                    jax.ShapeDtypeStruct((t_rows, SSD_INNER), BF16),
                    jax.ShapeDtypeStruct((t_rows, SSD_CONV_DIM), F32),
                    jax.ShapeDtypeStruct((t_rows, LANES), F32)])
    if after_moe:
        x1, y, info = stream
        ysp = lambda kk: pl.BlockSpec((None, tm * TILE_SUBLANES, LANES), lambda i: (kk, i, 0))
        lead, lead_specs = (x1, y, y, info), [row(D_MODEL), ysp(0), ysp(1), row(LANES)]
        out_specs = out_specs + [row(D_MODEL)]
        out_shape = out_shape + [jax.ShapeDtypeStruct((t_rows, D_MODEL), F32)]
    return pl.pallas_call(
        functools.partial(_even_in_kernel, seq // tm, after_moe),
        grid=(n,),
        in_specs=lead_specs + [full(a) for a in consts],
        out_specs=out_specs,
        out_shape=out_shape,
        scratch_shapes=[pltpu.VMEM((1, LANES), F32)],
        compiler_params=_cparams(("arbitrary",)),
        name="even_in_proj",
    )(*lead, *consts)


def _fox_kernel(qi_ref, ki_ref, q_ref, qb_ref, k_ref, kb_ref, vt_ref, o_ref, qs_ref, m_ref, l_ref, acc_ref):
    step_idx = pl.program_id(2)
    qi = qi_ref[step_idx]
    ki = ki_ref[step_idx]
    tq = q_ref.shape[0]
    tk = k_ref.shape[0]
    heads = range(2 * FOX_PAIRS_PER_STEP)
    lanes_of = lambda h: slice((h // 2) * LANES, (h // 2 + 1) * LANES)

    @pl.when(ki == 0)
    def _():
        m_ref[...] = jnp.full_like(m_ref, NEG_BIG)
        l_ref[...] = jnp.zeros_like(l_ref)
        acc_ref[...] = jnp.zeros_like(acc_ref)
        lane = lax.broadcasted_iota(jnp.int32, (tq, LANES), 1)
        zero = jnp.zeros((tq, LANES), q_ref.dtype)
        for h in heads:
            hh = h % 2
            qh = jnp.where((lane >= hh * HEAD_DIM) & (lane < (hh + 1) * HEAD_DIM), q_ref[:, lanes_of(h)], zero)
            bh = jnp.where((lane >= hh * BIAS_LANES) & (lane < (hh + 1) * BIAS_LANES), qb_ref[:, lanes_of(h)], zero)
            qs_ref[h] = jnp.concatenate([qh, bh], axis=1)

    def step(masked):
        if masked:
            keep = (lax.broadcasted_iota(jnp.int32, (tk, tq), 0)
                    <= lax.broadcasted_iota(jnp.int32, (tk, tq), 1))
        ka = [jnp.concatenate([k_ref[:, lanes_of(2 * p)], kb_ref[:, lanes_of(2 * p)]], axis=1)
              for p in range(FOX_PAIRS_PER_STEP)]
        scores = [_dot_nt(ka[h // 2], qs_ref[h]) for h in heads]
        if masked:
            scores = [jnp.where(keep, s, NEG_BIG) for s in scores]
        m_prev = [m_ref[h] for h in heads]
        m_new = [jnp.maximum(m_prev[h], jnp.max(scores[h], axis=0, keepdims=True)) for h in heads]
        probs = [jnp.exp2(scores[h] - m_new[h]) for h in heads]
        alpha = [jnp.exp2(m_prev[h] - m_new[h]) for h in heads]
        pv = [_dot(vt_ref[lanes_of(h), :], probs[h].astype(BF16)) for h in heads]
        for h in heads:
            l_ref[h] = alpha[h] * l_ref[h] + jnp.sum(probs[h], axis=0, keepdims=True)
            acc_ref[h] = alpha[h] * acc_ref[h] + pv[h]
            m_ref[h] = m_new[h]

    @pl.when(ki < qi)
    def _():
        step(False)

    @pl.when(ki == qi)
    def _():
        step(True)
        for p in range(FOX_PAIRS_PER_STEP):
            o0 = acc_ref[2 * p] / l_ref[2 * p]
            o1 = acc_ref[2 * p + 1] / l_ref[2 * p + 1]
            o_t = jnp.concatenate([o0[:HEAD_DIM], o1[HEAD_DIM:]], axis=0)
            o_ref[:, lanes_of(2 * p)] = o_t.T.astype(o_ref.dtype)


def _fox_attention(q, qb, k, kb, vt, batch, seq):
    t = min(ATTN_TILE, seq)
    n = seq // t
    w = FOX_PAIRS_PER_STEP * LANES
    groups = FOX_WIDTH // w
    tri = [(i, j) for i in range(n) for j in range(i + 1)]
    qi_tab = jnp.asarray([i for i, _ in tri], jnp.int32)
    ki_tab = jnp.asarray([j for _, j in tri], jnp.int32)
    qspec = pl.BlockSpec((t, w), lambda bi, g, s, qi, ki: (bi * n + qi[s], g))
    kspec = pl.BlockSpec((t, w), lambda bi, g, s, qi, ki: (bi * n + ki[s], g))
    vspec = pl.BlockSpec((w, t), lambda bi, g, s, qi, ki: (g, bi * n + ki[s]))
    heads = 2 * FOX_PAIRS_PER_STEP
    grid_spec = pltpu.PrefetchScalarGridSpec(
        num_scalar_prefetch=2,
        grid=(batch, groups, len(tri)),
        in_specs=[qspec, qspec, kspec, kspec, vspec],
        out_specs=qspec,
        scratch_shapes=[pltpu.VMEM((heads, t, 2 * LANES), BF16), pltpu.VMEM((heads, 1, t), F32),
                        pltpu.VMEM((heads, 1, t), F32), pltpu.VMEM((heads, LANES, t), F32)],
    )
    return pl.pallas_call(
        _fox_kernel,
        grid_spec=grid_spec,
        out_shape=jax.ShapeDtypeStruct((batch * seq, FOX_WIDTH), BF16),
        compiler_params=_cparams(("arbitrary",) * 3),
        name="fox_attention",
    )(qi_tab, ki_tab, q, qb, k, kb, vt)


def _ssd_kernel(xbc_ref, prev_ref, fd_ref, dtt_ref, z_ref, cw_ref, cb_ref, alog_ref, alogc_ref,
                dsk_ref, nrm_ref, tri_ref, y_ref, state_ref):
    c_idx = pl.program_id(1)
    L = SSD_CHUNK
    P = HEAD_DIM
    N = SSD_STATE

    @pl.when(c_idx == 0)
    def _():
        state_ref[...] = jnp.zeros_like(state_ref)

    cur = xbc_ref[...]
    prev = jnp.where(c_idx > 0, prev_ref[...], 0.0)
    ext = jnp.concatenate([prev, cur], axis=0)
    cw = cw_ref[...]
    conv = cb_ref[...] + cw[SSD_CONV - 1:SSD_CONV, :] * cur
    for kk in range(SSD_CONV - 1):
        sh = SSD_CONV - 1 - kk
        conv = conv + cw[kk:kk + 1, :] * ext[8 - sh:8 - sh + L, :]
    xc = _silu(conv)
    xs = xc[:, :SSD_INNER]
    bm = xc[:, SSD_INNER:SSD_INNER + SSD_GROUPS * N]
    cm = xc[:, SSD_INNER + SSD_GROUPS * N:]

    dt = fd_ref[:, FOX_HEADS:FOX_HEADS + SSD_HEADS]
    a_row = -jnp.exp(alog_ref[...])
    a_col = -jnp.exp(alogc_ref[...])
    tri = tri_ref[...]
    cum = _dot_exact_lhs(tri, dt * a_row, 2)
    cum_t = lax.dot_general(_split_bf16(dtt_ref[...] * a_col, 2)[0], tri, (((1,), (1,)), ((), ())),
                            preferred_element_type=F32)
    cum_t = cum_t + lax.dot_general(_split_bf16(dtt_ref[...] * a_col, 2)[1], tri,
                                    (((1,), (1,)), ((), ())), preferred_element_type=F32)
    row_i = lax.broadcasted_iota(jnp.int32, (L, L), 0)
    col_i = lax.broadcasted_iota(jnp.int32, (L, L), 1)
    causal = row_i >= col_i

    ys = []
    for g in range(SSD_GROUPS):
        bm_g = bm[:, g * N:(g + 1) * N]
        cm_g = cm[:, g * N:(g + 1) * N]
        cb_g = _dot_nt(cm_g.astype(BF16), bm_g.astype(BF16))
        bm_t = bm_g.T
        for hh in range(SSD_HEADS // SSD_GROUPS):
            h = g * (SSD_HEADS // SSD_GROUPS) + hh
            cum_c = cum[:, h:h + 1]
            cum_r = cum_t[h:h + 1, :]
            seg = jnp.where(causal, jnp.exp(jnp.minimum(cum_c - cum_r, 0.0)), 0.0)
            x_h = xs[:, h * P:(h + 1) * P]
            xdt = (x_h * dt[:, h:h + 1]).astype(BF16)
            y_h = _dot((cb_g * seg).astype(BF16), xdt)
            h_prev = state_ref[h]
            y_h = y_h + _dot((cm_g * jnp.exp(cum_c)).astype(BF16), h_prev.astype(BF16))
            last = cum_t[h:h + 1, L - 1:L]
            dte = jnp.exp(last - cum_r)
            st = _dot((bm_t * dte).astype(BF16), xdt)
            state_ref[h] = jnp.exp(last) * h_prev + st
            ys.append(y_h)
    y = jnp.concatenate(ys, axis=1) + xs * dsk_ref[...]
    y = y * _silu(z_ref[...].astype(F32))
    gw = SSD_INNER // SSD_GROUPS
    outs = []
    for g in range(SSD_GROUPS):
        yg = y[:, g * gw:(g + 1) * gw]
        outs.append(yg * lax.rsqrt(jnp.mean(yg * yg, axis=-1, keepdims=True) + RMS_EPS))
    y_ref[...] = (jnp.concatenate(outs, axis=1) * nrm_ref[...]).astype(y_ref.dtype)


def _ssd(xbc, fd, dtt, z, conv_w, conv_b, a_log, d_skip_c, ssd_norm, tri, batch, seq):
    L = SSD_CHUNK
    nc = seq // L
    t_rows = batch * seq
    row = lambda w: pl.BlockSpec((L, w), lambda b, c: (b * nc + c, 0))
    full = lambda a: pl.BlockSpec(a.shape, lambda b, c: (0,) * a.ndim)
    prev = pl.BlockSpec((8, SSD_CONV_DIM), lambda b, c: (jnp.maximum((b * nc + c) * (L // 8) - 1, 0), 0))
    return pl.pallas_call(
        _ssd_kernel,
        grid=(batch, nc),
        in_specs=[row(SSD_CONV_DIM), prev, row(LANES),
                  pl.BlockSpec((None, SSD_HEADS, L), lambda b, c: (b, 0, c)),
                  row(SSD_INNER), full(conv_w), full(conv_b), full(a_log[None, :]),
                  full(a_log[:, None]), full(d_skip_c), full(ssd_norm), full(tri)],
        out_specs=row(SSD_INNER),
        out_shape=jax.ShapeDtypeStruct((t_rows, SSD_INNER), BF16),
        scratch_shapes=[pltpu.VMEM((SSD_HEADS, SSD_STATE, HEAD_DIM), F32)],
        compiler_params=_cparams(("arbitrary", "arbitrary")),
        name="ssd_scan",
    )(xbc, xbc, fd, dtt, z, conv_w, conv_b, a_log[None, :], a_log[:, None], d_skip_c, ssd_norm, tri)


def _swiglu_hidden_chunk(h, w13_ref, act_ref, c, tf):
    f = w13_ref.shape[1] // 2
    off = c * tf if isinstance(c, int) else pl.multiple_of(c * tf, tf)
    gate = _dot(h, w13_ref[:, pl.ds(off, tf)])
    up = _dot(h, w13_ref[:, pl.ds(f + off, tf)])
    act_ref[:, pl.ds(off, tf)] = (_silu(gate) * up).astype(BF16)


def _even_out_kernel(o_ref, y_ref, x_ref, wo_ref, g_ref, w13_ref, w2_ref, out_ref, act_ref):
    w = FOX_WIDTH
    x1 = x_ref[...] + _dot(o_ref[...], wo_ref[:w, :]) + _dot(y_ref[...], wo_ref[w:, :])
    h = _rms(x1, g_ref[...]).astype(BF16)

    for c in range(w2_ref.shape[0] // FFN_CHUNK):
        _swiglu_hidden_chunk(h, w13_ref, act_ref, c, FFN_CHUNK)
    out_ref[...] = x1 + _dot(act_ref[...], w2_ref[...])


def _even_out_ffn(o_fox, y_ssd, x2d, wo, g, w13, w2):
    t_rows = x2d.shape[0]
    tm = min(ROW_TILE, t_rows)
    row = lambda w: pl.BlockSpec((tm, w), lambda i: (i, 0))
    full = lambda a: pl.BlockSpec(a.shape, lambda i: (0,) * a.ndim, pipeline_mode=pl.Buffered(1))
    return pl.pallas_call(
        _even_out_kernel,
        grid=(t_rows // tm,),
        in_specs=[row(FOX_WIDTH), row(SSD_INNER), row(D_MODEL), full(wo), full(g), full(w13), full(w2)],
        out_specs=row(D_MODEL),
        out_shape=jax.ShapeDtypeStruct((t_rows, D_MODEL), F32),
        scratch_shapes=[pltpu.VMEM((tm, w2.shape[0]), BF16)],
        compiler_params=_cparams(("arbitrary",)),
        name="even_out_ffn",
    )(o_fox, y_ssd, x2d, wo, g, w13, w2)


def _rope(x, cos, sin_signed):
    lane = lax.broadcasted_iota(jnp.int32, x.shape, 1)
    first = (lane % HEAD_DIM) < (HEAD_DIM // 2)
    rot = jnp.where(first, pltpu.roll(x, LANES - HEAD_DIM // 2, 1), pltpu.roll(x, HEAD_DIM // 2, 1))
    return x * cos + rot * sin_signed


def _emit_dilated_chunk(val, c, width, refs, slab_ref):
    tm = val.shape[0]
    slab_ref[c * tm:(c + 1) * tm, :] = val
    for (_, d), ref in zip(DIL_PATTERNS, refs):
        if d == 1:
            ref[:, c * LANES:(c + 1) * LANES] = val.astype(ref.dtype)
            continue
        n = tm // d
        for r in range(d):
            piece = slab_ref[pl.ds(c * tm + r, n, stride=d), :]
            ref[:, r * width + c * LANES:r * width + (c + 1) * LANES] = piece.astype(ref.dtype)


def _read_dilated_view(ref, d, tm, slab_ref):
    if d == 1:
        return ref[...].astype(F32)
    width = ref.shape[1] // d
    chunks = width // LANES
    n = tm // d
    for r in range(d):
        for c in range(chunks):
            piece = ref[:, r * width + c * LANES:r * width + (c + 1) * LANES].astype(F32)
            slab_ref[pl.ds(c * tm + r, n, stride=d), :] = piece
    return jnp.concatenate([slab_ref[c * tm:(c + 1) * tm, :] for c in range(chunks)], axis=1)


def _odd_qkv_kernel(x_ref, g_ref, w_ref, qg_ref, kg_ref, hm_ref, cos_ref, sin_ref, *rest):
    n_pat = len(DIL_PATTERNS)
    q_refs, k_refs, v_refs = rest[:n_pat], rest[n_pat:2 * n_pat], rest[2 * n_pat:3 * n_pat]
    slabs = rest[3 * n_pat:]
    h = _rms(x_ref[...], g_ref[...]).astype(BF16)
    expand = hm_ref[...]
    w = DIL_WIDTH
    cos = cos_ref[...]
    sin = sin_ref[...]

    def normed_rotated(col, gain_ref, scale, refs, slab_ref):
        x = _dot(h, w_ref[:, col:col + w])
        sums = _dot_nt((x * x).astype(BF16), expand) * (1.0 / HEAD_DIM)
        inv = lax.rsqrt(_dot_exact_rhs(sums, expand, 2) + RMS_EPS)
        for c in range(w // LANES):
            sl = slice(c * LANES, (c + 1) * LANES)
            xc = x[:, sl] * inv[:, sl] * gain_ref[:, sl]
            _emit_dilated_chunk(_rope(xc, cos, sin) * scale, c, w, refs, slab_ref)

    normed_rotated(0, qg_ref, HEAD_DIM ** -0.5 * LOG2E, q_refs, slabs[0])
    normed_rotated(w, kg_ref, 1.0, k_refs, slabs[1])
    v = _dot(h, w_ref[:, 2 * w:])
    for c in range(w // LANES):
        _emit_dilated_chunk(v[:, c * LANES:(c + 1) * LANES], c, w, v_refs, slabs[2])


def _odd_qkv(x2d, seq, g, w, qg, kg, hm, cos, sin):
    t_rows = x2d.shape[0]
    tm = min(ROW_TILE, seq)
    per_seq = seq // tm
    row = lambda rows, wd: pl.BlockSpec((rows, wd), lambda i: (i, 0))
    full = lambda a: pl.BlockSpec(a.shape, lambda i: (0,) * a.ndim)
    tab = pl.BlockSpec((tm, LANES), lambda i: (i % per_seq, 0))
    views = [(tm // d, t_rows // d, d * DIL_WIDTH) for (_, d) in DIL_PATTERNS]
    outs = pl.pallas_call(
        _odd_qkv_kernel,
        grid=(t_rows // tm,),
        in_specs=[row(tm, D_MODEL), full(g), full(w), full(qg), full(kg), full(hm), tab, tab],
        out_specs=[row(r, wd) for (r, _, wd) in views] * 3,
        out_shape=[jax.ShapeDtypeStruct((n, wd), BF16) for (_, n, wd) in views] * 3,
        scratch_shapes=[pltpu.VMEM((tm * (DIL_WIDTH // LANES), LANES), F32)] * 3,
        compiler_params=_cparams(("arbitrary",)),
        name="odd_qkv",
    )(x2d, g, w, qg, kg, hm, cos, sin)
    n_pat = len(DIL_PATTERNS)
    return outs[:n_pat], outs[n_pat:2 * n_pat], outs[2 * n_pat:]


def _dilated_kernel(q_ref, kp_ref, kc_ref, vp_ref, vc_ref, band_ref, o_ref, lse_ref):
    j = pl.program_id(2)
    blk = DIL_BLOCK
    n_sub = q_ref.shape[0] // blk
    key_row = lax.broadcasted_iota(jnp.int32, (2 * blk, blk), 0)
    band = band_ref[...]
    bias = [jnp.where((j == 0) & (key_row < blk), NEG_BIG, band)] + [band] * (n_sub - 1)
    lane = lax.broadcasted_iota(jnp.int32, (blk, LANES), 1)
    ones_rows = jnp.ones((DEN_ROWS, 2 * blk), BF16)
    pairs = range(DIL_HEADS // 2)
    heads = range(DIL_HEADS)
    subs = range(n_sub)
    lanes_of = lambda pr: slice(pr * LANES, (pr + 1) * LANES)

    def keys_of(prev_ref, cur_ref, u, pr):
        if u == 0:
            return jnp.concatenate([prev_ref[:, lanes_of(pr)], cur_ref[:blk, lanes_of(pr)]], axis=0)
        return cur_ref[(u - 1) * blk:(u + 1) * blk, lanes_of(pr)]

    kk = [[keys_of(kp_ref, kc_ref, u, pr) for pr in pairs] for u in subs]
    vt = [[jnp.concatenate([keys_of(vp_ref, vc_ref, u, pr).astype(F32).T.astype(BF16), ones_rows], axis=0)
           for pr in pairs] for u in subs]
    zero = jnp.zeros((blk, LANES), q_ref.dtype)
    qm = [[jnp.where((lane >= (h % 2) * HEAD_DIM) & (lane < (h % 2 + 1) * HEAD_DIM),
                     q_ref[u * blk:(u + 1) * blk, lanes_of(h // 2)], zero) for h in heads] for u in subs]
    s = [[_dot_nt(kk[u][h // 2], qm[u][h]) + bias[u] for h in heads] for u in subs]
    m = [[jnp.max(s[u][h], axis=0, keepdims=True) for h in heads] for u in subs]
    e = [[jnp.exp2(s[u][h] - m[u][h]).astype(BF16) for h in heads] for u in subs]
    ov = [[_dot(vt[u][h // 2], e[u][h]) for h in heads] for u in subs]
    for u in subs:
        den = [ov[u][h][LANES:LANES + 1, :] for h in heads]
        for pr in pairs:
            o_t = jnp.concatenate([(ov[u][2 * pr][:LANES] / den[2 * pr])[:HEAD_DIM],
                                   (ov[u][2 * pr + 1][:LANES] / den[2 * pr + 1])[HEAD_DIM:]], axis=0)
            o_ref[u * blk:(u + 1) * blk, lanes_of(pr)] = o_t.T.astype(o_ref.dtype)
        lse_t = jnp.concatenate([m[u][h] + jnp.log2(den[h]) for h in heads]
                                + [jnp.zeros((LANES - DIL_HEADS, blk), F32)], axis=0)
        lse_ref[u * blk:(u + 1) * blk, :] = lse_t.T


def _dilated_branch(q, k, v, band, dilation, batch):
    d = dilation
    w = DIL_WIDTH
    n_sub = min(DIL_SUB, q.shape[0] // batch // DIL_BLOCK)
    rows = n_sub * DIL_BLOCK
    nb = q.shape[0] // batch // rows
    cur = pl.BlockSpec((rows, w), lambda bi, r, j: (bi * nb + j, r))
    prev = pl.BlockSpec((DIL_BLOCK, w), lambda bi, r, j: ((bi * nb + jnp.maximum(j, 1)) * n_sub - 1, r))
    lse_spec = pl.BlockSpec((rows, LANES), lambda bi, r, j: (bi * nb + j, r))
    return pl.pallas_call(
        _dilated_kernel,
        grid=(batch, d, nb),
        in_specs=[cur, prev, cur, prev, cur, pl.BlockSpec(band.shape, lambda bi, r, j: (0, 0))],
        out_specs=[cur, lse_spec],
        out_shape=[jax.ShapeDtypeStruct(q.shape, BF16),
                   jax.ShapeDtypeStruct((q.shape[0], d * LANES), F32)],
        compiler_params=_cparams(("arbitrary",) * 3),
        name=f"dilated_attn_d{d}",
    )(q, k, k, v, v, band)


def _odd_out_kernel(o1_ref, o2_ref, o3_ref, l1_ref, l2_ref, l3_ref, x_ref, wo_ref, g_ref,
                    rw_ref, ex_ref, tril_ref, x1_ref, h_ref, info_ref, cnt_ref, carry_ref, slab_ref):
    i = pl.program_id(0)
    tm = x_ref.shape[0]

    @pl.when(i == 0)
    def _():
        carry_ref[...] = jnp.zeros_like(carry_ref)

    dils = [d for (_, d) in DIL_PATTERNS]
    l1, l2, l3 = [_read_dilated_view(r, d, tm, slab_ref) for r, d in zip((l1_ref, l2_ref, l3_ref), dils)]
    mx = jnp.maximum(jnp.maximum(l1, l2), l3)
    a1, a2, a3 = jnp.exp2(l1 - mx), jnp.exp2(l2 - mx), jnp.exp2(l3 - mx)
    inv = 1.0 / (a1 + a2 + a3)
    ex = ex_ref[...]
    o = None
    for a, o_ref, d in zip((a1, a2, a3), (o1_ref, o2_ref, o3_ref), dils):
        term = _dot_exact_rhs(a * inv, ex, 1) * _read_dilated_view(o_ref, d, tm, slab_ref)
        o = term if o is None else o + term
    x1 = x_ref[...] + _dot(o.astype(BF16), wo_ref[...])
    x1_ref[...] = x1
    h = _rms(x1, g_ref[...])
    _wide_to_rows(h_ref, h)

    h_hi, h_lo = _split_bf16(h, 2)
    w_hi, w_lo = _split_bf16(rw_ref[...], 2)
    logits = _dot(h_hi, w_hi) + (_dot(h_lo, w_hi) + _dot(h_hi, w_lo))
    lane = lax.broadcasted_iota(jnp.int32, logits.shape, 1)
    logits = jnp.where(lane < N_EXPERTS, logits, -jnp.inf)
    m1 = jnp.max(logits, axis=-1, keepdims=True)
    i1 = jnp.min(jnp.where(logits == m1, lane, LANES), axis=-1, keepdims=True)
    rest = jnp.where(lane == i1, -jnp.inf, logits)
    m2 = jnp.max(rest, axis=-1, keepdims=True)
    i2 = jnp.min(jnp.where(rest == m2, lane, LANES), axis=-1, keepdims=True)
    e2 = jnp.exp(m2 - m1)
    g1 = 1.0 / (1.0 + e2)
    g2 = e2 * g1
    hot1 = lane == i1
    hot2 = lane == i2
    onehot = jnp.where(hot1 | hot2, 1.0, 0.0).astype(BF16)
    before = _dot(tril_ref[...], onehot) + carry_ref[...]
    r1 = jnp.sum(jnp.where(hot1, before, 0.0), axis=-1, keepdims=True)
    r2 = jnp.sum(jnp.where(hot2, before, 0.0), axis=-1, keepdims=True)
    total = before[-1:, :] + onehot[-1:, :].astype(F32)
    carry_ref[...] = total
    cnt_ref[...] = jnp.broadcast_to(total, cnt_ref.shape)
    info = jnp.where(lane == 0, i1.astype(F32), 0.0)
    info = jnp.where(lane == 1, i2.astype(F32), info)
    info = jnp.where(lane == 2, g1, info)
    info = jnp.where(lane == 3, g2, info)
    info = jnp.where(lane == 4, r1, info)
    info = jnp.where(lane == 5, r2, info)
    info_ref[...] = info


def _odd_out_router(o1, o2, o3, l1, l2, l3, x2d, wo, g, rw, ex, tril_strict):
    t_rows = x2d.shape[0]
    tm = min(ROW_TILE, t_rows)
    n = t_rows // tm
    row = lambda w: pl.BlockSpec((tm, w), lambda i: (i, 0))
    full = lambda a: pl.BlockSpec(a.shape, lambda i: (0,) * a.ndim)
    view = lambda w: [pl.BlockSpec((tm // d, d * w), lambda i: (i, 0)) for (_, d) in DIL_PATTERNS]
    return pl.pallas_call(
        _odd_out_kernel,
        grid=(n,),
        in_specs=view(DIL_WIDTH) + view(LANES) + [row(D_MODEL), full(wo), full(g), full(rw),
                                                  full(ex), full(tril_strict)],
        out_specs=[row(D_MODEL), pl.BlockSpec((tm * TILE_SUBLANES, LANES), lambda i: (i, 0)), row(LANES),
                   pl.BlockSpec((8, LANES), lambda i: (0, 0))],
        out_shape=[jax.ShapeDtypeStruct((t_rows, D_MODEL), F32),
                   jax.ShapeDtypeStruct((t_rows * TILE_SUBLANES, LANES), F32),
                   jax.ShapeDtypeStruct((t_rows, LANES), F32), jax.ShapeDtypeStruct((8, LANES), F32)],
        scratch_shapes=[pltpu.VMEM((1, LANES), F32), pltpu.VMEM((tm * (DIL_WIDTH // LANES), LANES), F32)],
        compiler_params=_cparams(("arbitrary",)),
        name="odd_out_router",
    )(o1, o2, o3, l1, l2, l3, x2d, wo, g, rw, ex, tril_strict)


def _rows_to_wide(ref):
    n = ref.shape[0] // TILE_SUBLANES
    return jnp.concatenate([ref[pl.ds(c, n, stride=TILE_SUBLANES), :] for c in range(TILE_SUBLANES)], axis=1)


def _wide_to_rows(ref, val):
    n = ref.shape[0] // TILE_SUBLANES
    for c in range(TILE_SUBLANES):
        ref[pl.ds(c, n, stride=TILE_SUBLANES), :] = val[:, c * LANES:(c + 1) * LANES]


def _moe_kernel(te_ref, nu_ref, src_cur_ref, src_nxt_ref, dst_prv_ref, dst_cur_ref, h_hbm, w13_ref, w2_ref,
                y_hbm, xa, xb, ya, yb, act_ref, gsem, ssem):
    i = pl.program_id(0)
    n_used = nu_ref[0]
    ts = TILE_SUBLANES
    gm = xa.shape[0] // ts
    n_chunks = w2_ref.shape[0] // MOE_CHUNK
    per_chunk = gm // n_chunks
    head_rows = gm - n_chunks * per_chunk

    def tile_of(ref, row):
        return ref.at[pl.ds(pl.multiple_of(row * ts, ts), ts)]

    def gather_row(src_ref, r, xdst, sem):
        pltpu.make_async_copy(tile_of(h_hbm, src_ref[0, 0, r]), tile_of(xdst, r), sem).start()

    def scatter_row(dst_ref, r, ysrc, sem):
        pltpu.make_async_copy(tile_of(ysrc, r), tile_of(y_hbm, dst_ref[0, 0, r]), sem).start()

    def wait_rows(buf, sem):
        pltpu.make_async_copy(h_hbm.at[pl.ds(0, gm * ts)], buf, sem).wait()

    @pl.when(i == 0)
    def _():
        yb[...] = jnp.zeros_like(yb)

        def body(r, carry):
            gather_row(src_cur_ref, r, xa, gsem.at[0])
            return carry
        lax.fori_loop(0, gm, body, 0)

    def tile(x_cur, x_nxt, y_cur, y_prv, p):
        wait_rows(x_cur, gsem.at[p])

        @pl.when(i >= 1)
        def _():
            wait_rows(y_cur, ssem.at[p])

        def issue(r):
            gather_row(src_nxt_ref, r, x_nxt, gsem.at[1 - p])
            scatter_row(dst_prv_ref, r, y_prv, ssem.at[1 - p])

        for r in range(head_rows):
            issue(n_chunks * per_chunk + r)
        x = _rows_to_wide(x_cur).astype(BF16)

        def body(c, carry):
            for u in range(MOE_UNROLL):
                _swiglu_hidden_chunk(x, w13_ref, act_ref, c * MOE_UNROLL + u, MOE_CHUNK)
            for j in range(per_chunk * MOE_UNROLL):
                issue(c * (per_chunk * MOE_UNROLL) + j)
            return carry

        lax.fori_loop(0, n_chunks // MOE_UNROLL, body, 0)
        _wide_to_rows(y_cur, _dot(act_ref[...], w2_ref[...]))

        @pl.when(i == n_used - 1)
        def _():
            def body(r, carry):
                scatter_row(dst_cur_ref, r, y_cur, ssem.at[p])
                return carry
            lax.fori_loop(0, gm, body, 0)
            wait_rows(y_prv, ssem.at[1 - p])
            wait_rows(y_cur, ssem.at[p])
            wait_rows(x_nxt, gsem.at[1 - p])

    @pl.when((i < n_used) & (i % 2 == 0))
    def _():
        tile(xa, xb, ya, yb, 0)

    @pl.when((i < n_used) & (i % 2 == 1))
    def _():
        tile(xb, xa, yb, ya, 1)


def _moe_ffn(h_tiles, w13, w2, layer, tile_expert, n_used, src, dst, out_rows):
    nt = tile_expert.shape[0]
    gm = src.shape[-1]
    tab = lambda f: pl.BlockSpec((1, 1, gm), lambda i, te, nu: (f(i), 0, 0), memory_space=pltpu.SMEM)
    grid_spec = pltpu.PrefetchScalarGridSpec(
        num_scalar_prefetch=2,
        grid=(nt,),
        in_specs=[
            tab(lambda i: i),
            tab(lambda i: jnp.minimum(i + 1, nt - 1)),
            tab(lambda i: i),
            tab(lambda i: i + 1),
            pl.BlockSpec(memory_space=pl.ANY),
            pl.BlockSpec((None, None) + w13.shape[2:], lambda i, te, nu: (layer, te[i], 0, 0),
                         pipeline_mode=pl.Buffered(1)),
            pl.BlockSpec((None, None) + w2.shape[2:], lambda i, te, nu: (layer, te[i], 0, 0),
                         pipeline_mode=pl.Buffered(1)),
        ],
        out_specs=pl.BlockSpec(memory_space=pl.ANY),
        scratch_shapes=[pltpu.VMEM((gm * TILE_SUBLANES, LANES), F32)] * 4
        + [pltpu.VMEM((gm, w2.shape[2]), BF16), pltpu.SemaphoreType.DMA((2,)), pltpu.SemaphoreType.DMA((2,))],
    )
    return pl.pallas_call(
        _moe_kernel,
        grid_spec=grid_spec,
        out_shape=jax.ShapeDtypeStruct((out_rows * TILE_SUBLANES, LANES), F32),
        compiler_params=_cparams(("arbitrary",)),
        name="moe_ffn",
    )(tile_expert, n_used, src, src, dst, dst, h_tiles, w13, w2)


def _combine_kernel(x_ref, y0_ref, y1_ref, info_ref, o_ref):
    o_ref[...] = (x_ref[...] + info_ref[:, 2:3] * _rows_to_wide(y0_ref)
                  + info_ref[:, 3:4] * _rows_to_wide(y1_ref))


def _combine(x1, y, info):
    t_rows = x1.shape[0]
    tm = min(ROW_TILE, t_rows)
    row = lambda w: pl.BlockSpec((tm, w), lambda i: (i, 0))
    ysp = lambda kk: pl.BlockSpec((None, tm * TILE_SUBLANES, LANES), lambda i: (kk, i, 0))
    return pl.pallas_call(
        _combine_kernel,
        grid=(t_rows // tm,),
        in_specs=[row(D_MODEL), ysp(0), ysp(1), row(LANES)],
        out_specs=row(D_MODEL),
        out_shape=jax.ShapeDtypeStruct((t_rows, D_MODEL), F32),
        compiler_params=_cparams(("arbitrary",)),
        name="moe_combine",
    )(x1, y, y, info)


def _head_mean_matrix(width):
    idx = np.arange(width) // HEAD_DIM
    return jnp.asarray((idx[:, None] == idx[None, :]).astype(np.float32) / HEAD_DIM, dtype=BF16)


def _tril(n, strict):
    r = np.arange(n)
    m = (r[None, :] < r[:, None]) if strict else (r[None, :] <= r[:, None])
    return jnp.asarray(m.astype(np.float32), dtype=BF16)


def _rope_tables(seq):
    half = HEAD_DIM // 2
    inv_freq = ROPE_THETA ** (-jnp.arange(half, dtype=F32) / half)
    ang = jnp.arange(seq, dtype=F32)[:, None] * inv_freq[None, :]
    cos = jnp.cos(ang)
    sin = jnp.sin(ang)
    cos_t = jnp.concatenate([cos, cos, cos, cos], axis=1)
    sin_t = jnp.concatenate([-sin, sin, -sin, sin], axis=1)
    return cos_t, sin_t


def _band_bias():
    qi = np.arange(DIL_BLOCK)[None, :]
    ki = np.arange(-DIL_BLOCK, DIL_BLOCK)[:, None]
    rel = qi - ki
    return jnp.asarray(np.where((rel >= 0) & (rel <= DIL_BLOCK), 0.0, NEG_BIG).astype(np.float32))


def _head_expand():
    m = np.zeros((LANES, DIL_WIDTH), np.float32)
    for h in range(DIL_HEADS):
        m[h, h * HEAD_DIM:(h + 1) * HEAD_DIM] = 1.0
    return jnp.asarray(m, dtype=BF16)


def _bias_placement():
    pq = np.zeros((BIAS_PARTS, LANES, FOX_WIDTH), np.float32)
    pk = np.zeros((BIAS_PARTS, LANES, FOX_WIDTH), np.float32)
    qone = np.zeros((1, FOX_WIDTH), np.float32)
    kone = np.zeros((1, FOX_WIDTH), np.float32)
    for head in range(FOX_HEADS):
        base = (head // 2) * LANES + (head % 2) * BIAS_LANES
        for j in range(BIAS_PARTS):
            pq[j, head, base + j] = 1.0
            pk[j, head, base + BIAS_PARTS + j] = -1.0
            qone[0, base + BIAS_PARTS + j] = 1.0
            kone[0, base + j] = 1.0
    return (jnp.asarray(pq, dtype=BF16), jnp.asarray(pk, dtype=BF16), jnp.asarray(qone), jnp.asarray(kone))


def _even_layer(x2d, batch, seq, norm1, w_in, fox_qn, fox_kn, fox_fbias, conv_w, conv_b, dt_bias, a_log,
                d_skip, ssd_norm, w_out, norm2, ffn_w13, ffn_w2):
    w = FOX_WIDTH
    o_z = 3 * w + FOX_HEADS
    o_x = o_z + SSD_INNER
    o_dt = o_x + SSD_CONV_DIM
    wqk = w_in[:, :2 * w].astype(BF16)
    wvt = w_in[:, 2 * w:3 * w].T.astype(BF16)
    wzx = w_in[:, o_z:o_dt].astype(BF16)
    pad = LANES - FOX_HEADS - SSD_HEADS
    wfd = jnp.concatenate([w_in[:, 3 * w:o_z], w_in[:, o_dt:], jnp.zeros((D_MODEL, pad), F32)], axis=1).astype(BF16)
    fdb = jnp.concatenate([fox_fbias, dt_bias, jnp.zeros((pad,), F32)])[None, :]
    tm = min(ROW_TILE, seq)
    outs = _even_in_proj(
        x2d, seq, norm1[None, :], wqk, wvt, wzx, wfd, fox_qn.reshape(1, w), fox_kn.reshape(1, w), fdb,
        _head_mean_matrix(w), _tril(tm, False), *_bias_placement())
    q, qb, k, kb, vt, z, xbc, fd = outs[:8]
    if isinstance(x2d, tuple):
        x2d = outs[8]

    o_fox = _fox_attention(q, qb, k, kb, vt, batch, seq)

    dtt = jnp.transpose(fd[:, FOX_HEADS:FOX_HEADS + SSD_HEADS].reshape(batch, seq, SSD_HEADS), (0, 2, 1))
    d_skip_c = jnp.repeat(d_skip, HEAD_DIM)[None, :]
    y = _ssd(xbc, fd, dtt, z, conv_w, conv_b[None, :], a_log, d_skip_c, ssd_norm[None, :],
             _tril(SSD_CHUNK, False), batch, seq)

    return _even_out_ffn(o_fox, y, x2d, w_out.astype(BF16), norm2[None, :],
                         ffn_w13.astype(BF16), ffn_w2.astype(BF16))


def _moe_tables(info, counts, t_rows, gm):
    e1 = info[:, 0].astype(jnp.int32)
    e2 = info[:, 1].astype(jnp.int32)
    r1 = info[:, 4].astype(jnp.int32)
    r2 = info[:, 5].astype(jnp.int32)
    cnt = counts[0, :N_EXPERTS].astype(jnp.int32)
    tiles = (cnt + gm - 1) // gm
    tile_end = jnp.cumsum(tiles)
    offs = (tile_end - tiles) * gm
    nt = (2 * t_rows) // gm + N_EXPERTS
    tile_idx = jnp.arange(nt, dtype=jnp.int32)
    tile_expert = jnp.minimum(jnp.sum((tile_idx[:, None] >= tile_end[None, :]).astype(jnp.int32), axis=1),
                              N_EXPERTS - 1)
    n_used = tile_end[-1:].astype(jnp.int32)
    slots = jnp.concatenate([offs[e1] + r1, offs[e2] + r2])
    codes = jnp.full((nt * gm,), -1, jnp.int32).at[slots].set(
        jnp.arange(2 * t_rows, dtype=jnp.int32), unique_indices=True)
    out_half = t_rows + N_EXPERTS * gm
    second = codes >= t_rows
    src = jnp.where(codes < 0, 0, jnp.where(second, codes - t_rows, codes))
    spare = t_rows + jnp.repeat(tile_expert, gm) * gm + jnp.tile(jnp.arange(gm, dtype=jnp.int32), nt)
    dst = jnp.where(codes < 0, spare, jnp.where(second, codes - t_rows + out_half, codes))
    dst = jnp.concatenate([t_rows + jnp.arange(gm, dtype=jnp.int32), dst])
    return tile_expert, n_used, src.reshape(nt, 1, gm), dst.reshape(nt + 1, 1, gm), out_half


def _odd_layer(x2d, batch, seq, norm1, w_qkv, qn, kn, w_out, norm2, router, moe_w13, moe_w2, layer):
    t_rows = batch * seq
    cos, sin = _rope_tables(seq)
    qs, ks, vs = _odd_qkv(x2d, seq, norm1[None, :], w_qkv.astype(BF16), qn.reshape(1, DIL_WIDTH),
                          kn.reshape(1, DIL_WIDTH), _head_expand(), cos, sin)
    band = _band_bias()
    outs, lses = zip(*[_dilated_branch(q, k, v, band, d, batch)
                       for q, k, v, (_, d) in zip(qs, ks, vs, DIL_PATTERNS)])
    rw = jnp.concatenate([router, jnp.zeros((D_MODEL, LANES - N_EXPERTS), F32)], axis=1)
    tm = min(ROW_TILE, t_rows)
    x1, h, info, counts = _odd_out_router(*outs, *lses, x2d, w_out.astype(BF16), norm2[None, :], rw,
                                          _head_expand(), _tril(tm, True))
    gm = min(MOE_TILE, t_rows)
    tile_expert, n_used, src, dst, out_half = _moe_tables(info, counts, t_rows, gm)
    y = _moe_ffn(h, moe_w13, moe_w2, layer, tile_expert, n_used, src, dst, 2 * out_half)
    return x1, y.reshape(2, out_half * TILE_SUBLANES, LANES), info


def kernel(x, e_norm1, e_w_in, e_fox_qn, e_fox_kn, e_fox_fbias, e_conv_w, e_conv_b, e_dt_bias, e_a_log, e_d_skip, e_ssd_norm, e_w_out, e_norm2, e_ffn_w13, e_ffn_w2, o_norm1, o_w_qkv, o_qn, o_kn, o_w_out, o_norm2, o_router, o_moe_w13, o_moe_w2):
    batch, seq, _ = x.shape
    depth = e_norm1.shape[0] + o_norm1.shape[0]
    x2d = x.reshape(batch * seq, D_MODEL)
    moe_w13 = o_moe_w13.astype(BF16)
    moe_w2 = o_moe_w2.astype(BF16)
    for i in range(depth):
        j = i // 2
        if i % 2 == 0:
            x2d = _even_layer(x2d, batch, seq, e_norm1[j], e_w_in[j], e_fox_qn[j], e_fox_kn[j], e_fox_fbias[j],
                              e_conv_w[j], e_conv_b[j], e_dt_bias[j], e_a_log[j], e_d_skip[j], e_ssd_norm[j],
                              e_w_out[j], e_norm2[j], e_ffn_w13[j], e_ffn_w2[j])
        else:
            x2d = _odd_layer(x2d, batch, seq, o_norm1[j], o_w_qkv[j], o_qn[j], o_kn[j], o_w_out[j], o_norm2[j],
                             o_router[j], moe_w13, moe_w2, j)
            if i == depth - 1:
                x2d = _combine(*x2d)
    return x2d.reshape(batch, seq, D_MODEL)
```

```python
import functools
import math

import jax
import jax.numpy as jnp
import numpy as np
from jax import lax
from jax.experimental import pallas as pl
from jax.experimental.pallas import tpu as pltpu

F32 = jnp.float32
BF16 = jnp.bfloat16

D_MODEL = 1024
HEAD_DIM = 64
RMS_EPS = 1e-6
ROPE_THETA = 10000.0
FOX_HEADS = 8
FOX_WIDTH = FOX_HEADS * HEAD_DIM
SSD_HEADS = 8
SSD_INNER = 512
SSD_GROUPS = 2
SSD_STATE = 128
SSD_CONV = 4
SSD_CHUNK = 128
SSD_CONV_DIM = SSD_INNER + 2 * SSD_GROUPS * SSD_STATE
DIL_HEADS = 16
DIL_WIDTH = DIL_HEADS * HEAD_DIM
DIL_PATTERNS = ((128, 1), (512, 4), (2048, 16))
DIL_BLOCK = 128
DIL_SUB = 4
FFN_DIM = 2816
N_EXPERTS = 8
EXPERT_DIM = 3584

LANES = 128
TILE_SUBLANES = D_MODEL // LANES
NEG_BIG = -1e30
LOG2E = math.log2(math.e)
DEN_ROWS = 16
BIAS_PARTS = 3
BIAS_LANES = 2 * BIAS_PARTS
VMEM_LIMIT = 56 * 1024 * 1024

ROW_TILE = 512
ATTN_TILE = 512
FOX_PAIRS_PER_STEP = 4
FFN_CHUNK = 256
MOE_CHUNK = 256
MOE_UNROLL = 14
MOE_TILE = 512


def _cparams(sem):
    return pltpu.CompilerParams(dimension_semantics=sem, vmem_limit_bytes=VMEM_LIMIT)


def _softplus_parts(x):
    return jnp.log(1.0 + jnp.exp(-jnp.abs(x)))


def _split_bf16(a, parts):
    out = []
    r = a
    for _ in range(parts):
        p = r.astype(BF16)
        out.append(p)
        r = r - p.astype(F32)
    return out


def _dot(a, b):
    return jnp.dot(a, b, preferred_element_type=F32)


def _dot_nt(a, b):
    return lax.dot_general(a, b, (((1,), (1,)), ((), ())), preferred_element_type=F32)


def _dot_exact_rhs(a_f32, b_bf16, parts):
    acc = None
    for p in _split_bf16(a_f32, parts):
        t = _dot(p, b_bf16)
        acc = t if acc is None else acc + t
    return acc


def _dot_exact_lhs(a_bf16, b_f32, parts):
    acc = None
    for p in _split_bf16(b_f32, parts):
        t = _dot(a_bf16, p)
        acc = t if acc is None else acc + t
    return acc


def _rms(x, g):
    return x * lax.rsqrt(jnp.mean(x * x, axis=-1, keepdims=True) + RMS_EPS) * g


def _head_norm(x, hm, gain):
    ms = _dot((x * x).astype(BF16), hm)
    return x * lax.rsqrt(ms + RMS_EPS) * gain


def _silu(x):
    return x * (1.0 / (1.0 + jnp.exp(-x)))


def _even_in_kernel(tiles_per_seq, after_moe, *refs):
    n_lead = 4 if after_moe else 1
    lead, refs = refs[:n_lead], refs[n_lead:]
    (g_ref, wqk_ref, wvt_ref, wzx_ref, wfd_ref, qg_ref, kg_ref, fdb_ref, hm_ref, tril_ref, pq_ref, pk_ref,
     qone_ref, kone_ref, q_ref, qb_ref, k_ref, kb_ref, vt_ref, z_ref, xbc_ref, fd_ref) = refs[:22]
    carry_ref = refs[-1]
    i = pl.program_id(0)

    @pl.when(i % tiles_per_seq == 0)
    def _():
        carry_ref[...] = jnp.zeros_like(carry_ref)

    if after_moe:
        x1_ref, y0_ref, y1_ref, info_ref = lead
        x = (x1_ref[...] + info_ref[:, 2:3] * _rows_to_wide(y0_ref)
             + info_ref[:, 3:4] * _rows_to_wide(y1_ref))
        refs[22][...] = x
    else:
        x = lead[0][...]
    h = _rms(x, g_ref[...]).astype(BF16)
    qk = _dot(h, wqk_ref[...])
    hm = hm_ref[...]
    w = FOX_WIDTH
    q_ref[...] = (_head_norm(qk[:, :w], hm, qg_ref[...]) * (HEAD_DIM ** -0.5 * LOG2E)).astype(BF16)
    k_ref[...] = _head_norm(qk[:, w:], hm, kg_ref[...]).astype(BF16)
    vt_ref[...] = _dot_nt(wvt_ref[...], h).astype(BF16)
    zx = _dot(h, wzx_ref[...])
    z_ref[...] = zx[:, :SSD_INNER].astype(BF16)
    xbc_ref[...] = zx[:, SSD_INNER:]
    fd = _dot(h, wfd_ref[...]) + fdb_ref[...]
    t = _softplus_parts(fd)
    log_f = jnp.minimum(fd, 0.0) - t
    dt = jnp.maximum(fd, 0.0) + t
    c = _dot_exact_lhs(tril_ref[...], log_f, 2) + carry_ref[...]
    carry_ref[...] = c[-1:, :]
    lane = lax.broadcasted_iota(jnp.int32, fd.shape, 1)
    fd_ref[...] = jnp.where(lane < FOX_HEADS, c, dt)
    qb = qone_ref[...]
    kb = kone_ref[...]
    for j, part in enumerate(_split_bf16(c * LOG2E, 3)):
        qb = qb + _dot(part, pq_ref[j])
        kb = kb + _dot(part, pk_ref[j])
    qb_ref[...] = qb.astype(BF16)
    kb_ref[...] = kb.astype(BF16)


def _even_in_proj(stream, seq, g, wqk, wvt, wzx, wfd, qg, kg, fdb, hm, tril, pq, pk, qone, kone):
    after_moe = isinstance(stream, tuple)
    t_rows = stream[0].shape[0] if after_moe else stream.shape[0]
    tm = min(ROW_TILE, seq)
    n = t_rows // tm
    row = lambda w: pl.BlockSpec((tm, w), lambda i: (i, 0))
    full = lambda a: pl.BlockSpec(a.shape, lambda i: (0,) * a.ndim)
    consts = (g, wqk, wvt, wzx, wfd, qg, kg, fdb, hm, tril, pq, pk, qone, kone)
    lead, lead_specs = (stream,), [row(D_MODEL)]
    out_specs = [row(FOX_WIDTH)] * 4 + [pl.BlockSpec((FOX_WIDTH, tm), lambda i: (0, i)),
                                        row(SSD_INNER), row(SSD_CONV_DIM), row(LANES)]
    out_shape = ([jax.ShapeDtypeStruct((t_rows, FOX_WIDTH), BF16)] * 4
                 + [jax.ShapeDtypeStruct((FOX_WIDTH, t_rows), BF16),
                    jax.ShapeDtypeStruct((t_rows, SSD_INNER), BF16),
                    jax.ShapeDtypeStruct((t_rows, SSD_CONV_DIM), F32),
                    jax.ShapeDtypeStruct((t_rows, LANES), F32)])
    if after_moe:
        x1, y, info = stream
        ysp = lambda kk: pl.BlockSpec((None, tm * TILE_SUBLANES, LANES), lambda i: (kk, i, 0))
        lead, lead_specs = (x1, y, y, info), [row(D_MODEL), ysp(0), ysp(1), row(LANES)]
        out_specs = out_specs + [row(D_MODEL)]
        out_shape = out_shape + [jax.ShapeDtypeStruct((t_rows, D_MODEL), F32)]
    return pl.pallas_call(
        functools.partial(_even_in_kernel, seq // tm, after_moe),
        grid=(n,),
        in_specs=lead_specs + [full(a) for a in consts],
        out_specs=out_specs,
        out_shape=out_shape,
        scratch_shapes=[pltpu.VMEM((1, LANES), F32)],
        compiler_params=_cparams(("arbitrary",)),
        name="even_in_proj",
    )(*lead, *consts)


def _fox_kernel(qi_ref, ki_ref, q_ref, qb_ref, k_ref, kb_ref, vt_ref, o_ref, qs_ref, m_ref, l_ref, acc_ref):
    step_idx = pl.program_id(2)
    qi = qi_ref[step_idx]
    ki = ki_ref[step_idx]
    tq = q_ref.shape[0]
    tk = k_ref.shape[0]
    heads = range(2 * FOX_PAIRS_PER_STEP)
    lanes_of = lambda h: slice((h // 2) * LANES, (h // 2 + 1) * LANES)

    @pl.when(ki == 0)
    def _():
        m_ref[...] = jnp.full_like(m_ref, NEG_BIG)
        l_ref[...] = jnp.zeros_like(l_ref)
        acc_ref[...] = jnp.zeros_like(acc_ref)
        lane = lax.broadcasted_iota(jnp.int32, (tq, LANES), 1)
        zero = jnp.zeros((tq, LANES), q_ref.dtype)
        for h in heads:
            hh = h % 2
            qh = jnp.where((lane >= hh * HEAD_DIM) & (lane < (hh + 1) * HEAD_DIM), q_ref[:, lanes_of(h)], zero)
            bh = jnp.where((lane >= hh * BIAS_LANES) & (lane < (hh + 1) * BIAS_LANES), qb_ref[:, lanes_of(h)], zero)
            qs_ref[h] = jnp.concatenate([qh, bh], axis=1)

    def step(masked):
        if masked:
            keep = (lax.broadcasted_iota(jnp.int32, (tk, tq), 0)
                    <= lax.broadcasted_iota(jnp.int32, (tk, tq), 1))
        ka = [jnp.concatenate([k_ref[:, lanes_of(2 * p)], kb_ref[:, lanes_of(2 * p)]], axis=1)
              for p in range(FOX_PAIRS_PER_STEP)]
        scores = [_dot_nt(ka[h // 2], qs_ref[h]) for h in heads]
        if masked:
            scores = [jnp.where(keep, s, NEG_BIG) for s in scores]
        m_prev = [m_ref[h] for h in heads]
        m_new = [jnp.maximum(m_prev[h], jnp.max(scores[h], axis=0, keepdims=True)) for h in heads]
        probs = [jnp.exp2(scores[h] - m_new[h]) for h in heads]
        alpha = [jnp.exp2(m_prev[h] - m_new[h]) for h in heads]
        pv = [_dot(vt_ref[lanes_of(h), :], probs[h].astype(BF16)) for h in heads]
        for h in heads:
            l_ref[h] = alpha[h] * l_ref[h] + jnp.sum(probs[h], axis=0, keepdims=True)
            acc_ref[h] = alpha[h] * acc_ref[h] + pv[h]
            m_ref[h] = m_new[h]

    @pl.when(ki < qi)
    def _():
        step(False)

    @pl.when(ki == qi)
    def _():
        step(True)
        for p in range(FOX_PAIRS_PER_STEP):
            o0 = acc_ref[2 * p] / l_ref[2 * p]
            o1 = acc_ref[2 * p + 1] / l_ref[2 * p + 1]
            o_t = jnp.concatenate([o0[:HEAD_DIM], o1[HEAD_DIM:]], axis=0)
            o_ref[:, lanes_of(2 * p)] = o_t.T.astype(o_ref.dtype)


def _fox_attention(q, qb, k, kb, vt, batch, seq):
    t = min(ATTN_TILE, seq)
    n = seq // t
    w = FOX_PAIRS_PER_STEP * LANES
    groups = FOX_WIDTH // w
    tri = [(i, j) for i in range(n) for j in range(i + 1)]
    qi_tab = jnp.asarray([i for i, _ in tri], jnp.int32)
    ki_tab = jnp.asarray([j for _, j in tri], jnp.int32)
    qspec = pl.BlockSpec((t, w), lambda bi, g, s, qi, ki: (bi * n + qi[s], g))
    kspec = pl.BlockSpec((t, w), lambda bi, g, s, qi, ki: (bi * n + ki[s], g))
    vspec = pl.BlockSpec((w, t), lambda bi, g, s, qi, ki: (g, bi * n + ki[s]))
    heads = 2 * FOX_PAIRS_PER_STEP
    grid_spec = pltpu.PrefetchScalarGridSpec(
        num_scalar_prefetch=2,
        grid=(batch, groups, len(tri)),
        in_specs=[qspec, qspec, kspec, kspec, vspec],
        out_specs=qspec,
        scratch_shapes=[pltpu.VMEM((heads, t, 2 * LANES), BF16), pltpu.VMEM((heads, 1, t), F32),
                        pltpu.VMEM((heads, 1, t), F32), pltpu.VMEM((heads, LANES, t), F32)],
    )
    return pl.pallas_call(
        _fox_kernel,
        grid_spec=grid_spec,
        out_shape=jax.ShapeDtypeStruct((batch * seq, FOX_WIDTH), BF16),
        compiler_params=_cparams(("arbitrary",) * 3),
        name="fox_attention",
    )(qi_tab, ki_tab, q, qb, k, kb, vt)


def _ssd_kernel(xbc_ref, prev_ref, fd_ref, dtt_ref, z_ref, cw_ref, cb_ref, alog_ref, alogc_ref,
                dsk_ref, nrm_ref, tri_ref, y_ref, state_ref):
    c_idx = pl.program_id(1)
    L = SSD_CHUNK
    P = HEAD_DIM
    N = SSD_STATE

    @pl.when(c_idx == 0)
    def _():
        state_ref[...] = jnp.zeros_like(state_ref)

    cur = xbc_ref[...]
    prev = jnp.where(c_idx > 0, prev_ref[...], 0.0)
    ext = jnp.concatenate([prev, cur], axis=0)
    cw = cw_ref[...]
    conv = cb_ref[...] + cw[SSD_CONV - 1:SSD_CONV, :] * cur
    for kk in range(SSD_CONV - 1):
        sh = SSD_CONV - 1 - kk
        conv = conv + cw[kk:kk + 1, :] * ext[8 - sh:8 - sh + L, :]
    xc = _silu(conv)
    xs = xc[:, :SSD_INNER]
    bm = xc[:, SSD_INNER:SSD_INNER + SSD_GROUPS * N]
    cm = xc[:, SSD_INNER + SSD_GROUPS * N:]

    dt = fd_ref[:, FOX_HEADS:FOX_HEADS + SSD_HEADS]
    a_row = -jnp.exp(alog_ref[...])
    a_col = -jnp.exp(alogc_ref[...])
    tri = tri_ref[...]
    cum = _dot_exact_lhs(tri, dt * a_row, 2)
    cum_t = lax.dot_general(_split_bf16(dtt_ref[...] * a_col, 2)[0], tri, (((1,), (1,)), ((), ())),
                            preferred_element_type=F32)
    cum_t = cum_t + lax.dot_general(_split_bf16(dtt_ref[...] * a_col, 2)[1], tri,
                                    (((1,), (1,)), ((), ())), preferred_element_type=F32)
    row_i = lax.broadcasted_iota(jnp.int32, (L, L), 0)
    col_i = lax.broadcasted_iota(jnp.int32, (L, L), 1)
    causal = row_i >= col_i

    groups = range(SSD_GROUPS)
    heads = range(SSD_HEADS)
    grp = lambda h: h // (SSD_HEADS // SSD_GROUPS)
    bm_g = [bm[:, g * N:(g + 1) * N] for g in groups]
    cm_g = [cm[:, g * N:(g + 1) * N] for g in groups]
    cb_g = [_dot_nt(cm_g[g].astype(BF16), bm_g[g].astype(BF16)) for g in groups]
    bm_t = [bm_g[g].T for g in groups]
    cum_c = [cum[:, h:h + 1] for h in heads]
    cum_r = [cum_t[h:h + 1, :] for h in heads]
    last = [cum_t[h:h + 1, L - 1:L] for h in heads]
    seg = [jnp.where(causal, jnp.exp(jnp.minimum(cum_c[h] - cum_r[h], 0.0)), 0.0) for h in heads]
    xdt = [(xs[:, h * P:(h + 1) * P] * dt[:, h:h + 1]).astype(BF16) for h in heads]
    h_prev = [state_ref[h] for h in heads]
    y_diag = [_dot((cb_g[grp(h)] * seg[h]).astype(BF16), xdt[h]) for h in heads]
    y_off = [_dot((cm_g[grp(h)] * jnp.exp(cum_c[h])).astype(BF16), h_prev[h].astype(BF16)) for h in heads]
    st = [_dot((bm_t[grp(h)] * jnp.exp(last[h] - cum_r[h])).astype(BF16), xdt[h]) for h in heads]
    for h in heads:
        state_ref[h] = jnp.exp(last[h]) * h_prev[h] + st[h]
    y = jnp.concatenate([y_diag[h] + y_off[h] for h in heads], axis=1) + xs * dsk_ref[...]
    y = y * _silu(z_ref[...].astype(F32))
    gw = SSD_INNER // SSD_GROUPS
    outs = []
    for g in range(SSD_GROUPS):
        yg = y[:, g * gw:(g + 1) * gw]
        outs.append(yg * lax.rsqrt(jnp.mean(yg * yg, axis=-1, keepdims=True) + RMS_EPS))
    y_ref[...] = (jnp.concatenate(outs, axis=1) * nrm_ref[...]).astype(y_ref.dtype)


def _ssd(xbc, fd, dtt, z, conv_w, conv_b, a_log, d_skip_c, ssd_norm, tri, batch, seq):
    L = SSD_CHUNK
    nc = seq // L
    t_rows = batch * seq
    row = lambda w: pl.BlockSpec((L, w), lambda b, c: (b * nc + c, 0))
    full = lambda a: pl.BlockSpec(a.shape, lambda b, c: (0,) * a.ndim)
    prev = pl.BlockSpec((8, SSD_CONV_DIM), lambda b, c: (jnp.maximum((b * nc + c) * (L // 8) - 1, 0), 0))
    return pl.pallas_call(
        _ssd_kernel,
        grid=(batch, nc),
        in_specs=[row(SSD_CONV_DIM), prev, row(LANES),
                  pl.BlockSpec((None, SSD_HEADS, L), lambda b, c: (b, 0, c)),
                  row(SSD_INNER), full(conv_w), full(conv_b), full(a_log[None, :]),
                  full(a_log[:, None]), full(d_skip_c), full(ssd_norm), full(tri)],
        out_specs=row(SSD_INNER),
        out_shape=jax.ShapeDtypeStruct((t_rows, SSD_INNER), BF16),
        scratch_shapes=[pltpu.VMEM((SSD_HEADS, SSD_STATE, HEAD_DIM), F32)],
        compiler_params=_cparams(("arbitrary", "arbitrary")),
        name="ssd_scan",
    )(xbc, xbc, fd, dtt, z, conv_w, conv_b, a_log[None, :], a_log[:, None], d_skip_c, ssd_norm, tri)


def _swiglu_hidden_chunk(h, w13_ref, act_ref, c, tf):
    f = w13_ref.shape[1] // 2
    off = c * tf if isinstance(c, int) else pl.multiple_of(c * tf, tf)
    gate = _dot(h, w13_ref[:, pl.ds(off, tf)])
    up = _dot(h, w13_ref[:, pl.ds(f + off, tf)])
    act_ref[:, pl.ds(off, tf)] = (_silu(gate) * up).astype(BF16)


def _even_out_kernel(o_ref, y_ref, x_ref, wo_ref, g_ref, w13_ref, w2_ref, out_ref, act_ref):
    w = FOX_WIDTH
    x1 = x_ref[...] + _dot(o_ref[...], wo_ref[:w, :]) + _dot(y_ref[...], wo_ref[w:, :])
    h = _rms(x1, g_ref[...]).astype(BF16)

    for c in range(w2_ref.shape[0] // FFN_CHUNK):
        _swiglu_hidden_chunk(h, w13_ref, act_ref, c, FFN_CHUNK)
    out_ref[...] = x1 + _dot(act_ref[...], w2_ref[...])


def _even_out_ffn(o_fox, y_ssd, x2d, wo, g, w13, w2):
    t_rows = x2d.shape[0]
    tm = min(ROW_TILE, t_rows)
    row = lambda w: pl.BlockSpec((tm, w), lambda i: (i, 0))
    full = lambda a: pl.BlockSpec(a.shape, lambda i: (0,) * a.ndim, pipeline_mode=pl.Buffered(1))
    return pl.pallas_call(
        _even_out_kernel,
        grid=(t_rows // tm,),
        in_specs=[row(FOX_WIDTH), row(SSD_INNER), row(D_MODEL), full(wo), full(g), full(w13), full(w2)],
        out_specs=row(D_MODEL),
        out_shape=jax.ShapeDtypeStruct((t_rows, D_MODEL), F32),
        scratch_shapes=[pltpu.VMEM((tm, w2.shape[0]), BF16)],
        compiler_params=_cparams(("arbitrary",)),
        name="even_out_ffn",
    )(o_fox, y_ssd, x2d, wo, g, w13, w2)


def _rope(x, cos, sin_signed):
    lane = lax.broadcasted_iota(jnp.int32, x.shape, 1)
    first = (lane % HEAD_DIM) < (HEAD_DIM // 2)
    rot = jnp.where(first, pltpu.roll(x, LANES - HEAD_DIM // 2, 1), pltpu.roll(x, HEAD_DIM // 2, 1))
    return x * cos + rot * sin_signed


def _emit_dilated_chunk(val, c, width, refs, slab_ref):
    tm = val.shape[0]
    slab_ref[c * tm:(c + 1) * tm, :] = val
    for (_, d), ref in zip(DIL_PATTERNS, refs):
        if d == 1:
            ref[:, c * LANES:(c + 1) * LANES] = val.astype(ref.dtype)
            continue
        n = tm // d
        for r in range(d):
            piece = slab_ref[pl.ds(c * tm + r, n, stride=d), :]
            ref[:, r * width + c * LANES:r * width + (c + 1) * LANES] = piece.astype(ref.dtype)


def _read_dilated_view(ref, d, tm, slab_ref):
    if d == 1:
        return ref[...].astype(F32)
    width = ref.shape[1] // d
    chunks = width // LANES
    n = tm // d
    for r in range(d):
        for c in range(chunks):
            piece = ref[:, r * width + c * LANES:r * width + (c + 1) * LANES].astype(F32)
            slab_ref[pl.ds(c * tm + r, n, stride=d), :] = piece
    return jnp.concatenate([slab_ref[c * tm:(c + 1) * tm, :] for c in range(chunks)], axis=1)


def _odd_qkv_kernel(x_ref, g_ref, w_ref, qg_ref, kg_ref, hm_ref, cos_ref, sin_ref, *rest):
    n_pat = len(DIL_PATTERNS)
    q_refs, k_refs, v_refs = rest[:n_pat], rest[n_pat:2 * n_pat], rest[2 * n_pat:3 * n_pat]
    slabs = rest[3 * n_pat:]
    h = _rms(x_ref[...], g_ref[...]).astype(BF16)
    expand = hm_ref[...]
    w = DIL_WIDTH
    cos = cos_ref[...]
    sin = sin_ref[...]

    def normed_rotated(col, gain_ref, scale, refs, slab_ref):
        x = _dot(h, w_ref[:, col:col + w])
        sums = _dot_nt((x * x).astype(BF16), expand) * (1.0 / HEAD_DIM)
        inv = lax.rsqrt(_dot_exact_rhs(sums, expand, 2) + RMS_EPS)
        for c in range(w // LANES):
            sl = slice(c * LANES, (c + 1) * LANES)
            xc = x[:, sl] * inv[:, sl] * gain_ref[:, sl]
            _emit_dilated_chunk(_rope(xc, cos, sin) * scale, c, w, refs, slab_ref)

    normed_rotated(0, qg_ref, HEAD_DIM ** -0.5 * LOG2E, q_refs, slabs[0])
    normed_rotated(w, kg_ref, 1.0, k_refs, slabs[1])
    v = _dot(h, w_ref[:, 2 * w:])
    for c in range(w // LANES):
        _emit_dilated_chunk(v[:, c * LANES:(c + 1) * LANES], c, w, v_refs, slabs[2])


def _odd_qkv(x2d, seq, g, w, qg, kg, hm, cos, sin):
    t_rows = x2d.shape[0]
    tm = min(ROW_TILE, seq)
    per_seq = seq // tm
    row = lambda rows, wd: pl.BlockSpec((rows, wd), lambda i: (i, 0))
    full = lambda a: pl.BlockSpec(a.shape, lambda i: (0,) * a.ndim)
    tab = pl.BlockSpec((tm, LANES), lambda i: (i % per_seq, 0))
    views = [(tm // d, t_rows // d, d * DIL_WIDTH) for (_, d) in DIL_PATTERNS]
    outs = pl.pallas_call(
        _odd_qkv_kernel,
        grid=(t_rows // tm,),
        in_specs=[row(tm, D_MODEL), full(g), full(w), full(qg), full(kg), full(hm), tab, tab],
        out_specs=[row(r, wd) for (r, _, wd) in views] * 3,
        out_shape=[jax.ShapeDtypeStruct((n, wd), BF16) for (_, n, wd) in views] * 3,
        scratch_shapes=[pltpu.VMEM((tm * (DIL_WIDTH // LANES), LANES), F32)] * 3,
        compiler_params=_cparams(("arbitrary",)),
        name="odd_qkv",
    )(x2d, g, w, qg, kg, hm, cos, sin)
    n_pat = len(DIL_PATTERNS)
    return outs[:n_pat], outs[n_pat:2 * n_pat], outs[2 * n_pat:]


def _dilated_kernel(q_ref, kp_ref, kc_ref, vp_ref, vc_ref, band_ref, o_ref, lse_ref):
    j = pl.program_id(2)
    blk = DIL_BLOCK
    n_sub = q_ref.shape[0] // blk
    key_row = lax.broadcasted_iota(jnp.int32, (2 * blk, blk), 0)
    band = band_ref[...]
    bias = [jnp.where((j == 0) & (key_row < blk), NEG_BIG, band)] + [band] * (n_sub - 1)
    lane = lax.broadcasted_iota(jnp.int32, (blk, LANES), 1)
    ones_rows = jnp.ones((DEN_ROWS, 2 * blk), BF16)
    pairs = range(DIL_HEADS // 2)
    heads = range(DIL_HEADS)
    subs = range(n_sub)
    lanes_of = lambda pr: slice(pr * LANES, (pr + 1) * LANES)

    def keys_of(prev_ref, cur_ref, u, pr):
        if u == 0:
            return jnp.concatenate([prev_ref[:, lanes_of(pr)], cur_ref[:blk, lanes_of(pr)]], axis=0)
        return cur_ref[(u - 1) * blk:(u + 1) * blk, lanes_of(pr)]

    kk = [[keys_of(kp_ref, kc_ref, u, pr) for pr in pairs] for u in subs]
    vt = [[jnp.concatenate([keys_of(vp_ref, vc_ref, u, pr).astype(F32).T.astype(BF16), ones_rows], axis=0)
           for pr in pairs] for u in subs]
    zero = jnp.zeros((blk, LANES), q_ref.dtype)
    qm = [[jnp.where((lane >= (h % 2) * HEAD_DIM) & (lane < (h % 2 + 1) * HEAD_DIM),
                     q_ref[u * blk:(u + 1) * blk, lanes_of(h // 2)], zero) for h in heads] for u in subs]
    s = [[_dot_nt(kk[u][h // 2], qm[u][h]) + bias[u] for h in heads] for u in subs]
    m = [[jnp.max(s[u][h], axis=0, keepdims=True) for h in heads] for u in subs]
    e = [[jnp.exp2(s[u][h] - m[u][h]).astype(BF16) for h in heads] for u in subs]
    ov = [[_dot(vt[u][h // 2], e[u][h]) for h in heads] for u in subs]
    for u in subs:
        den = [ov[u][h][LANES:LANES + 1, :] for h in heads]
        for pr in pairs:
            o_t = jnp.concatenate([(ov[u][2 * pr][:LANES] / den[2 * pr])[:HEAD_DIM],
                                   (ov[u][2 * pr + 1][:LANES] / den[2 * pr + 1])[HEAD_DIM:]], axis=0)
            o_ref[u * blk:(u + 1) * blk, lanes_of(pr)] = o_t.T.astype(o_ref.dtype)
        lse_t = jnp.concatenate([m[u][h] + jnp.log2(den[h]) for h in heads]
                                + [jnp.zeros((LANES - DIL_HEADS, blk), F32)], axis=0)
        lse_ref[u * blk:(u + 1) * blk, :] = lse_t.T


def _dilated_branch(q, k, v, band, dilation, batch):
    d = dilation
    w = DIL_WIDTH
    n_sub = min(DIL_SUB, q.shape[0] // batch // DIL_BLOCK)
    rows = n_sub * DIL_BLOCK
    nb = q.shape[0] // batch // rows
    cur = pl.BlockSpec((rows, w), lambda bi, r, j: (bi * nb + j, r))
    prev = pl.BlockSpec((DIL_BLOCK, w), lambda bi, r, j: ((bi * nb + jnp.maximum(j, 1)) * n_sub - 1, r))
    lse_spec = pl.BlockSpec((rows, LANES), lambda bi, r, j: (bi * nb + j, r))
    return pl.pallas_call(
        _dilated_kernel,
        grid=(batch, d, nb),
        in_specs=[cur, prev, cur, prev, cur, pl.BlockSpec(band.shape, lambda bi, r, j: (0, 0))],
        out_specs=[cur, lse_spec],
        out_shape=[jax.ShapeDtypeStruct(q.shape, BF16),
                   jax.ShapeDtypeStruct((q.shape[0], d * LANES), F32)],
        compiler_params=_cparams(("arbitrary",) * 3),
        name=f"dilated_attn_d{d}",
    )(q, k, k, v, v, band)


def _odd_out_kernel(o1_ref, o2_ref, o3_ref, l1_ref, l2_ref, l3_ref, x_ref, wo_ref, g_ref,
                    rw_ref, ex_ref, tril_ref, x1_ref, h_ref, info_ref, cnt_ref, carry_ref, slab_ref):
    i = pl.program_id(0)
    tm = x_ref.shape[0]

    @pl.when(i == 0)
    def _():
        carry_ref[...] = jnp.zeros_like(carry_ref)

    dils = [d for (_, d) in DIL_PATTERNS]
    l1, l2, l3 = [_read_dilated_view(r, d, tm, slab_ref) for r, d in zip((l1_ref, l2_ref, l3_ref), dils)]
    mx = jnp.maximum(jnp.maximum(l1, l2), l3)
    a1, a2, a3 = jnp.exp2(l1 - mx), jnp.exp2(l2 - mx), jnp.exp2(l3 - mx)
    inv = 1.0 / (a1 + a2 + a3)
    ex = ex_ref[...]
    o = None
    for a, o_ref, d in zip((a1, a2, a3), (o1_ref, o2_ref, o3_ref), dils):
        term = _dot_exact_rhs(a * inv, ex, 1) * _read_dilated_view(o_ref, d, tm, slab_ref)
        o = term if o is None else o + term
    x1 = x_ref[...] + _dot(o.astype(BF16), wo_ref[...])
    x1_ref[...] = x1
    h = _rms(x1, g_ref[...])
    _wide_to_rows(h_ref, h)

    h_hi, h_lo = _split_bf16(h, 2)
    w_hi, w_lo = _split_bf16(rw_ref[...], 2)
    logits = _dot(h_hi, w_hi) + (_dot(h_lo, w_hi) + _dot(h_hi, w_lo))
    lane = lax.broadcasted_iota(jnp.int32, logits.shape, 1)
    logits = jnp.where(lane < N_EXPERTS, logits, -jnp.inf)
    m1 = jnp.max(logits, axis=-1, keepdims=True)
    i1 = jnp.min(jnp.where(logits == m1, lane, LANES), axis=-1, keepdims=True)
    rest = jnp.where(lane == i1, -jnp.inf, logits)
    m2 = jnp.max(rest, axis=-1, keepdims=True)
    i2 = jnp.min(jnp.where(rest == m2, lane, LANES), axis=-1, keepdims=True)
    e2 = jnp.exp(m2 - m1)
    g1 = 1.0 / (1.0 + e2)
    g2 = e2 * g1
    hot1 = lane == i1
    hot2 = lane == i2
    onehot = jnp.where(hot1 | hot2, 1.0, 0.0).astype(BF16)
    before = _dot(tril_ref[...], onehot) + carry_ref[...]
    r1 = jnp.sum(jnp.where(hot1, before, 0.0), axis=-1, keepdims=True)
    r2 = jnp.sum(jnp.where(hot2, before, 0.0), axis=-1, keepdims=True)
    total = before[-1:, :] + onehot[-1:, :].astype(F32)
    carry_ref[...] = total
    cnt_ref[...] = jnp.broadcast_to(total, cnt_ref.shape)
    info = jnp.where(lane == 0, i1.astype(F32), 0.0)
    info = jnp.where(lane == 1, i2.astype(F32), info)
    info = jnp.where(lane == 2, g1, info)
    info = jnp.where(lane == 3, g2, info)
    info = jnp.where(lane == 4, r1, info)
    info = jnp.where(lane == 5, r2, info)
    info_ref[...] = info


def _odd_out_router(o1, o2, o3, l1, l2, l3, x2d, wo, g, rw, ex, tril_strict):
    t_rows = x2d.shape[0]
    tm = min(ROW_TILE, t_rows)
    n = t_rows // tm
    row = lambda w: pl.BlockSpec((tm, w), lambda i: (i, 0))
    full = lambda a: pl.BlockSpec(a.shape, lambda i: (0,) * a.ndim)
    view = lambda w: [pl.BlockSpec((tm // d, d * w), lambda i: (i, 0)) for (_, d) in DIL_PATTERNS]
    return pl.pallas_call(
        _odd_out_kernel,
        grid=(n,),
        in_specs=view(DIL_WIDTH) + view(LANES) + [row(D_MODEL), full(wo), full(g), full(rw),
                                                  full(ex), full(tril_strict)],
        out_specs=[row(D_MODEL), pl.BlockSpec((tm * TILE_SUBLANES, LANES), lambda i: (i, 0)), row(LANES),
                   pl.BlockSpec((8, LANES), lambda i: (0, 0))],
        out_shape=[jax.ShapeDtypeStruct((t_rows, D_MODEL), F32),
                   jax.ShapeDtypeStruct((t_rows * TILE_SUBLANES, LANES), F32),
                   jax.ShapeDtypeStruct((t_rows, LANES), F32), jax.ShapeDtypeStruct((8, LANES), F32)],
        scratch_shapes=[pltpu.VMEM((1, LANES), F32), pltpu.VMEM((tm * (DIL_WIDTH // LANES), LANES), F32)],
        compiler_params=_cparams(("arbitrary",)),
        name="odd_out_router",
    )(o1, o2, o3, l1, l2, l3, x2d, wo, g, rw, ex, tril_strict)


def _rows_to_wide(ref):
    n = ref.shape[0] // TILE_SUBLANES
    return jnp.concatenate([ref[pl.ds(c, n, stride=TILE_SUBLANES), :] for c in range(TILE_SUBLANES)], axis=1)


def _wide_to_rows(ref, val):
    n = ref.shape[0] // TILE_SUBLANES
    for c in range(TILE_SUBLANES):
        ref[pl.ds(c, n, stride=TILE_SUBLANES), :] = val[:, c * LANES:(c + 1) * LANES]


def _moe_kernel(te_ref, nu_ref, src_cur_ref, src_nxt_ref, dst_prv_ref, dst_cur_ref, h_hbm, w13_ref, w2_ref,
                y_hbm, xa, xb, ya, yb, act_ref, gsem, ssem):
    i = pl.program_id(0)
    n_used = nu_ref[0]
    ts = TILE_SUBLANES
    gm = xa.shape[0] // ts
    n_chunks = w2_ref.shape[0] // MOE_CHUNK
    per_chunk = gm // n_chunks
    head_rows = gm - n_chunks * per_chunk

    def tile_of(ref, row):
        return ref.at[pl.ds(pl.multiple_of(row * ts, ts), ts)]

    def gather_row(src_ref, r, xdst, sem):
        pltpu.make_async_copy(tile_of(h_hbm, src_ref[0, 0, r]), tile_of(xdst, r), sem).start()

    def scatter_row(dst_ref, r, ysrc, sem):
        pltpu.make_async_copy(tile_of(ysrc, r), tile_of(y_hbm, dst_ref[0, 0, r]), sem).start()

    def wait_rows(buf, sem):
        pltpu.make_async_copy(h_hbm.at[pl.ds(0, gm * ts)], buf, sem).wait()

    @pl.when(i == 0)
    def _():
        yb[...] = jnp.zeros_like(yb)
        out_half = y_hbm.shape[0] // ts // 2
        spare = [y_hbm.at[pl.ds((k * out_half + (out_half - N_EXPERTS * gm) + e * gm) * ts, gm * ts)]
                 for k in range(2) for e in range(N_EXPERTS)]
        for blk in spare:
            pltpu.make_async_copy(yb, blk, ssem.at[1]).start()
        for blk in spare:
            pltpu.make_async_copy(yb, blk, ssem.at[1]).wait()

        def body(r, carry):
            gather_row(src_cur_ref, r, xa, gsem.at[0])
            return carry
        lax.fori_loop(0, gm, body, 0)

    def tile(x_cur, x_nxt, y_cur, y_prv, p):
        wait_rows(x_cur, gsem.at[p])

        @pl.when(i >= 1)
        def _():
            wait_rows(y_cur, ssem.at[p])

        def issue(r):
            gather_row(src_nxt_ref, r, x_nxt, gsem.at[1 - p])
            scatter_row(dst_prv_ref, r, y_prv, ssem.at[1 - p])

        for r in range(head_rows):
            issue(n_chunks * per_chunk + r)
        x = _rows_to_wide(x_cur).astype(BF16)

        def body(c, carry):
            for u in range(MOE_UNROLL):
                _swiglu_hidden_chunk(x, w13_ref, act_ref, c * MOE_UNROLL + u, MOE_CHUNK)
            for j in range(per_chunk * MOE_UNROLL):
                issue(c * (per_chunk * MOE_UNROLL) + j)
            return carry

        lax.fori_loop(0, n_chunks // MOE_UNROLL, body, 0)
        _wide_to_rows(y_cur, _dot(act_ref[...], w2_ref[...]))

        @pl.when(i == n_used - 1)
        def _():
            def body(r, carry):
                scatter_row(dst_cur_ref, r, y_cur, ssem.at[p])
                return carry
            lax.fori_loop(0, gm, body, 0)
            wait_rows(y_prv, ssem.at[1 - p])
            wait_rows(y_cur, ssem.at[p])
            wait_rows(x_nxt, gsem.at[1 - p])

    @pl.when((i < n_used) & (i % 2 == 0))
    def _():
        tile(xa, xb, ya, yb, 0)

    @pl.when((i < n_used) & (i % 2 == 1))
    def _():
        tile(xb, xa, yb, ya, 1)


def _moe_ffn(h_tiles, w13, w2, layer, tile_expert, n_used, src, dst, out_rows):
    nt = tile_expert.shape[0]
    gm = src.shape[-1]
    tab = lambda f: pl.BlockSpec((1, 1, gm), lambda i, te, nu: (f(i), 0, 0), memory_space=pltpu.SMEM)
    grid_spec = pltpu.PrefetchScalarGridSpec(
        num_scalar_prefetch=2,
        grid=(nt,),
        in_specs=[
            tab(lambda i: i),
            tab(lambda i: jnp.minimum(i + 1, nt - 1)),
            tab(lambda i: i),
            tab(lambda i: i + 1),
            pl.BlockSpec(memory_space=pl.ANY),
            pl.BlockSpec((None, None) + w13.shape[2:], lambda i, te, nu: (layer, te[i], 0, 0),
                         pipeline_mode=pl.Buffered(1)),
            pl.BlockSpec((None, None) + w2.shape[2:], lambda i, te, nu: (layer, te[i], 0, 0),
                         pipeline_mode=pl.Buffered(1)),
        ],
        out_specs=pl.BlockSpec(memory_space=pl.ANY),
        scratch_shapes=[pltpu.VMEM((gm * TILE_SUBLANES, LANES), F32)] * 4
        + [pltpu.VMEM((gm, w2.shape[2]), BF16), pltpu.SemaphoreType.DMA((2,)), pltpu.SemaphoreType.DMA((2,))],
    )
    return pl.pallas_call(
        _moe_kernel,
        grid_spec=grid_spec,
        out_shape=jax.ShapeDtypeStruct((out_rows * TILE_SUBLANES, LANES), F32),
        compiler_params=_cparams(("arbitrary",)),
        name="moe_ffn",
    )(tile_expert, n_used, src, src, dst, dst, h_tiles, w13, w2)


def _combine_kernel(x_ref, y0_ref, y1_ref, info_ref, o_ref):
    o_ref[...] = (x_ref[...] + info_ref[:, 2:3] * _rows_to_wide(y0_ref)
                  + info_ref[:, 3:4] * _rows_to_wide(y1_ref))


def _combine(x1, y, info):
    t_rows = x1.shape[0]
    tm = min(ROW_TILE, t_rows)
    row = lambda w: pl.BlockSpec((tm, w), lambda i: (i, 0))
    ysp = lambda kk: pl.BlockSpec((None, tm * TILE_SUBLANES, LANES), lambda i: (kk, i, 0))
    return pl.pallas_call(
        _combine_kernel,
        grid=(t_rows // tm,),
        in_specs=[row(D_MODEL), ysp(0), ysp(1), row(LANES)],
        out_specs=row(D_MODEL),
        out_shape=jax.ShapeDtypeStruct((t_rows, D_MODEL), F32),
        compiler_params=_cparams(("arbitrary",)),
        name="moe_combine",
    )(x1, y, y, info)


def _head_mean_matrix(width):
    idx = np.arange(width) // HEAD_DIM
    return jnp.asarray((idx[:, None] == idx[None, :]).astype(np.float32) / HEAD_DIM, dtype=BF16)


def _tril(n, strict):
    r = np.arange(n)
    m = (r[None, :] < r[:, None]) if strict else (r[None, :] <= r[:, None])
    return jnp.asarray(m.astype(np.float32), dtype=BF16)


def _rope_tables(seq):
    half = HEAD_DIM // 2
    inv_freq = ROPE_THETA ** (-jnp.arange(half, dtype=F32) / half)
    ang = jnp.arange(seq, dtype=F32)[:, None] * inv_freq[None, :]
    cos = jnp.cos(ang)
    sin = jnp.sin(ang)
    cos_t = jnp.concatenate([cos, cos, cos, cos], axis=1)
    sin_t = jnp.concatenate([-sin, sin, -sin, sin], axis=1)
    return cos_t, sin_t


def _band_bias():
    qi = np.arange(DIL_BLOCK)[None, :]
    ki = np.arange(-DIL_BLOCK, DIL_BLOCK)[:, None]
    rel = qi - ki
    return jnp.asarray(np.where((rel >= 0) & (rel <= DIL_BLOCK), 0.0, NEG_BIG).astype(np.float32))


def _head_expand():
    m = np.zeros((LANES, DIL_WIDTH), np.float32)
    for h in range(DIL_HEADS):
        m[h, h * HEAD_DIM:(h + 1) * HEAD_DIM] = 1.0
    return jnp.asarray(m, dtype=BF16)


def _bias_placement():
    pq = np.zeros((BIAS_PARTS, LANES, FOX_WIDTH), np.float32)
    pk = np.zeros((BIAS_PARTS, LANES, FOX_WIDTH), np.float32)
    qone = np.zeros((1, FOX_WIDTH), np.float32)
    kone = np.zeros((1, FOX_WIDTH), np.float32)
    for head in range(FOX_HEADS):
        base = (head // 2) * LANES + (head % 2) * BIAS_LANES
        for j in range(BIAS_PARTS):
            pq[j, head, base + j] = 1.0
            pk[j, head, base + BIAS_PARTS + j] = -1.0
            qone[0, base + BIAS_PARTS + j] = 1.0
            kone[0, base + j] = 1.0
    return (jnp.asarray(pq, dtype=BF16), jnp.asarray(pk, dtype=BF16), jnp.asarray(qone), jnp.asarray(kone))


def _even_layer(x2d, batch, seq, norm1, w_in, fox_qn, fox_kn, fox_fbias, conv_w, conv_b, dt_bias, a_log,
                d_skip, ssd_norm, w_out, norm2, ffn_w13, ffn_w2):
    w = FOX_WIDTH
    o_z = 3 * w + FOX_HEADS
    o_x = o_z + SSD_INNER
    o_dt = o_x + SSD_CONV_DIM
    wqk = w_in[:, :2 * w].astype(BF16)
    wvt = w_in[:, 2 * w:3 * w].T.astype(BF16)
    wzx = w_in[:, o_z:o_dt].astype(BF16)
    pad = LANES - FOX_HEADS - SSD_HEADS
    wfd = jnp.concatenate([w_in[:, 3 * w:o_z], w_in[:, o_dt:], jnp.zeros((D_MODEL, pad), F32)], axis=1).astype(BF16)
    fdb = jnp.concatenate([fox_fbias, dt_bias, jnp.zeros((pad,), F32)])[None, :]
    tm = min(ROW_TILE, seq)
    outs = _even_in_proj(
        x2d, seq, norm1[None, :], wqk, wvt, wzx, wfd, fox_qn.reshape(1, w), fox_kn.reshape(1, w), fdb,
        _head_mean_matrix(w), _tril(tm, False), *_bias_placement())
    q, qb, k, kb, vt, z, xbc, fd = outs[:8]
    if isinstance(x2d, tuple):
        x2d = outs[8]

    o_fox = _fox_attention(q, qb, k, kb, vt, batch, seq)

    dtt = jnp.transpose(fd[:, FOX_HEADS:FOX_HEADS + SSD_HEADS].reshape(batch, seq, SSD_HEADS), (0, 2, 1))
    d_skip_c = jnp.repeat(d_skip, HEAD_DIM)[None, :]
    y = _ssd(xbc, fd, dtt, z, conv_w, conv_b[None, :], a_log, d_skip_c, ssd_norm[None, :],
             _tril(SSD_CHUNK, False), batch, seq)

    return _even_out_ffn(o_fox, y, x2d, w_out.astype(BF16), norm2[None, :],
                         ffn_w13.astype(BF16), ffn_w2.astype(BF16))


def _moe_tables(info, counts, t_rows, gm):
    e1 = info[:, 0].astype(jnp.int32)
    e2 = info[:, 1].astype(jnp.int32)
    r1 = info[:, 4].astype(jnp.int32)
    r2 = info[:, 5].astype(jnp.int32)
    cnt = counts[0, :N_EXPERTS].astype(jnp.int32)
    tiles = (cnt + gm - 1) // gm
    tile_end = jnp.cumsum(tiles)
    offs = (tile_end - tiles) * gm
    nt = (2 * t_rows) // gm + N_EXPERTS
    tile_idx = jnp.arange(nt, dtype=jnp.int32)
    tile_expert = jnp.minimum(jnp.sum((tile_idx[:, None] >= tile_end[None, :]).astype(jnp.int32), axis=1),
                              N_EXPERTS - 1)
    n_used = tile_end[-1:].astype(jnp.int32)
    slots = jnp.concatenate([offs[e1] + r1, offs[e2] + r2])
    codes = jnp.full((nt * gm,), -1, jnp.int32).at[slots].set(
        jnp.arange(2 * t_rows, dtype=jnp.int32), unique_indices=True)
    out_half = t_rows + N_EXPERTS * gm
    second = codes >= t_rows
    src = jnp.where(codes < 0, 0, jnp.where(second, codes - t_rows, codes))
    spare = t_rows + jnp.repeat(tile_expert, gm) * gm + jnp.tile(jnp.arange(gm, dtype=jnp.int32), nt)
    dst = jnp.where(codes < 0, spare, jnp.where(second, codes - t_rows + out_half, codes))
    dst = jnp.concatenate([t_rows + jnp.arange(gm, dtype=jnp.int32), dst])
    return tile_expert, n_used, src.reshape(nt, 1, gm), dst.reshape(nt + 1, 1, gm), out_half


def _odd_layer(x2d, batch, seq, norm1, w_qkv, qn, kn, w_out, norm2, router, moe_w13, moe_w2, layer):
    t_rows = batch * seq
    cos, sin = _rope_tables(seq)
    qs, ks, vs = _odd_qkv(x2d, seq, norm1[None, :], w_qkv.astype(BF16), qn.reshape(1, DIL_WIDTH),
                          kn.reshape(1, DIL_WIDTH), _head_expand(), cos, sin)
    band = _band_bias()
    outs, lses = zip(*[_dilated_branch(q, k, v, band, d, batch)
                       for q, k, v, (_, d) in zip(qs, ks, vs, DIL_PATTERNS)])
    rw = jnp.concatenate([router, jnp.zeros((D_MODEL, LANES - N_EXPERTS), F32)], axis=1)
    tm = min(ROW_TILE, t_rows)
    x1, h, info, counts = _odd_out_router(*outs, *lses, x2d, w_out.astype(BF16), norm2[None, :], rw,
                                          _head_expand(), _tril(tm, True))
    gm = min(MOE_TILE, t_rows)
    tile_expert, n_used, src, dst, out_half = _moe_tables(info, counts, t_rows, gm)
    y = _moe_ffn(h, moe_w13, moe_w2, layer, tile_expert, n_used, src, dst, 2 * out_half)
    return x1, y.reshape(2, out_half * TILE_SUBLANES, LANES), info


def kernel(x, e_norm1, e_w_in, e_fox_qn, e_fox_kn, e_fox_fbias, e_conv_w, e_conv_b, e_dt_bias, e_a_log, e_d_skip, e_ssd_norm, e_w_out, e_norm2, e_ffn_w13, e_ffn_w2, o_norm1, o_w_qkv, o_qn, o_kn, o_w_out, o_norm2, o_router, o_moe_w13, o_moe_w2):
    batch, seq, _ = x.shape
    depth = e_norm1.shape[0] + o_norm1.shape[0]
    x2d = x.reshape(batch * seq, D_MODEL)
    moe_w13 = o_moe_w13.astype(BF16)
    moe_w2 = o_moe_w2.astype(BF16)
    for i in range(depth):
        j = i // 2
        if i % 2 == 0:
            x2d = _even_layer(x2d, batch, seq, e_norm1[j], e_w_in[j], e_fox_qn[j], e_fox_kn[j], e_fox_fbias[j],
                              e_conv_w[j], e_conv_b[j], e_dt_bias[j], e_a_log[j], e_d_skip[j], e_ssd_norm[j],
                              e_w_out[j], e_norm2[j], e_ffn_w13[j], e_ffn_w2[j])
        else:
            x2d = _odd_layer(x2d, batch, seq, o_norm1[j], o_w_qkv[j], o_qn[j], o_kn[j], o_w_out[j], o_norm2[j],
                             o_router[j], moe_w13, moe_w2, j)
            if i == depth - 1:
                x2d = _combine(*x2d)
    return x2d.reshape(batch, seq, D_MODEL)
```

```python
import functools
import math

import jax
import jax.numpy as jnp
import numpy as np
from jax import lax
from jax.experimental import pallas as pl
from jax.experimental.pallas import tpu as pltpu

F32 = jnp.float32
BF16 = jnp.bfloat16

D_MODEL = 1024
HEAD_DIM = 64
RMS_EPS = 1e-6
ROPE_THETA = 10000.0
FOX_HEADS = 8
FOX_WIDTH = FOX_HEADS * HEAD_DIM
SSD_HEADS = 8
SSD_INNER = 512
SSD_GROUPS = 2
SSD_STATE = 128
SSD_CONV = 4
SSD_CHUNK = 128
SSD_CONV_DIM = SSD_INNER + 2 * SSD_GROUPS * SSD_STATE
DIL_HEADS = 16
DIL_WIDTH = DIL_HEADS * HEAD_DIM
DIL_PATTERNS = ((128, 1), (512, 4), (2048, 16))
DIL_BLOCK = 128
DIL_SUB = 4
FFN_DIM = 2816
N_EXPERTS = 8
EXPERT_DIM = 3584

LANES = 128
TILE_SUBLANES = D_MODEL // LANES
NEG_BIG = -1e30
LOG2E = math.log2(math.e)
DEN_ROWS = 16
BIAS_PARTS = 3
BIAS_LANES = 2 * BIAS_PARTS
VMEM_LIMIT = 56 * 1024 * 1024

ROW_TILE = 512
ATTN_TILE = 512
FOX_PAIRS_PER_STEP = 4
FFN_CHUNK = 256
MOE_CHUNK = 256
MOE_UNROLL = 14
MOE_TILE = 512


def _cparams(sem):
    return pltpu.CompilerParams(dimension_semantics=sem, vmem_limit_bytes=VMEM_LIMIT)


def _softplus_parts(x):
    return jnp.log(1.0 + jnp.exp(-jnp.abs(x)))


def _split_bf16(a, parts):
    out = []
    r = a
    for _ in range(parts):
        p = r.astype(BF16)
        out.append(p)
        r = r - p.astype(F32)
    return out


def _dot(a, b):
    return jnp.dot(a, b, preferred_element_type=F32)


def _dot_nt(a, b):
    return lax.dot_general(a, b, (((1,), (1,)), ((), ())), preferred_element_type=F32)


def _dot_exact_rhs(a_f32, b_bf16, parts):
    acc = None
    for p in _split_bf16(a_f32, parts):
        t = _dot(p, b_bf16)
        acc = t if acc is None else acc + t
    return acc


def _dot_exact_lhs(a_bf16, b_f32, parts):
    acc = None
    for p in _split_bf16(b_f32, parts):
        t = _dot(a_bf16, p)
        acc = t if acc is None else acc + t
    return acc


def _rms(x, g):
    return x * lax.rsqrt(jnp.mean(x * x, axis=-1, keepdims=True) + RMS_EPS) * g


def _head_norm(x, hm, gain):
    ms = _dot((x * x).astype(BF16), hm)
    return x * lax.rsqrt(ms + RMS_EPS) * gain


def _silu(x):
    return x * (1.0 / (1.0 + jnp.exp(-x)))


def _even_in_kernel(tiles_per_seq, after_moe, *refs):
    n_lead = 4 if after_moe else 1
    lead, refs = refs[:n_lead], refs[n_lead:]
    (g_ref, wqk_ref, wvt_ref, wzx_ref, wfd_ref, qg_ref, kg_ref, fdb_ref, hm_ref, tril_ref, pq_ref, pk_ref,
     qone_ref, kone_ref, q_ref, qb_ref, k_ref, kb_ref, vt_ref, z_ref, xbc_ref, fd_ref) = refs[:22]
    carry_ref = refs[-1]
    i = pl.program_id(0)

    @pl.when(i % tiles_per_seq == 0)
    def _():
        carry_ref[...] = jnp.zeros_like(carry_ref)

    if after_moe:
        x1_ref, y0_ref, y1_ref, info_ref = lead
        x = (x1_ref[...] + info_ref[:, 2:3] * _rows_to_wide(y0_ref)
             + info_ref[:, 3:4] * _rows_to_wide(y1_ref))
        refs[22][...] = x
    else:
        x = lead[0][...]
    h = _rms(x, g_ref[...]).astype(BF16)
    qk = _dot(h, wqk_ref[...])
    hm = hm_ref[...]
    w = FOX_WIDTH
    q_ref[...] = (_head_norm(qk[:, :w], hm, qg_ref[...]) * (HEAD_DIM ** -0.5 * LOG2E)).astype(BF16)
    k_ref[...] = _head_norm(qk[:, w:], hm, kg_ref[...]).astype(BF16)
    vt_ref[...] = _dot_nt(wvt_ref[...], h).astype(BF16)
    zx = _dot(h, wzx_ref[...])
    z_ref[...] = zx[:, :SSD_INNER].astype(BF16)
    xbc_ref[...] = zx[:, SSD_INNER:]
    fd = _dot(h, wfd_ref[...]) + fdb_ref[...]
    t = _softplus_parts(fd)
    log_f = jnp.minimum(fd, 0.0) - t
    dt = jnp.maximum(fd, 0.0) + t
    c = _dot_exact_lhs(tril_ref[...], log_f, 2) + carry_ref[...]
    carry_ref[...] = c[-1:, :]
    lane = lax.broadcasted_iota(jnp.int32, fd.shape, 1)
    fd_ref[...] = jnp.where(lane < FOX_HEADS, c, dt)
    qb = qone_ref[...]
    kb = kone_ref[...]
    for j, part in enumerate(_split_bf16(c * LOG2E, 3)):
        qb = qb + _dot(part, pq_ref[j])
        kb = kb + _dot(part, pk_ref[j])
    qb_ref[...] = qb.astype(BF16)
    kb_ref[...] = kb.astype(BF16)


def _even_in_proj(stream, seq, g, wqk, wvt, wzx, wfd, qg, kg, fdb, hm, tril, pq, pk, qone, kone):
    after_moe = isinstance(stream, tuple)
    t_rows = stream[0].shape[0] if after_moe else stream.shape[0]
    tm = min(ROW_TILE, seq)
    n = t_rows // tm
    row = lambda w: pl.BlockSpec((tm, w), lambda i: (i, 0))
    full = lambda a: pl.BlockSpec(a.shape, lambda i: (0,) * a.ndim)
    consts = (g, wqk, wvt, wzx, wfd, qg, kg, fdb, hm, tril, pq, pk, qone, kone)
    lead, lead_specs = (stream,), [row(D_MODEL)]
    out_specs = [row(FOX_WIDTH)] * 4 + [pl.BlockSpec((FOX_WIDTH, tm), lambda i: (0, i)),
                                        row(SSD_INNER), row(SSD_CONV_DIM), row(LANES)]
    out_shape = ([jax.ShapeDtypeStruct((t_rows, FOX_WIDTH), BF16)] * 4
                 + [jax.ShapeDtypeStruct((FOX_WIDTH, t_rows), BF16),
                    jax.ShapeDtypeStruct((t_rows, SSD_INNER), BF16),
                    jax.ShapeDtypeStruct((t_rows, SSD_CONV_DIM), F32),
                    jax.ShapeDtypeStruct((t_rows, LANES), F32)])
    if after_moe:
        x1, y, info = stream
        ysp = lambda kk: pl.BlockSpec((None, tm * TILE_SUBLANES, LANES), lambda i: (kk, i, 0))
        lead, lead_specs = (x1, y, y, info), [row(D_MODEL), ysp(0), ysp(1), row(LANES)]
        out_specs = out_specs + [row(D_MODEL)]
        out_shape = out_shape + [jax.ShapeDtypeStruct((t_rows, D_MODEL), F32)]
    return pl.pallas_call(
        functools.partial(_even_in_kernel, seq // tm, after_moe),
        grid=(n,),
        in_specs=lead_specs + [full(a) for a in consts],
        out_specs=out_specs,
        out_shape=out_shape,
        scratch_shapes=[pltpu.VMEM((1, LANES), F32)],
        compiler_params=_cparams(("arbitrary",)),
        name="even_in_proj",
    )(*lead, *consts)


def _fox_kernel(qi_ref, ki_ref, q_ref, qb_ref, k_ref, kb_ref, vt_ref, o_ref, qs_ref, m_ref, l_ref, acc_ref):
    step_idx = pl.program_id(2)
    qi = qi_ref[step_idx]
    ki = ki_ref[step_idx]
    tq = q_ref.shape[0]
    tk = k_ref.shape[0]
    heads = range(2 * FOX_PAIRS_PER_STEP)
    lanes_of = lambda h: slice((h // 2) * LANES, (h // 2 + 1) * LANES)

    @pl.when(ki == 0)
    def _():
        m_ref[...] = jnp.full_like(m_ref, NEG_BIG)
        l_ref[...] = jnp.zeros_like(l_ref)
        acc_ref[...] = jnp.zeros_like(acc_ref)
        lane = lax.broadcasted_iota(jnp.int32, (tq, LANES), 1)
        zero = jnp.zeros((tq, LANES), q_ref.dtype)
        for h in heads:
            hh = h % 2
            qh = jnp.where((lane >= hh * HEAD_DIM) & (lane < (hh + 1) * HEAD_DIM), q_ref[:, lanes_of(h)], zero)
            bh = jnp.where((lane >= hh * BIAS_LANES) & (lane < (hh + 1) * BIAS_LANES), qb_ref[:, lanes_of(h)], zero)
            qs_ref[h] = jnp.concatenate([qh, bh], axis=1)

    def step(masked):
        if masked:
            keep = (lax.broadcasted_iota(jnp.int32, (tk, tq), 0)
                    <= lax.broadcasted_iota(jnp.int32, (tk, tq), 1))
        ka = [jnp.concatenate([k_ref[:, lanes_of(2 * p)], kb_ref[:, lanes_of(2 * p)]], axis=1)
              for p in range(FOX_PAIRS_PER_STEP)]
        scores = [_dot_nt(ka[h // 2], qs_ref[h]) for h in heads]
        if masked:
            scores = [jnp.where(keep, s, NEG_BIG) for s in scores]
        m_prev = [m_ref[h] for h in heads]
        m_new = [jnp.maximum(m_prev[h], jnp.max(scores[h], axis=0, keepdims=True)) for h in heads]
        probs = [jnp.exp2(scores[h] - m_new[h]) for h in heads]
        alpha = [jnp.exp2(m_prev[h] - m_new[h]) for h in heads]
        pv = [_dot(vt_ref[lanes_of(h), :], probs[h].astype(BF16)) for h in heads]
        for h in heads:
            l_ref[h] = alpha[h] * l_ref[h] + jnp.sum(probs[h], axis=0, keepdims=True)
            acc_ref[h] = alpha[h] * acc_ref[h] + pv[h]
            m_ref[h] = m_new[h]

    @pl.when(ki < qi)
    def _():
        step(False)

    @pl.when(ki == qi)
    def _():
        step(True)
        for p in range(FOX_PAIRS_PER_STEP):
            o0 = acc_ref[2 * p] / l_ref[2 * p]
            o1 = acc_ref[2 * p + 1] / l_ref[2 * p + 1]
            o_t = jnp.concatenate([o0[:HEAD_DIM], o1[HEAD_DIM:]], axis=0)
            o_ref[:, lanes_of(2 * p)] = o_t.T.astype(o_ref.dtype)


def _fox_attention(q, qb, k, kb, vt, batch, seq):
    t = min(ATTN_TILE, seq)
    n = seq // t
    w = FOX_PAIRS_PER_STEP * LANES
    groups = FOX_WIDTH // w
    tri = [(i, j) for i in range(n) for j in range(i + 1)]
    qi_tab = jnp.asarray([i for i, _ in tri], jnp.int32)
    ki_tab = jnp.asarray([j for _, j in tri], jnp.int32)
    qspec = pl.BlockSpec((t, w), lambda bi, g, s, qi, ki: (bi * n + qi[s], g))
    kspec = pl.BlockSpec((t, w), lambda bi, g, s, qi, ki: (bi * n + ki[s], g))
    vspec = pl.BlockSpec((w, t), lambda bi, g, s, qi, ki: (g, bi * n + ki[s]))
    heads = 2 * FOX_PAIRS_PER_STEP
    grid_spec = pltpu.PrefetchScalarGridSpec(
        num_scalar_prefetch=2,
        grid=(batch, groups, len(tri)),
        in_specs=[qspec, qspec, kspec, kspec, vspec],
        out_specs=qspec,
        scratch_shapes=[pltpu.VMEM((heads, t, 2 * LANES), BF16), pltpu.VMEM((heads, 1, t), F32),
                        pltpu.VMEM((heads, 1, t), F32), pltpu.VMEM((heads, LANES, t), F32)],
    )
    return pl.pallas_call(
        _fox_kernel,
        grid_spec=grid_spec,
        out_shape=jax.ShapeDtypeStruct((batch * seq, FOX_WIDTH), BF16),
        compiler_params=_cparams(("arbitrary",) * 3),
        name="fox_attention",
    )(qi_tab, ki_tab, q, qb, k, kb, vt)


def _ssd_kernel(xbc_ref, prev_ref, fd_ref, dtt_ref, z_ref, cw_ref, cb_ref, alog_ref, alogc_ref,
                dsk_ref, nrm_ref, tri_ref, y_ref, state_ref):
    c_idx = pl.program_id(1)
    L = SSD_CHUNK
    P = HEAD_DIM
    N = SSD_STATE

    @pl.when(c_idx == 0)
    def _():
        state_ref[...] = jnp.zeros_like(state_ref)

    cur = xbc_ref[...]
    prev = jnp.where(c_idx > 0, prev_ref[...], 0.0)
    ext = jnp.concatenate([prev, cur], axis=0)
    cw = cw_ref[...]
    conv = cb_ref[...] + cw[SSD_CONV - 1:SSD_CONV, :] * cur
    for kk in range(SSD_CONV - 1):
        sh = SSD_CONV - 1 - kk
        conv = conv + cw[kk:kk + 1, :] * ext[8 - sh:8 - sh + L, :]
    xc = _silu(conv)
    xs = xc[:, :SSD_INNER]
    bm = xc[:, SSD_INNER:SSD_INNER + SSD_GROUPS * N]
    cm = xc[:, SSD_INNER + SSD_GROUPS * N:]

    dt = fd_ref[:, FOX_HEADS:FOX_HEADS + SSD_HEADS]
    a_row = -jnp.exp(alog_ref[...])
    a_col = -jnp.exp(alogc_ref[...])
    tri = tri_ref[...]
    cum = _dot_exact_lhs(tri, dt * a_row, 2)
    da_t = _split_bf16(dtt_ref[...] * a_col, 2)
    cum_t = _dot_nt(da_t[0], tri) + _dot_nt(da_t[1], tri)
    row_i = lax.broadcasted_iota(jnp.int32, (L, L), 0)
    col_i = lax.broadcasted_iota(jnp.int32, (L, L), 1)
    causal = row_i >= col_i

    groups = range(SSD_GROUPS)
    heads = range(SSD_HEADS)
    grp = lambda h: h // (SSD_HEADS // SSD_GROUPS)
    bm_g = [bm[:, g * N:(g + 1) * N] for g in groups]
    cm_g = [cm[:, g * N:(g + 1) * N] for g in groups]
    cb_g = [_dot_nt(cm_g[g].astype(BF16), bm_g[g].astype(BF16)) for g in groups]
    bm_t = [bm_g[g].T for g in groups]
    cum_c = [cum[:, h:h + 1] for h in heads]
    cum_r = [cum_t[h:h + 1, :] for h in heads]
    last = [cum_t[h:h + 1, L - 1:L] for h in heads]
    seg = [jnp.where(causal, jnp.exp(jnp.minimum(cum_c[h] - cum_r[h], 0.0)), 0.0) for h in heads]
    xdt = [(xs[:, h * P:(h + 1) * P] * dt[:, h:h + 1]).astype(BF16) for h in heads]
    h_prev = [state_ref[h] for h in heads]
    y_diag = [_dot((cb_g[grp(h)] * seg[h]).astype(BF16), xdt[h]) for h in heads]
    y_off = [_dot((cm_g[grp(h)] * jnp.exp(cum_c[h])).astype(BF16), h_prev[h].astype(BF16)) for h in heads]
    st = [_dot((bm_t[grp(h)] * jnp.exp(last[h] - cum_r[h])).astype(BF16), xdt[h]) for h in heads]
    for h in heads:
        state_ref[h] = jnp.exp(last[h]) * h_prev[h] + st[h]
    y = jnp.concatenate([y_diag[h] + y_off[h] for h in heads], axis=1) + xs * dsk_ref[...]
    y = y * _silu(z_ref[...].astype(F32))
    gw = SSD_INNER // SSD_GROUPS
    outs = []
    for g in range(SSD_GROUPS):
        yg = y[:, g * gw:(g + 1) * gw]
        outs.append(yg * lax.rsqrt(jnp.mean(yg * yg, axis=-1, keepdims=True) + RMS_EPS))
    y_ref[...] = (jnp.concatenate(outs, axis=1) * nrm_ref[...]).astype(y_ref.dtype)


def _ssd(xbc, fd, dtt, z, conv_w, conv_b, a_log, d_skip_c, ssd_norm, tri, batch, seq):
    L = SSD_CHUNK
    nc = seq // L
    t_rows = batch * seq
    row = lambda w: pl.BlockSpec((L, w), lambda b, c: (b * nc + c, 0))
    full = lambda a: pl.BlockSpec(a.shape, lambda b, c: (0,) * a.ndim)
    prev = pl.BlockSpec((8, SSD_CONV_DIM), lambda b, c: (jnp.maximum((b * nc + c) * (L // 8) - 1, 0), 0))
    return pl.pallas_call(
        _ssd_kernel,
        grid=(batch, nc),
        in_specs=[row(SSD_CONV_DIM), prev, row(LANES),
                  pl.BlockSpec((None, SSD_HEADS, L), lambda b, c: (b, 0, c)),
                  row(SSD_INNER), full(conv_w), full(conv_b), full(a_log[None, :]),
                  full(a_log[:, None]), full(d_skip_c), full(ssd_norm), full(tri)],
        out_specs=row(SSD_INNER),
        out_shape=jax.ShapeDtypeStruct((t_rows, SSD_INNER), BF16),
        scratch_shapes=[pltpu.VMEM((SSD_HEADS, SSD_STATE, HEAD_DIM), F32)],
        compiler_params=_cparams(("arbitrary", "arbitrary")),
        name="ssd_scan",
    )(xbc, xbc, fd, dtt, z, conv_w, conv_b, a_log[None, :], a_log[:, None], d_skip_c, ssd_norm, tri)


def _swiglu_hidden_chunk(h, w13_ref, act_ref, c, tf):
    f = w13_ref.shape[1] // 2
    off = c * tf if isinstance(c, int) else pl.multiple_of(c * tf, tf)
    gate = _dot(h, w13_ref[:, pl.ds(off, tf)])
    up = _dot(h, w13_ref[:, pl.ds(f + off, tf)])
    act_ref[:, pl.ds(off, tf)] = (_silu(gate) * up).astype(BF16)


def _even_out_kernel(o_ref, y_ref, x_ref, wo_ref, g_ref, w13_ref, w2_ref, out_ref, act_ref):
    w = FOX_WIDTH
    x1 = x_ref[...] + _dot(o_ref[...], wo_ref[:w, :]) + _dot(y_ref[...], wo_ref[w:, :])
    h = _rms(x1, g_ref[...]).astype(BF16)

    for c in range(w2_ref.shape[0] // FFN_CHUNK):
        _swiglu_hidden_chunk(h, w13_ref, act_ref, c, FFN_CHUNK)
    out_ref[...] = x1 + _dot(act_ref[...], w2_ref[...])


def _even_out_ffn(o_fox, y_ssd, x2d, wo, g, w13, w2):
    t_rows = x2d.shape[0]
    tm = min(ROW_TILE, t_rows)
    row = lambda w: pl.BlockSpec((tm, w), lambda i: (i, 0))
    full = lambda a: pl.BlockSpec(a.shape, lambda i: (0,) * a.ndim, pipeline_mode=pl.Buffered(1))
    return pl.pallas_call(
        _even_out_kernel,
        grid=(t_rows // tm,),
        in_specs=[row(FOX_WIDTH), row(SSD_INNER), row(D_MODEL), full(wo), full(g), full(w13), full(w2)],
        out_specs=row(D_MODEL),
        out_shape=jax.ShapeDtypeStruct((t_rows, D_MODEL), F32),
        scratch_shapes=[pltpu.VMEM((tm, w2.shape[0]), BF16)],
        compiler_params=_cparams(("arbitrary",)),
        name="even_out_ffn",
    )(o_fox, y_ssd, x2d, wo, g, w13, w2)


def _rope(x, cos, sin_signed):
    lane = lax.broadcasted_iota(jnp.int32, x.shape, 1)
    first = (lane % HEAD_DIM) < (HEAD_DIM // 2)
    rot = jnp.where(first, pltpu.roll(x, LANES - HEAD_DIM // 2, 1), pltpu.roll(x, HEAD_DIM // 2, 1))
    return x * cos + rot * sin_signed


def _emit_dilated_chunk(val, c, width, refs, slab_ref):
    tm = val.shape[0]
    slab_ref[c * tm:(c + 1) * tm, :] = val
    for (_, d), ref in zip(DIL_PATTERNS, refs):
        if d == 1:
            ref[:, c * LANES:(c + 1) * LANES] = val.astype(ref.dtype)
            continue
        n = tm // d
        for r in range(d):
            piece = slab_ref[pl.ds(c * tm + r, n, stride=d), :]
            ref[:, r * width + c * LANES:r * width + (c + 1) * LANES] = piece.astype(ref.dtype)


def _read_dilated_view(ref, d, tm, slab_ref):
    if d == 1:
        return ref[...].astype(F32)
    width = ref.shape[1] // d
    chunks = width // LANES
    n = tm // d
    for r in range(d):
        for c in range(chunks):
            piece = ref[:, r * width + c * LANES:r * width + (c + 1) * LANES].astype(F32)
            slab_ref[pl.ds(c * tm + r, n, stride=d), :] = piece
    return jnp.concatenate([slab_ref[c * tm:(c + 1) * tm, :] for c in range(chunks)], axis=1)


def _odd_qkv_kernel(x_ref, g_ref, w_ref, qg_ref, kg_ref, hm_ref, cos_ref, sin_ref, *rest):
    n_pat = len(DIL_PATTERNS)
    q_refs, k_refs, v_refs = rest[:n_pat], rest[n_pat:2 * n_pat], rest[2 * n_pat:3 * n_pat]
    slabs = rest[3 * n_pat:]
    h = _rms(x_ref[...], g_ref[...]).astype(BF16)
    expand = hm_ref[...]
    w = DIL_WIDTH
    cos = cos_ref[...]
    sin = sin_ref[...]

    def normed_rotated(col, gain_ref, scale, refs, slab_ref):
        x = _dot(h, w_ref[:, col:col + w])
        sums = _dot_nt((x * x).astype(BF16), expand) * (1.0 / HEAD_DIM)
        inv = lax.rsqrt(_dot_exact_rhs(sums, expand, 2) + RMS_EPS)
        for c in range(w // LANES):
            sl = slice(c * LANES, (c + 1) * LANES)
            xc = x[:, sl] * inv[:, sl] * gain_ref[:, sl]
            _emit_dilated_chunk(_rope(xc, cos, sin) * scale, c, w, refs, slab_ref)

    normed_rotated(0, qg_ref, HEAD_DIM ** -0.5 * LOG2E, q_refs, slabs[0])
    normed_rotated(w, kg_ref, 1.0, k_refs, slabs[1])
    v = _dot(h, w_ref[:, 2 * w:])
    for c in range(w // LANES):
        _emit_dilated_chunk(v[:, c * LANES:(c + 1) * LANES], c, w, v_refs, slabs[2])


def _odd_qkv(x2d, seq, g, w, qg, kg, hm, cos, sin):
    t_rows = x2d.shape[0]
    tm = min(ROW_TILE, seq)
    per_seq = seq // tm
    row = lambda rows, wd: pl.BlockSpec((rows, wd), lambda i: (i, 0))
    full = lambda a: pl.BlockSpec(a.shape, lambda i: (0,) * a.ndim)
    tab = pl.BlockSpec((tm, LANES), lambda i: (i % per_seq, 0))
    views = [(tm // d, t_rows // d, d * DIL_WIDTH) for (_, d) in DIL_PATTERNS]
    outs = pl.pallas_call(
        _odd_qkv_kernel,
        grid=(t_rows // tm,),
        in_specs=[row(tm, D_MODEL), full(g), full(w), full(qg), full(kg), full(hm), tab, tab],
        out_specs=[row(r, wd) for (r, _, wd) in views] * 3,
        out_shape=[jax.ShapeDtypeStruct((n, wd), BF16) for (_, n, wd) in views] * 3,
        scratch_shapes=[pltpu.VMEM((tm * (DIL_WIDTH // LANES), LANES), F32)] * 3,
        compiler_params=_cparams(("arbitrary",)),
        name="odd_qkv",
    )(x2d, g, w, qg, kg, hm, cos, sin)
    n_pat = len(DIL_PATTERNS)
    return outs[:n_pat], outs[n_pat:2 * n_pat], outs[2 * n_pat:]


def _dilated_kernel(q_ref, kp_ref, kc_ref, vp_ref, vc_ref, band_ref, o_ref, lse_ref):
    j = pl.program_id(2)
    blk = DIL_BLOCK
    n_sub = q_ref.shape[0] // blk
    key_row = lax.broadcasted_iota(jnp.int32, (2 * blk, blk), 0)
    band = band_ref[...]
    bias = [jnp.where((j == 0) & (key_row < blk), NEG_BIG, band)] + [band] * (n_sub - 1)
    lane = lax.broadcasted_iota(jnp.int32, (blk, LANES), 1)
    ones_rows = jnp.ones((DEN_ROWS, 2 * blk), BF16)
    pairs = range(DIL_HEADS // 2)
    heads = range(DIL_HEADS)
    subs = range(n_sub)
    lanes_of = lambda pr: slice(pr * LANES, (pr + 1) * LANES)

    def keys_of(prev_ref, cur_ref, u, pr):
        if u == 0:
            return jnp.concatenate([prev_ref[:, lanes_of(pr)], cur_ref[:blk, lanes_of(pr)]], axis=0)
        return cur_ref[(u - 1) * blk:(u + 1) * blk, lanes_of(pr)]

    kk = [[keys_of(kp_ref, kc_ref, u, pr) for pr in pairs] for u in subs]
    vt = [[jnp.concatenate([keys_of(vp_ref, vc_ref, u, pr).astype(F32).T.astype(BF16), ones_rows], axis=0)
           for pr in pairs] for u in subs]
    zero = jnp.zeros((blk, LANES), q_ref.dtype)
    qm = [[jnp.where((lane >= (h % 2) * HEAD_DIM) & (lane < (h % 2 + 1) * HEAD_DIM),
                     q_ref[u * blk:(u + 1) * blk, lanes_of(h // 2)], zero) for h in heads] for u in subs]
    s = [[_dot_nt(kk[u][h // 2], qm[u][h]) + bias[u] for h in heads] for u in subs]
    m = [[jnp.max(s[u][h], axis=0, keepdims=True) for h in heads] for u in subs]
    e = [[jnp.exp2(s[u][h] - m[u][h]).astype(BF16) for h in heads] for u in subs]
    ov = [[_dot(vt[u][h // 2], e[u][h]) for h in heads] for u in subs]
    for u in subs:
        den = [ov[u][h][LANES:LANES + 1, :] for h in heads]
        for pr in pairs:
            o_t = jnp.concatenate([(ov[u][2 * pr][:LANES] / den[2 * pr])[:HEAD_DIM],
                                   (ov[u][2 * pr + 1][:LANES] / den[2 * pr + 1])[HEAD_DIM:]], axis=0)
            o_ref[u * blk:(u + 1) * blk, lanes_of(pr)] = o_t.T.astype(o_ref.dtype)
        lse_t = jnp.concatenate([m[u][h] + jnp.log2(den[h]) for h in heads]
                                + [jnp.zeros((LANES - DIL_HEADS, blk), F32)], axis=0)
        lse_ref[u * blk:(u + 1) * blk, :] = lse_t.T


def _dilated_branch(q, k, v, band, dilation, batch):
    d = dilation
    w = DIL_WIDTH
    n_sub = min(DIL_SUB, q.shape[0] // batch // DIL_BLOCK)
    rows = n_sub * DIL_BLOCK
    nb = q.shape[0] // batch // rows
    cur = pl.BlockSpec((rows, w), lambda bi, r, j: (bi * nb + j, r))
    prev = pl.BlockSpec((DIL_BLOCK, w), lambda bi, r, j: ((bi * nb + jnp.maximum(j, 1)) * n_sub - 1, r))
    lse_spec = pl.BlockSpec((rows, LANES), lambda bi, r, j: (bi * nb + j, r))
    return pl.pallas_call(
        _dilated_kernel,
        grid=(batch, d, nb),
        in_specs=[cur, prev, cur, prev, cur, pl.BlockSpec(band.shape, lambda bi, r, j: (0, 0))],
        out_specs=[cur, lse_spec],
        out_shape=[jax.ShapeDtypeStruct(q.shape, BF16),
                   jax.ShapeDtypeStruct((q.shape[0], d * LANES), F32)],
        compiler_params=_cparams(("arbitrary",) * 3),
        name=f"dilated_attn_d{d}",
    )(q, k, k, v, v, band)


def _odd_out_kernel(o1_ref, o2_ref, o3_ref, l1_ref, l2_ref, l3_ref, x_ref, wo_ref, g_ref,
                    rw_ref, ex_ref, tril_ref, x1_ref, h_ref, info_ref, cnt_ref, carry_ref, slab_ref):
    i = pl.program_id(0)
    tm = x_ref.shape[0]

    @pl.when(i == 0)
    def _():
        carry_ref[...] = jnp.zeros_like(carry_ref)

    dils = [d for (_, d) in DIL_PATTERNS]
    l1, l2, l3 = [_read_dilated_view(r, d, tm, slab_ref) for r, d in zip((l1_ref, l2_ref, l3_ref), dils)]
    mx = jnp.maximum(jnp.maximum(l1, l2), l3)
    a1, a2, a3 = jnp.exp2(l1 - mx), jnp.exp2(l2 - mx), jnp.exp2(l3 - mx)
    inv = 1.0 / (a1 + a2 + a3)
    ex = ex_ref[...]
    o = None
    for a, o_ref, d in zip((a1, a2, a3), (o1_ref, o2_ref, o3_ref), dils):
        term = _dot_exact_rhs(a * inv, ex, 1) * _read_dilated_view(o_ref, d, tm, slab_ref)
        o = term if o is None else o + term
    x1 = x_ref[...] + _dot(o.astype(BF16), wo_ref[...])
    x1_ref[...] = x1
    h = _rms(x1, g_ref[...])
    _wide_to_rows(h_ref, h)

    h_hi, h_lo = _split_bf16(h, 2)
    w_hi, w_lo = _split_bf16(rw_ref[...], 2)
    logits = _dot(h_hi, w_hi) + (_dot(h_lo, w_hi) + _dot(h_hi, w_lo))
    lane = lax.broadcasted_iota(jnp.int32, logits.shape, 1)
    logits = jnp.where(lane < N_EXPERTS, logits, -jnp.inf)
    m1 = jnp.max(logits, axis=-1, keepdims=True)
    i1 = jnp.min(jnp.where(logits == m1, lane, LANES), axis=-1, keepdims=True)
    rest = jnp.where(lane == i1, -jnp.inf, logits)
    m2 = jnp.max(rest, axis=-1, keepdims=True)
    i2 = jnp.min(jnp.where(rest == m2, lane, LANES), axis=-1, keepdims=True)
    e2 = jnp.exp(m2 - m1)
    g1 = 1.0 / (1.0 + e2)
    g2 = e2 * g1
    hot1 = lane == i1
    hot2 = lane == i2
    onehot = jnp.where(hot1 | hot2, 1.0, 0.0).astype(BF16)
    before = _dot(tril_ref[...], onehot) + carry_ref[...]
    r1 = jnp.sum(jnp.where(hot1, before, 0.0), axis=-1, keepdims=True)
    r2 = jnp.sum(jnp.where(hot2, before, 0.0), axis=-1, keepdims=True)
    total = before[-1:, :] + onehot[-1:, :].astype(F32)
    carry_ref[...] = total
    cnt_ref[...] = jnp.broadcast_to(total, cnt_ref.shape)
    info = jnp.where(lane == 0, i1.astype(F32), 0.0)
    info = jnp.where(lane == 1, i2.astype(F32), info)
    info = jnp.where(lane == 2, g1, info)
    info = jnp.where(lane == 3, g2, info)
    info = jnp.where(lane == 4, r1, info)
    info = jnp.where(lane == 5, r2, info)
    info_ref[...] = info


def _odd_out_router(o1, o2, o3, l1, l2, l3, x2d, wo, g, rw, ex, tril_strict):
    t_rows = x2d.shape[0]
    tm = min(ROW_TILE, t_rows)
    n = t_rows // tm
    row = lambda w: pl.BlockSpec((tm, w), lambda i: (i, 0))
    full = lambda a: pl.BlockSpec(a.shape, lambda i: (0,) * a.ndim)
    view = lambda w: [pl.BlockSpec((tm // d, d * w), lambda i: (i, 0)) for (_, d) in DIL_PATTERNS]
    return pl.pallas_call(
        _odd_out_kernel,
        grid=(n,),
        in_specs=view(DIL_WIDTH) + view(LANES) + [row(D_MODEL), full(wo), full(g), full(rw),
                                                  full(ex), full(tril_strict)],
        out_specs=[row(D_MODEL), pl.BlockSpec((tm * TILE_SUBLANES, LANES), lambda i: (i, 0)), row(LANES),
                   pl.BlockSpec((8, LANES), lambda i: (0, 0))],
        out_shape=[jax.ShapeDtypeStruct((t_rows, D_MODEL), F32),
                   jax.ShapeDtypeStruct((t_rows * TILE_SUBLANES, LANES), F32),
                   jax.ShapeDtypeStruct((t_rows, LANES), F32), jax.ShapeDtypeStruct((8, LANES), F32)],
        scratch_shapes=[pltpu.VMEM((1, LANES), F32), pltpu.VMEM((tm * (DIL_WIDTH // LANES), LANES), F32)],
        compiler_params=_cparams(("arbitrary",)),
        name="odd_out_router",
    )(o1, o2, o3, l1, l2, l3, x2d, wo, g, rw, ex, tril_strict)


def _rows_to_wide(ref):
    n = ref.shape[0] // TILE_SUBLANES
    return jnp.concatenate([ref[pl.ds(c, n, stride=TILE_SUBLANES), :] for c in range(TILE_SUBLANES)], axis=1)


def _wide_to_rows(ref, val):
    n = ref.shape[0] // TILE_SUBLANES
    for c in range(TILE_SUBLANES):
        ref[pl.ds(c, n, stride=TILE_SUBLANES), :] = val[:, c * LANES:(c + 1) * LANES]


def _moe_kernel(te_ref, nu_ref, src_cur_ref, src_nxt_ref, dst_prv_ref, dst_cur_ref, h_hbm, w13_ref, w2_ref,
                y_hbm, xa, xb, ya, yb, act_ref, gsem, ssem):
    i = pl.program_id(0)
    n_used = nu_ref[0]
    ts = TILE_SUBLANES
    gm = xa.shape[0] // ts
    n_chunks = w2_ref.shape[0] // MOE_CHUNK
    per_chunk = gm // n_chunks
    head_rows = gm - n_chunks * per_chunk

    def tile_of(ref, row):
        return ref.at[pl.ds(pl.multiple_of(row * ts, ts), ts)]

    def gather_row(src_ref, r, xdst, sem):
        pltpu.make_async_copy(tile_of(h_hbm, src_ref[0, 0, r]), tile_of(xdst, r), sem).start()

    def scatter_row(dst_ref, r, ysrc, sem):
        pltpu.make_async_copy(tile_of(ysrc, r), tile_of(y_hbm, dst_ref[0, 0, r]), sem).start()

    def wait_rows(buf, sem):
        pltpu.make_async_copy(h_hbm.at[pl.ds(0, gm * ts)], buf, sem).wait()

    @pl.when(i == 0)
    def _():
        yb[...] = jnp.zeros_like(yb)
        out_half = y_hbm.shape[0] // ts // 2
        spare = [y_hbm.at[pl.ds((k * out_half + (out_half - N_EXPERTS * gm) + e * gm) * ts, gm * ts)]
                 for k in range(2) for e in range(N_EXPERTS)]
        for blk in spare:
            pltpu.make_async_copy(yb, blk, ssem.at[1]).start()
        for blk in spare:
            pltpu.make_async_copy(yb, blk, ssem.at[1]).wait()

        def body(r, carry):
            gather_row(src_cur_ref, r, xa, gsem.at[0])
            return carry
        lax.fori_loop(0, gm, body, 0)

    def tile(x_cur, x_nxt, y_cur, y_prv, p):
        wait_rows(x_cur, gsem.at[p])

        @pl.when(i >= 1)
        def _():
            wait_rows(y_cur, ssem.at[p])

        def issue(r):
            gather_row(src_nxt_ref, r, x_nxt, gsem.at[1 - p])
            scatter_row(dst_prv_ref, r, y_prv, ssem.at[1 - p])

        for r in range(head_rows):
            issue(n_chunks * per_chunk + r)
        x = _rows_to_wide(x_cur).astype(BF16)

        def body(c, carry):
            for u in range(MOE_UNROLL):
                _swiglu_hidden_chunk(x, w13_ref, act_ref, c * MOE_UNROLL + u, MOE_CHUNK)
            for j in range(per_chunk * MOE_UNROLL):
                issue(c * (per_chunk * MOE_UNROLL) + j)
            return carry

        lax.fori_loop(0, n_chunks // MOE_UNROLL, body, 0)
        _wide_to_rows(y_cur, _dot(act_ref[...], w2_ref[...]))

        @pl.when(i == n_used - 1)
        def _():
            def body(r, carry):
                scatter_row(dst_cur_ref, r, y_cur, ssem.at[p])
                return carry
            lax.fori_loop(0, gm, body, 0)
            wait_rows(y_prv, ssem.at[1 - p])
            wait_rows(y_cur, ssem.at[p])
            wait_rows(x_nxt, gsem.at[1 - p])

    @pl.when((i < n_used) & (i % 2 == 0))
    def _():
        tile(xa, xb, ya, yb, 0)

    @pl.when((i < n_used) & (i % 2 == 1))
    def _():
        tile(xb, xa, yb, ya, 1)


def _moe_ffn(h_tiles, w13, w2, layer, tile_expert, n_used, src, dst, out_rows):
    nt = tile_expert.shape[0]
    gm = src.shape[-1]
    tab = lambda f: pl.BlockSpec((1, 1, gm), lambda i, te, nu: (f(i), 0, 0), memory_space=pltpu.SMEM)
    grid_spec = pltpu.PrefetchScalarGridSpec(
        num_scalar_prefetch=2,
        grid=(nt,),
        in_specs=[
            tab(lambda i: i),
            tab(lambda i: jnp.minimum(i + 1, nt - 1)),
            tab(lambda i: i),
            tab(lambda i: i + 1),
            pl.BlockSpec(memory_space=pl.ANY),
            pl.BlockSpec((None, None) + w13.shape[2:], lambda i, te, nu: (layer, te[i], 0, 0),
                         pipeline_mode=pl.Buffered(1)),
            pl.BlockSpec((None, None) + w2.shape[2:], lambda i, te, nu: (layer, te[i], 0, 0),
                         pipeline_mode=pl.Buffered(1)),
        ],
        out_specs=pl.BlockSpec(memory_space=pl.ANY),
        scratch_shapes=[pltpu.VMEM((gm * TILE_SUBLANES, LANES), F32)] * 4
        + [pltpu.VMEM((gm, w2.shape[2]), BF16), pltpu.SemaphoreType.DMA((2,)), pltpu.SemaphoreType.DMA((2,))],
    )
    return pl.pallas_call(
        _moe_kernel,
        grid_spec=grid_spec,
        out_shape=jax.ShapeDtypeStruct((out_rows * TILE_SUBLANES, LANES), F32),
        compiler_params=_cparams(("arbitrary",)),
        name="moe_ffn",
    )(tile_expert, n_used, src, src, dst, dst, h_tiles, w13, w2)


def _combine_kernel(x_ref, y0_ref, y1_ref, info_ref, o_ref):
    o_ref[...] = (x_ref[...] + info_ref[:, 2:3] * _rows_to_wide(y0_ref)
                  + info_ref[:, 3:4] * _rows_to_wide(y1_ref))


def _combine(x1, y, info):
    t_rows = x1.shape[0]
    tm = min(ROW_TILE, t_rows)
    row = lambda w: pl.BlockSpec((tm, w), lambda i: (i, 0))
    ysp = lambda kk: pl.BlockSpec((None, tm * TILE_SUBLANES, LANES), lambda i: (kk, i, 0))
    return pl.pallas_call(
        _combine_kernel,
        grid=(t_rows // tm,),
        in_specs=[row(D_MODEL), ysp(0), ysp(1), row(LANES)],
        out_specs=row(D_MODEL),
        out_shape=jax.ShapeDtypeStruct((t_rows, D_MODEL), F32),
        compiler_params=_cparams(("arbitrary",)),
        name="moe_combine",
    )(x1, y, y, info)


def _head_mean_matrix(width):
    idx = np.arange(width) // HEAD_DIM
    return jnp.asarray((idx[:, None] == idx[None, :]).astype(np.float32) / HEAD_DIM, dtype=BF16)


def _tril(n, strict):
    r = np.arange(n)
    m = (r[None, :] < r[:, None]) if strict else (r[None, :] <= r[:, None])
    return jnp.asarray(m.astype(np.float32), dtype=BF16)


def _rope_tables(seq):
    half = HEAD_DIM // 2
    inv_freq = ROPE_THETA ** (-jnp.arange(half, dtype=F32) / half)
    ang = jnp.arange(seq, dtype=F32)[:, None] * inv_freq[None, :]
    cos = jnp.cos(ang)
    sin = jnp.sin(ang)
    cos_t = jnp.concatenate([cos, cos, cos, cos], axis=1)
    sin_t = jnp.concatenate([-sin, sin, -sin, sin], axis=1)
    return cos_t, sin_t


def _band_bias():
    qi = np.arange(DIL_BLOCK)[None, :]
    ki = np.arange(-DIL_BLOCK, DIL_BLOCK)[:, None]
    rel = qi - ki
    return jnp.asarray(np.where((rel >= 0) & (rel <= DIL_BLOCK), 0.0, NEG_BIG).astype(np.float32))


def _head_expand():
    m = np.zeros((LANES, DIL_WIDTH), np.float32)
    for h in range(DIL_HEADS):
        m[h, h * HEAD_DIM:(h + 1) * HEAD_DIM] = 1.0
    return jnp.asarray(m, dtype=BF16)


def _bias_placement():
    pq = np.zeros((BIAS_PARTS, LANES, FOX_WIDTH), np.float32)
    pk = np.zeros((BIAS_PARTS, LANES, FOX_WIDTH), np.float32)
    qone = np.zeros((1, FOX_WIDTH), np.float32)
    kone = np.zeros((1, FOX_WIDTH), np.float32)
    for head in range(FOX_HEADS):
        base = (head // 2) * LANES + (head % 2) * BIAS_LANES
        for j in range(BIAS_PARTS):
            pq[j, head, base + j] = 1.0
            pk[j, head, base + BIAS_PARTS + j] = -1.0
            qone[0, base + BIAS_PARTS + j] = 1.0
            kone[0, base + j] = 1.0
    return (jnp.asarray(pq, dtype=BF16), jnp.asarray(pk, dtype=BF16), jnp.asarray(qone), jnp.asarray(kone))


def _even_layer(x2d, batch, seq, norm1, w_in, fox_qn, fox_kn, fox_fbias, conv_w, conv_b, dt_bias, a_log,
                d_skip, ssd_norm, w_out, norm2, ffn_w13, ffn_w2):
    w = FOX_WIDTH
    o_z = 3 * w + FOX_HEADS
    o_x = o_z + SSD_INNER
    o_dt = o_x + SSD_CONV_DIM
    wqk = w_in[:, :2 * w].astype(BF16)
    wvt = w_in[:, 2 * w:3 * w].T.astype(BF16)
    wzx = w_in[:, o_z:o_dt].astype(BF16)
    pad = LANES - FOX_HEADS - SSD_HEADS
    wfd = jnp.concatenate([w_in[:, 3 * w:o_z], w_in[:, o_dt:], jnp.zeros((D_MODEL, pad), F32)], axis=1).astype(BF16)
    fdb = jnp.concatenate([fox_fbias, dt_bias, jnp.zeros((pad,), F32)])[None, :]
    tm = min(ROW_TILE, seq)
    outs = _even_in_proj(
        x2d, seq, norm1[None, :], wqk, wvt, wzx, wfd, fox_qn.reshape(1, w), fox_kn.reshape(1, w), fdb,
        _head_mean_matrix(w), _tril(tm, False), *_bias_placement())
    q, qb, k, kb, vt, z, xbc, fd = outs[:8]
    if isinstance(x2d, tuple):
        x2d = outs[8]

    o_fox = _fox_attention(q, qb, k, kb, vt, batch, seq)

    dtt = jnp.transpose(fd[:, FOX_HEADS:FOX_HEADS + SSD_HEADS].reshape(batch, seq, SSD_HEADS), (0, 2, 1))
    d_skip_c = jnp.repeat(d_skip, HEAD_DIM)[None, :]
    y = _ssd(xbc, fd, dtt, z, conv_w, conv_b[None, :], a_log, d_skip_c, ssd_norm[None, :],
             _tril(SSD_CHUNK, False), batch, seq)

    return _even_out_ffn(o_fox, y, x2d, w_out.astype(BF16), norm2[None, :],
                         ffn_w13.astype(BF16), ffn_w2.astype(BF16))


def _moe_tables(info, counts, t_rows, gm):
    e1 = info[:, 0].astype(jnp.int32)
    e2 = info[:, 1].astype(jnp.int32)
    r1 = info[:, 4].astype(jnp.int32)
    r2 = info[:, 5].astype(jnp.int32)
    cnt = counts[0, :N_EXPERTS].astype(jnp.int32)
    tiles = (cnt + gm - 1) // gm
    tile_end = jnp.cumsum(tiles)
    offs = (tile_end - tiles) * gm
    nt = (2 * t_rows) // gm + N_EXPERTS
    tile_idx = jnp.arange(nt, dtype=jnp.int32)
    tile_expert = jnp.minimum(jnp.sum((tile_idx[:, None] >= tile_end[None, :]).astype(jnp.int32), axis=1),
                              N_EXPERTS - 1)
    n_used = tile_end[-1:].astype(jnp.int32)
    slots = jnp.concatenate([offs[e1] + r1, offs[e2] + r2])
    codes = jnp.full((nt * gm,), -1, jnp.int32).at[slots].set(
        jnp.arange(2 * t_rows, dtype=jnp.int32), unique_indices=True)
    out_half = t_rows + N_EXPERTS * gm
    second = codes >= t_rows
    src = jnp.where(codes < 0, 0, jnp.where(second, codes - t_rows, codes))
    spare = t_rows + jnp.repeat(tile_expert, gm) * gm + jnp.tile(jnp.arange(gm, dtype=jnp.int32), nt)
    dst = jnp.where(codes < 0, spare, jnp.where(second, codes - t_rows + out_half, codes))
    dst = jnp.concatenate([t_rows + jnp.arange(gm, dtype=jnp.int32), dst])
    return tile_expert, n_used, src.reshape(nt, 1, gm), dst.reshape(nt + 1, 1, gm), out_half


def _odd_layer(x2d, batch, seq, norm1, w_qkv, qn, kn, w_out, norm2, router, moe_w13, moe_w2, layer):
    t_rows = batch * seq
    cos, sin = _rope_tables(seq)
    qs, ks, vs = _odd_qkv(x2d, seq, norm1[None, :], w_qkv.astype(BF16), qn.reshape(1, DIL_WIDTH),
                          kn.reshape(1, DIL_WIDTH), _head_expand(), cos, sin)
    band = _band_bias()
    outs, lses = zip(*[_dilated_branch(q, k, v, band, d, batch)
                       for q, k, v, (_, d) in zip(qs, ks, vs, DIL_PATTERNS)])
    rw = jnp.concatenate([router, jnp.zeros((D_MODEL, LANES - N_EXPERTS), F32)], axis=1)
    tm = min(ROW_TILE, t_rows)
    x1, h, info, counts = _odd_out_router(*outs, *lses, x2d, w_out.astype(BF16), norm2[None, :], rw,
                                          _head_expand(), _tril(tm, True))
    gm = min(MOE_TILE, t_rows)
    tile_expert, n_used, src, dst, out_half = _moe_tables(info, counts, t_rows, gm)
    y = _moe_ffn(h, moe_w13, moe_w2, layer, tile_expert, n_used, src, dst, 2 * out_half)
    return x1, y.reshape(2, out_half * TILE_SUBLANES, LANES), info


def kernel(x, e_norm1, e_w_in, e_fox_qn, e_fox_kn, e_fox_fbias, e_conv_w, e_conv_b, e_dt_bias, e_a_log, e_d_skip, e_ssd_norm, e_w_out, e_norm2, e_ffn_w13, e_ffn_w2, o_norm1, o_w_qkv, o_qn, o_kn, o_w_out, o_norm2, o_router, o_moe_w13, o_moe_w2):
    batch, seq, _ = x.shape
    depth = e_norm1.shape[0] + o_norm1.shape[0]
    x2d = x.reshape(batch * seq, D_MODEL)
    moe_w13 = o_moe_w13.astype(BF16)
    moe_w2 = o_moe_w2.astype(BF16)
    for i in range(depth):
        j = i // 2
        if i % 2 == 0:
            x2d = _even_layer(x2d, batch, seq, e_norm1[j], e_w_in[j], e_fox_qn[j], e_fox_kn[j], e_fox_fbias[j],
                              e_conv_w[j], e_conv_b[j], e_dt_bias[j], e_a_log[j], e_d_skip[j], e_ssd_norm[j],
                              e_w_out[j], e_norm2[j], e_ffn_w13[j], e_ffn_w2[j])
        else:
            x2d = _odd_layer(x2d, batch, seq, o_norm1[j], o_w_qkv[j], o_qn[j], o_kn[j], o_w_out[j], o_norm2[j],
                             o_router[j], moe_w13, moe_w2, j)
            if i == depth - 1:
                x2d = _combine(*x2d)
    return x2d.reshape(batch, seq, D_MODEL)
```

```python
import functools
import math

import jax
import jax.numpy as jnp
import numpy as np
from jax import lax
from jax.experimental import pallas as pl
from jax.experimental.pallas import tpu as pltpu

F32 = jnp.float32
BF16 = jnp.bfloat16

D_MODEL = 1024
HEAD_DIM = 64
RMS_EPS = 1e-6
ROPE_THETA = 10000.0
FOX_HEADS = 8
FOX_WIDTH = FOX_HEADS * HEAD_DIM
SSD_HEADS = 8
SSD_INNER = 512
SSD_GROUPS = 2
SSD_STATE = 128
SSD_CONV = 4
SSD_CHUNK = 128
SSD_CONV_DIM = SSD_INNER + 2 * SSD_GROUPS * SSD_STATE
DIL_HEADS = 16
DIL_WIDTH = DIL_HEADS * HEAD_DIM
DIL_PATTERNS = ((128, 1), (512, 4), (2048, 16))
DIL_BLOCK = 128
DIL_SUB = 4
FFN_DIM = 2816
N_EXPERTS = 8
EXPERT_DIM = 3584

LANES = 128
TILE_SUBLANES = D_MODEL // LANES
NEG_BIG = -1e30
LOG2E = math.log2(math.e)
DEN_ROWS = 16
BIAS_PARTS = 3
BIAS_LANES = 2 * BIAS_PARTS
VMEM_LIMIT = 56 * 1024 * 1024

ROW_TILE = 512
ATTN_TILE = 512
FOX_PAIRS_PER_STEP = 4
FFN_CHUNK = 256
MOE_CHUNK = 256
MOE_UNROLL = 14
MOE_TILE = 512


def _cparams(sem):
    return pltpu.CompilerParams(dimension_semantics=sem, vmem_limit_bytes=VMEM_LIMIT)


def _softplus_parts(x):
    return jnp.log(1.0 + jnp.exp(-jnp.abs(x)))


def _split_bf16(a, parts):
    out = []
    r = a
    for _ in range(parts):
        p = r.astype(BF16)
        out.append(p)
        r = r - p.astype(F32)
    return out


def _dot(a, b):
    return jnp.dot(a, b, preferred_element_type=F32)


def _dot_nt(a, b):
    return lax.dot_general(a, b, (((1,), (1,)), ((), ())), preferred_element_type=F32)


def _dot_exact_rhs(a_f32, b_bf16, parts):
    acc = None
    for p in _split_bf16(a_f32, parts):
        t = _dot(p, b_bf16)
        acc = t if acc is None else acc + t
    return acc


def _dot_exact_lhs(a_bf16, b_f32, parts):
    acc = None
    for p in _split_bf16(b_f32, parts):
        t = _dot(a_bf16, p)
        acc = t if acc is None else acc + t
    return acc


def _rms(x, g):
    return x * lax.rsqrt(jnp.mean(x * x, axis=-1, keepdims=True) + RMS_EPS) * g


def _head_norm(x, hm, gain):
    ms = _dot((x * x).astype(BF16), hm)
    return x * lax.rsqrt(ms + RMS_EPS) * gain


def _silu(x):
    return x * (1.0 / (1.0 + jnp.exp(-x)))


def _even_in_kernel(tiles_per_seq, after_moe, *refs):
    n_lead = 4 if after_moe else 1
    lead, refs = refs[:n_lead], refs[n_lead:]
    (g_ref, wqk_ref, wvt_ref, wzx_ref, wfd_ref, qg_ref, kg_ref, fdb_ref, hm_ref, tril_ref, pq_ref, pk_ref,
     qone_ref, kone_ref, q_ref, qb_ref, k_ref, kb_ref, vt_ref, z_ref, xbc_ref, fd_ref) = refs[:22]
    carry_ref = refs[-1]
    i = pl.program_id(0)

    @pl.when(i % tiles_per_seq == 0)
    def _():
        carry_ref[...] = jnp.zeros_like(carry_ref)

    if after_moe:
        x1_ref, y0_ref, y1_ref, info_ref = lead
        x = (x1_ref[...] + info_ref[:, 2:3] * _rows_to_wide(y0_ref)
             + info_ref[:, 3:4] * _rows_to_wide(y1_ref))
        refs[22][...] = x
    else:
        x = lead[0][...]
    h = _rms(x, g_ref[...]).astype(BF16)
    qk = _dot(h, wqk_ref[...])
    hm = hm_ref[...]
    w = FOX_WIDTH
    q_ref[...] = (_head_norm(qk[:, :w], hm, qg_ref[...]) * (HEAD_DIM ** -0.5 * LOG2E)).astype(BF16)
    k_ref[...] = _head_norm(qk[:, w:], hm, kg_ref[...]).astype(BF16)
    vt_ref[...] = _dot_nt(wvt_ref[...], h).astype(BF16)
    zx = _dot(h, wzx_ref[...])
    z_ref[...] = zx[:, :SSD_INNER].astype(BF16)
    xbc_ref[...] = zx[:, SSD_INNER:]
    fd = _dot(h, wfd_ref[...]) + fdb_ref[...]
    t = _softplus_parts(fd)
    log_f = jnp.minimum(fd, 0.0) - t
    dt = jnp.maximum(fd, 0.0) + t
    c = _dot_exact_lhs(tril_ref[...], log_f, 2) + carry_ref[...]
    carry_ref[...] = c[-1:, :]
    lane = lax.broadcasted_iota(jnp.int32, fd.shape, 1)
    fd_ref[...] = jnp.where(lane < FOX_HEADS, c, dt)
    qb = qone_ref[...]
    kb = kone_ref[...]
    for j, part in enumerate(_split_bf16(c * LOG2E, 3)):
        qb = qb + _dot(part, pq_ref[j])
        kb = kb + _dot(part, pk_ref[j])
    qb_ref[...] = qb.astype(BF16)
    kb_ref[...] = kb.astype(BF16)


def _even_in_proj(stream, seq, g, wqk, wvt, wzx, wfd, qg, kg, fdb, hm, tril, pq, pk, qone, kone):
    after_moe = isinstance(stream, tuple)
    t_rows = stream[0].shape[0] if after_moe else stream.shape[0]
    tm = min(ROW_TILE, seq)
    n = t_rows // tm
    row = lambda w: pl.BlockSpec((tm, w), lambda i: (i, 0))
    full = lambda a: pl.BlockSpec(a.shape, lambda i: (0,) * a.ndim)
    consts = (g, wqk, wvt, wzx, wfd, qg, kg, fdb, hm, tril, pq, pk, qone, kone)
    lead, lead_specs = (stream,), [row(D_MODEL)]
    out_specs = [row(FOX_WIDTH)] * 4 + [pl.BlockSpec((FOX_WIDTH, tm), lambda i: (0, i)),
                                        row(SSD_INNER), row(SSD_CONV_DIM), row(LANES)]
    out_shape = ([jax.ShapeDtypeStruct((t_rows, FOX_WIDTH), BF16)] * 4
                 + [jax.ShapeDtypeStruct((FOX_WIDTH, t_rows), BF16),
                    jax.ShapeDtypeStruct((t_rows, SSD_INNER), BF16),
                    jax.ShapeDtypeStruct((t_rows, SSD_CONV_DIM), F32),
                    jax.ShapeDtypeStruct((t_rows, LANES), F32)])
    if after_moe:
        x1, y, info = stream
        ysp = lambda kk: pl.BlockSpec((None, tm * TILE_SUBLANES, LANES), lambda i: (kk, i, 0))
        lead, lead_specs = (x1, y, y, info), [row(D_MODEL), ysp(0), ysp(1), row(LANES)]
        out_specs = out_specs + [row(D_MODEL)]
        out_shape = out_shape + [jax.ShapeDtypeStruct((t_rows, D_MODEL), F32)]
    return pl.pallas_call(
        functools.partial(_even_in_kernel, seq // tm, after_moe),
        grid=(n,),
        in_specs=lead_specs + [full(a) for a in consts],
        out_specs=out_specs,
        out_shape=out_shape,
        scratch_shapes=[pltpu.VMEM((1, LANES), F32)],
        compiler_params=_cparams(("arbitrary",)),
        name="even_in_proj",
    )(*lead, *consts)


def _fox_kernel(qi_ref, ki_ref, q_ref, qb_ref, k_ref, kb_ref, vt_ref, o_ref, qs_ref, m_ref, l_ref, acc_ref):
    step_idx = pl.program_id(2)
    qi = qi_ref[step_idx]
    ki = ki_ref[step_idx]
    tq = q_ref.shape[0]
    tk = k_ref.shape[0]
    heads = range(2 * FOX_PAIRS_PER_STEP)
    lanes_of = lambda h: slice((h // 2) * LANES, (h // 2 + 1) * LANES)

    @pl.when(ki == 0)
    def _():
        m_ref[...] = jnp.full_like(m_ref, NEG_BIG)
        l_ref[...] = jnp.zeros_like(l_ref)
        acc_ref[...] = jnp.zeros_like(acc_ref)
        lane = lax.broadcasted_iota(jnp.int32, (tq, LANES), 1)
        zero = jnp.zeros((tq, LANES), q_ref.dtype)
        for h in heads:
            hh = h % 2
            qh = jnp.where((lane >= hh * HEAD_DIM) & (lane < (hh + 1) * HEAD_DIM), q_ref[:, lanes_of(h)], zero)
            bh = jnp.where((lane >= hh * BIAS_LANES) & (lane < (hh + 1) * BIAS_LANES), qb_ref[:, lanes_of(h)], zero)
            qs_ref[h] = jnp.concatenate([qh, bh], axis=1)

    def step(masked):
        if masked:
            keep = (lax.broadcasted_iota(jnp.int32, (tk, tq), 0)
                    <= lax.broadcasted_iota(jnp.int32, (tk, tq), 1))
        ka = [jnp.concatenate([k_ref[:, lanes_of(2 * p)], kb_ref[:, lanes_of(2 * p)]], axis=1)
              for p in range(FOX_PAIRS_PER_STEP)]
        scores = [_dot_nt(ka[h // 2], qs_ref[h]) for h in heads]
        if masked:
            scores = [jnp.where(keep, s, NEG_BIG) for s in scores]
        m_prev = [m_ref[h] for h in heads]
        m_new = [jnp.maximum(m_prev[h], jnp.max(scores[h], axis=0, keepdims=True)) for h in heads]
        probs = [jnp.exp2(scores[h] - m_new[h]) for h in heads]
        alpha = [jnp.exp2(m_prev[h] - m_new[h]) for h in heads]
        pv = [_dot(vt_ref[lanes_of(h), :], probs[h].astype(BF16)) for h in heads]
        for h in heads:
            l_ref[h] = alpha[h] * l_ref[h] + jnp.sum(probs[h], axis=0, keepdims=True)
            acc_ref[h] = alpha[h] * acc_ref[h] + pv[h]
            m_ref[h] = m_new[h]

    @pl.when(ki < qi)
    def _():
        step(False)

    @pl.when(ki == qi)
    def _():
        step(True)
        for p in range(FOX_PAIRS_PER_STEP):
            o0 = acc_ref[2 * p] / l_ref[2 * p]
            o1 = acc_ref[2 * p + 1] / l_ref[2 * p + 1]
            o_t = jnp.concatenate([o0[:HEAD_DIM], o1[HEAD_DIM:]], axis=0)
            o_ref[:, lanes_of(2 * p)] = o_t.T.astype(o_ref.dtype)


def _fox_attention(q, qb, k, kb, vt, batch, seq):
    t = min(ATTN_TILE, seq)
    n = seq // t
    w = FOX_PAIRS_PER_STEP * LANES
    groups = FOX_WIDTH // w
    tri = [(i, j) for i in range(n) for j in range(i + 1)]
    qi_tab = jnp.asarray([i for i, _ in tri], jnp.int32)
    ki_tab = jnp.asarray([j for _, j in tri], jnp.int32)
    qspec = pl.BlockSpec((t, w), lambda bi, g, s, qi, ki: (bi * n + qi[s], g))
    kspec = pl.BlockSpec((t, w), lambda bi, g, s, qi, ki: (bi * n + ki[s], g))
    vspec = pl.BlockSpec((w, t), lambda bi, g, s, qi, ki: (g, bi * n + ki[s]))
    heads = 2 * FOX_PAIRS_PER_STEP
    grid_spec = pltpu.PrefetchScalarGridSpec(
        num_scalar_prefetch=2,
        grid=(batch, groups, len(tri)),
        in_specs=[qspec, qspec, kspec, kspec, vspec],
        out_specs=qspec,
        scratch_shapes=[pltpu.VMEM((heads, t, 2 * LANES), BF16), pltpu.VMEM((heads, 1, t), F32),
                        pltpu.VMEM((heads, 1, t), F32), pltpu.VMEM((heads, LANES, t), F32)],
    )
    return pl.pallas_call(
        _fox_kernel,
        grid_spec=grid_spec,
        out_shape=jax.ShapeDtypeStruct((batch * seq, FOX_WIDTH), BF16),
        compiler_params=_cparams(("arbitrary",) * 3),
        name="fox_attention",
    )(qi_tab, ki_tab, q, qb, k, kb, vt)


def _ssd_kernel(xbc_ref, prev_ref, fd_ref, dtt_ref, z_ref, cw_ref, cb_ref, alog_ref, alogc_ref,
                dsk_ref, nrm_ref, tri_ref, y_ref, state_ref):
    c_idx = pl.program_id(1)
    L = SSD_CHUNK
    P = HEAD_DIM
    N = SSD_STATE

    @pl.when(c_idx == 0)
    def _():
        state_ref[...] = jnp.zeros_like(state_ref)

    cur = xbc_ref[...]
    prev = jnp.where(c_idx > 0, prev_ref[...], 0.0)
    ext = jnp.concatenate([prev, cur], axis=0)
    cw = cw_ref[...]
    conv = cb_ref[...] + cw[SSD_CONV - 1:SSD_CONV, :] * cur
    for kk in range(SSD_CONV - 1):
        sh = SSD_CONV - 1 - kk
        conv = conv + cw[kk:kk + 1, :] * ext[8 - sh:8 - sh + L, :]
    xc = _silu(conv)
    xs = xc[:, :SSD_INNER]
    bm = xc[:, SSD_INNER:SSD_INNER + SSD_GROUPS * N]
    cm = xc[:, SSD_INNER + SSD_GROUPS * N:]

    dt = fd_ref[:, FOX_HEADS:FOX_HEADS + SSD_HEADS]
    a_row = -jnp.exp(alog_ref[...])
    a_col = -jnp.exp(alogc_ref[...])
    tri = tri_ref[...]
    cum = _dot_exact_lhs(tri, dt * a_row, 2)
    da_t = _split_bf16(dtt_ref[...] * a_col, 2)
    cum_t = _dot_nt(da_t[0], tri) + _dot_nt(da_t[1], tri)
    row_i = lax.broadcasted_iota(jnp.int32, (L, L), 0)
    col_i = lax.broadcasted_iota(jnp.int32, (L, L), 1)
    causal = row_i >= col_i

    groups = range(SSD_GROUPS)
    heads = range(SSD_HEADS)
    grp = lambda h: h // (SSD_HEADS // SSD_GROUPS)
    bm_g = [bm[:, g * N:(g + 1) * N] for g in groups]
    cm_g = [cm[:, g * N:(g + 1) * N] for g in groups]
    cb_g = [_dot_nt(cm_g[g].astype(BF16), bm_g[g].astype(BF16)) for g in groups]
    bm_t = [bm_g[g].T for g in groups]
    cum_c = [cum[:, h:h + 1] for h in heads]
    cum_r = [cum_t[h:h + 1, :] for h in heads]
    last = [cum_t[h:h + 1, L - 1:L] for h in heads]
    seg = [jnp.where(causal, jnp.exp(jnp.minimum(cum_c[h] - cum_r[h], 0.0)), 0.0) for h in heads]
    xdt = [(xs[:, h * P:(h + 1) * P] * dt[:, h:h + 1]).astype(BF16) for h in heads]
    h_prev = [state_ref[h] for h in heads]
    y_diag = [_dot((cb_g[grp(h)] * seg[h]).astype(BF16), xdt[h]) for h in heads]
    y_off = [_dot((cm_g[grp(h)] * jnp.exp(cum_c[h])).astype(BF16), h_prev[h].astype(BF16)) for h in heads]
    st = [_dot((bm_t[grp(h)] * jnp.exp(last[h] - cum_r[h])).astype(BF16), xdt[h]) for h in heads]
    for h in heads:
        state_ref[h] = jnp.exp(last[h]) * h_prev[h] + st[h]
    y = jnp.concatenate([y_diag[h] + y_off[h] for h in heads], axis=1) + xs * dsk_ref[...]
    y = y * _silu(z_ref[...].astype(F32))
    gw = SSD_INNER // SSD_GROUPS
    outs = []
    for g in range(SSD_GROUPS):
        yg = y[:, g * gw:(g + 1) * gw]
        outs.append(yg * lax.rsqrt(jnp.mean(yg * yg, axis=-1, keepdims=True) + RMS_EPS))
    y_ref[...] = (jnp.concatenate(outs, axis=1) * nrm_ref[...]).astype(y_ref.dtype)


def _ssd(xbc, fd, dtt, z, conv_w, conv_b, a_log, d_skip_c, ssd_norm, tri, batch, seq):
    L = SSD_CHUNK
    nc = seq // L
    t_rows = batch * seq
    row = lambda w: pl.BlockSpec((L, w), lambda b, c: (b * nc + c, 0))
    full = lambda a: pl.BlockSpec(a.shape, lambda b, c: (0,) * a.ndim)
    prev = pl.BlockSpec((8, SSD_CONV_DIM), lambda b, c: (jnp.maximum((b * nc + c) * (L // 8) - 1, 0), 0))
    return pl.pallas_call(
        _ssd_kernel,
        grid=(batch, nc),
        in_specs=[row(SSD_CONV_DIM), prev, row(LANES),
                  pl.BlockSpec((None, SSD_HEADS, L), lambda b, c: (b, 0, c)),
                  row(SSD_INNER), full(conv_w), full(conv_b), full(a_log[None, :]),
                  full(a_log[:, None]), full(d_skip_c), full(ssd_norm), full(tri)],
        out_specs=row(SSD_INNER),
        out_shape=jax.ShapeDtypeStruct((t_rows, SSD_INNER), BF16),
        scratch_shapes=[pltpu.VMEM((SSD_HEADS, SSD_STATE, HEAD_DIM), F32)],
        compiler_params=_cparams(("arbitrary", "arbitrary")),
        name="ssd_scan",
    )(xbc, xbc, fd, dtt, z, conv_w, conv_b, a_log[None, :], a_log[:, None], d_skip_c, ssd_norm, tri)


def _swiglu_hidden_chunk(h, w13_ref, act_ref, c, tf):
    f = w13_ref.shape[1] // 2
    off = c * tf if isinstance(c, int) else pl.multiple_of(c * tf, tf)
    gate = _dot(h, w13_ref[:, pl.ds(off, tf)])
    up = _dot(h, w13_ref[:, pl.ds(f + off, tf)])
    act_ref[:, pl.ds(off, tf)] = (_silu(gate) * up).astype(BF16)


def _even_out_kernel(o_ref, y_ref, x_ref, wo_ref, g_ref, w13_ref, w2_ref, out_ref, act_ref):
    w = FOX_WIDTH
    x1 = x_ref[...] + _dot(o_ref[...], wo_ref[:w, :]) + _dot(y_ref[...], wo_ref[w:, :])
    h = _rms(x1, g_ref[...]).astype(BF16)

    for c in range(w2_ref.shape[0] // FFN_CHUNK):
        _swiglu_hidden_chunk(h, w13_ref, act_ref, c, FFN_CHUNK)
    out_ref[...] = x1 + _dot(act_ref[...], w2_ref[...])


def _even_out_ffn(o_fox, y_ssd, x2d, wo, g, w13, w2):
    t_rows = x2d.shape[0]
    tm = min(ROW_TILE, t_rows)
    row = lambda w: pl.BlockSpec((tm, w), lambda i: (i, 0))
    full = lambda a: pl.BlockSpec(a.shape, lambda i: (0,) * a.ndim, pipeline_mode=pl.Buffered(1))
    return pl.pallas_call(
        _even_out_kernel,
        grid=(t_rows // tm,),
        in_specs=[row(FOX_WIDTH), row(SSD_INNER), row(D_MODEL), full(wo), full(g), full(w13), full(w2)],
        out_specs=row(D_MODEL),
        out_shape=jax.ShapeDtypeStruct((t_rows, D_MODEL), F32),
        scratch_shapes=[pltpu.VMEM((tm, w2.shape[0]), BF16)],
        compiler_params=_cparams(("arbitrary",)),
        name="even_out_ffn",
    )(o_fox, y_ssd, x2d, wo, g, w13, w2)


def _rope(x, cos, sin_signed):
    lane = lax.broadcasted_iota(jnp.int32, x.shape, 1)
    first = (lane % HEAD_DIM) < (HEAD_DIM // 2)
    rot = jnp.where(first, pltpu.roll(x, LANES - HEAD_DIM // 2, 1), pltpu.roll(x, HEAD_DIM // 2, 1))
    return x * cos + rot * sin_signed


def _emit_dilated_chunk(val, c, width, refs, slab_ref):
    tm = val.shape[0]
    slab_ref[c * tm:(c + 1) * tm, :] = val
    for (_, d), ref in zip(DIL_PATTERNS, refs):
        if d == 1:
            ref[:, c * LANES:(c + 1) * LANES] = val.astype(ref.dtype)
            continue
        n = tm // d
        for r in range(d):
            piece = slab_ref[pl.ds(c * tm + r, n, stride=d), :]
            ref[:, r * width + c * LANES:r * width + (c + 1) * LANES] = piece.astype(ref.dtype)


def _read_dilated_view(ref, d, tm, slab_ref):
    if d == 1:
        return ref[...].astype(F32)
    width = ref.shape[1] // d
    chunks = width // LANES
    n = tm // d
    for r in range(d):
        for c in range(chunks):
            piece = ref[:, r * width + c * LANES:r * width + (c + 1) * LANES].astype(F32)
            slab_ref[pl.ds(c * tm + r, n, stride=d), :] = piece
    return jnp.concatenate([slab_ref[c * tm:(c + 1) * tm, :] for c in range(chunks)], axis=1)


def _odd_qkv_kernel(x_ref, g_ref, w_ref, qg_ref, kg_ref, hm_ref, cos_ref, sin_ref, *rest):
    n_pat = len(DIL_PATTERNS)
    q_refs, k_refs, v_refs = rest[:n_pat], rest[n_pat:2 * n_pat], rest[2 * n_pat:3 * n_pat]
    slabs = rest[3 * n_pat:]
    h = _rms(x_ref[...], g_ref[...]).astype(BF16)
    expand = hm_ref[...]
    w = DIL_WIDTH
    cos = cos_ref[...]
    sin = sin_ref[...]

    def normed_rotated(col, gain_ref, scale, refs, slab_ref):
        x = _dot(h, w_ref[:, col:col + w])
        sums = _dot_nt((x * x).astype(BF16), expand) * (1.0 / HEAD_DIM)
        inv = lax.rsqrt(_dot_exact_rhs(sums, expand, 2) + RMS_EPS)
        for c in range(w // LANES):
            sl = slice(c * LANES, (c + 1) * LANES)
            xc = x[:, sl] * inv[:, sl] * gain_ref[:, sl]
            _emit_dilated_chunk(_rope(xc, cos, sin) * scale, c, w, refs, slab_ref)

    normed_rotated(0, qg_ref, HEAD_DIM ** -0.5 * LOG2E, q_refs, slabs[0])
    normed_rotated(w, kg_ref, 1.0, k_refs, slabs[1])
    v = _dot(h, w_ref[:, 2 * w:])
    for c in range(w // LANES):
        _emit_dilated_chunk(v[:, c * LANES:(c + 1) * LANES], c, w, v_refs, slabs[2])


def _odd_qkv(x2d, seq, g, w, qg, kg, hm, cos, sin):
    t_rows = x2d.shape[0]
    tm = min(ROW_TILE, seq)
    per_seq = seq // tm
    row = lambda rows, wd: pl.BlockSpec((rows, wd), lambda i: (i, 0))
    full = lambda a: pl.BlockSpec(a.shape, lambda i: (0,) * a.ndim)
    tab = pl.BlockSpec((tm, LANES), lambda i: (i % per_seq, 0))
    views = [(tm // d, t_rows // d, d * DIL_WIDTH) for (_, d) in DIL_PATTERNS]
    outs = pl.pallas_call(
        _odd_qkv_kernel,
        grid=(t_rows // tm,),
        in_specs=[row(tm, D_MODEL), full(g), full(w), full(qg), full(kg), full(hm), tab, tab],
        out_specs=[row(r, wd) for (r, _, wd) in views] * 3,
        out_shape=[jax.ShapeDtypeStruct((n, wd), BF16) for (_, n, wd) in views] * 3,
        scratch_shapes=[pltpu.VMEM((tm * (DIL_WIDTH // LANES), LANES), F32)] * 3,
        compiler_params=_cparams(("arbitrary",)),
        name="odd_qkv",
    )(x2d, g, w, qg, kg, hm, cos, sin)
    n_pat = len(DIL_PATTERNS)
    return outs[:n_pat], outs[n_pat:2 * n_pat], outs[2 * n_pat:]


def _dilated_kernel(q_ref, kp_ref, kc_ref, vp_ref, vc_ref, band_ref, o_ref, lse_ref):
    j = pl.program_id(2)
    blk = DIL_BLOCK
    n_sub = q_ref.shape[0] // blk
    key_row = lax.broadcasted_iota(jnp.int32, (2 * blk, blk), 0)
    band = band_ref[...]
    bias = [jnp.where((j == 0) & (key_row < blk), NEG_BIG, band)] + [band] * (n_sub - 1)
    lane = lax.broadcasted_iota(jnp.int32, (blk, LANES), 1)
    ones_rows = jnp.ones((DEN_ROWS, 2 * blk), BF16)
    pairs = range(DIL_HEADS // 2)
    heads = range(DIL_HEADS)
    subs = range(n_sub)
    lanes_of = lambda pr: slice(pr * LANES, (pr + 1) * LANES)

    def keys_of(prev_ref, cur_ref, u, pr):
        if u == 0:
            return jnp.concatenate([prev_ref[:, lanes_of(pr)], cur_ref[:blk, lanes_of(pr)]], axis=0)
        return cur_ref[(u - 1) * blk:(u + 1) * blk, lanes_of(pr)]

    kk = [[keys_of(kp_ref, kc_ref, u, pr) for pr in pairs] for u in subs]
    vt = [[jnp.concatenate([keys_of(vp_ref, vc_ref, u, pr).astype(F32).T.astype(BF16), ones_rows], axis=0)
           for pr in pairs] for u in subs]
    zero = jnp.zeros((blk, LANES), q_ref.dtype)
    qm = [[jnp.where((lane >= (h % 2) * HEAD_DIM) & (lane < (h % 2 + 1) * HEAD_DIM),
                     q_ref[u * blk:(u + 1) * blk, lanes_of(h // 2)], zero) for h in heads] for u in subs]
    s = [[_dot_nt(kk[u][h // 2], qm[u][h]) + bias[u] for h in heads] for u in subs]
    m = [[jnp.max(s[u][h], axis=0, keepdims=True) for h in heads] for u in subs]
    e = [[jnp.exp2(s[u][h] - m[u][h]).astype(BF16) for h in heads] for u in subs]
    ov = [[_dot(vt[u][h // 2], e[u][h]) for h in heads] for u in subs]
    for u in subs:
        den = [ov[u][h][LANES:LANES + 1, :] for h in heads]
        for pr in pairs:
            o_t = jnp.concatenate([(ov[u][2 * pr][:LANES] / den[2 * pr])[:HEAD_DIM],
                                   (ov[u][2 * pr + 1][:LANES] / den[2 * pr + 1])[HEAD_DIM:]], axis=0)
            o_ref[u * blk:(u + 1) * blk, lanes_of(pr)] = o_t.T.astype(o_ref.dtype)
        lse_t = jnp.concatenate([m[u][h] + jnp.log2(den[h]) for h in heads]
                                + [jnp.zeros((LANES - DIL_HEADS, blk), F32)], axis=0)
        lse_ref[u * blk:(u + 1) * blk, :] = lse_t.T


def _dilated_branch(q, k, v, band, dilation, batch):
    d = dilation
    w = DIL_WIDTH
    n_sub = min(DIL_SUB, q.shape[0] // batch // DIL_BLOCK)
    rows = n_sub * DIL_BLOCK
    nb = q.shape[0] // batch // rows
    cur = pl.BlockSpec((rows, w), lambda bi, r, j: (bi * nb + j, r))
    prev = pl.BlockSpec((DIL_BLOCK, w), lambda bi, r, j: ((bi * nb + jnp.maximum(j, 1)) * n_sub - 1, r))
    lse_spec = pl.BlockSpec((rows, LANES), lambda bi, r, j: (bi * nb + j, r))
    return pl.pallas_call(
        _dilated_kernel,
        grid=(batch, d, nb),
        in_specs=[cur, prev, cur, prev, cur, pl.BlockSpec(band.shape, lambda bi, r, j: (0, 0))],
        out_specs=[cur, lse_spec],
        out_shape=[jax.ShapeDtypeStruct(q.shape, BF16),
                   jax.ShapeDtypeStruct((q.shape[0], d * LANES), F32)],
        compiler_params=_cparams(("arbitrary",) * 3),
        name=f"dilated_attn_d{d}",
    )(q, k, k, v, v, band)


def _odd_out_kernel(o1_ref, o2_ref, o3_ref, l1_ref, l2_ref, l3_ref, x_ref, wo_ref, g_ref,
                    rw_ref, ex_ref, tril_ref, x1_ref, h_ref, info_ref, cnt_ref, carry_ref, *slabs):
    i = pl.program_id(0)
    tm = x_ref.shape[0]

    @pl.when(i == 0)
    def _():
        carry_ref[...] = jnp.zeros_like(carry_ref)

    n_pat = len(DIL_PATTERNS)
    dils = [d for (_, d) in DIL_PATTERNS]
    l1, l2, l3 = [_read_dilated_view(r, d, tm, s)
                  for r, d, s in zip((l1_ref, l2_ref, l3_ref), dils, slabs[:n_pat])]
    mx = jnp.maximum(jnp.maximum(l1, l2), l3)
    a1, a2, a3 = jnp.exp2(l1 - mx), jnp.exp2(l2 - mx), jnp.exp2(l3 - mx)
    inv = 1.0 / (a1 + a2 + a3)
    ex = ex_ref[...]
    o = None
    for a, o_ref, d, s in zip((a1, a2, a3), (o1_ref, o2_ref, o3_ref), dils, slabs[n_pat:]):
        term = _dot_exact_rhs(a * inv, ex, 1) * _read_dilated_view(o_ref, d, tm, s)
        o = term if o is None else o + term
    x1 = x_ref[...] + _dot(o.astype(BF16), wo_ref[...])
    x1_ref[...] = x1
    h = _rms(x1, g_ref[...])
    _wide_to_rows(h_ref, h)

    h_hi, h_lo = _split_bf16(h, 2)
    w_hi, w_lo = _split_bf16(rw_ref[...], 2)
    logits = _dot(h_hi, w_hi) + (_dot(h_lo, w_hi) + _dot(h_hi, w_lo))
    lane = lax.broadcasted_iota(jnp.int32, logits.shape, 1)
    logits = jnp.where(lane < N_EXPERTS, logits, -jnp.inf)
    m1 = jnp.max(logits, axis=-1, keepdims=True)
    i1 = jnp.min(jnp.where(logits == m1, lane, LANES), axis=-1, keepdims=True)
    rest = jnp.where(lane == i1, -jnp.inf, logits)
    m2 = jnp.max(rest, axis=-1, keepdims=True)
    i2 = jnp.min(jnp.where(rest == m2, lane, LANES), axis=-1, keepdims=True)
    e2 = jnp.exp(m2 - m1)
    g1 = 1.0 / (1.0 + e2)
    g2 = e2 * g1
    hot1 = lane == i1
    hot2 = lane == i2
    onehot = jnp.where(hot1 | hot2, 1.0, 0.0).astype(BF16)
    before = _dot(tril_ref[...], onehot) + carry_ref[...]
    r1 = jnp.sum(jnp.where(hot1, before, 0.0), axis=-1, keepdims=True)
    r2 = jnp.sum(jnp.where(hot2, before, 0.0), axis=-1, keepdims=True)
    total = before[-1:, :] + onehot[-1:, :].astype(F32)
    carry_ref[...] = total
    cnt_ref[...] = jnp.broadcast_to(total, cnt_ref.shape)
    info = jnp.where(lane == 0, i1.astype(F32), 0.0)
    info = jnp.where(lane == 1, i2.astype(F32), info)
    info = jnp.where(lane == 2, g1, info)
    info = jnp.where(lane == 3, g2, info)
    info = jnp.where(lane == 4, r1, info)
    info = jnp.where(lane == 5, r2, info)
    info_ref[...] = info


def _odd_out_router(o1, o2, o3, l1, l2, l3, x2d, wo, g, rw, ex, tril_strict):
    t_rows = x2d.shape[0]
    tm = min(ROW_TILE, t_rows)
    n = t_rows // tm
    row = lambda w: pl.BlockSpec((tm, w), lambda i: (i, 0))
    full = lambda a: pl.BlockSpec(a.shape, lambda i: (0,) * a.ndim)
    view = lambda w: [pl.BlockSpec((tm // d, d * w), lambda i: (i, 0)) for (_, d) in DIL_PATTERNS]
    return pl.pallas_call(
        _odd_out_kernel,
        grid=(n,),
        in_specs=view(DIL_WIDTH) + view(LANES) + [row(D_MODEL), full(wo), full(g), full(rw),
                                                  full(ex), full(tril_strict)],
        out_specs=[row(D_MODEL), pl.BlockSpec((tm * TILE_SUBLANES, LANES), lambda i: (i, 0)), row(LANES),
                   pl.BlockSpec((8, LANES), lambda i: (0, 0))],
        out_shape=[jax.ShapeDtypeStruct((t_rows, D_MODEL), F32),
                   jax.ShapeDtypeStruct((t_rows * TILE_SUBLANES, LANES), F32),
                   jax.ShapeDtypeStruct((t_rows, LANES), F32), jax.ShapeDtypeStruct((8, LANES), F32)],
        scratch_shapes=[pltpu.VMEM((1, LANES), F32)] + [pltpu.VMEM((tm, LANES), F32)] * len(DIL_PATTERNS)
        + [pltpu.VMEM((tm * (DIL_WIDTH // LANES), LANES), F32)] * len(DIL_PATTERNS),
        compiler_params=_cparams(("arbitrary",)),
        name="odd_out_router",
    )(o1, o2, o3, l1, l2, l3, x2d, wo, g, rw, ex, tril_strict)


def _rows_to_wide(ref):
    n = ref.shape[0] // TILE_SUBLANES
    return jnp.concatenate([ref[pl.ds(c, n, stride=TILE_SUBLANES), :] for c in range(TILE_SUBLANES)], axis=1)


def _wide_to_rows(ref, val):
    n = ref.shape[0] // TILE_SUBLANES
    for c in range(TILE_SUBLANES):
        ref[pl.ds(c, n, stride=TILE_SUBLANES), :] = val[:, c * LANES:(c + 1) * LANES]


def _moe_kernel(te_ref, nu_ref, src_cur_ref, src_nxt_ref, dst_prv_ref, dst_cur_ref, h_hbm, w13_ref, w2_ref,
                y_hbm, xa, xb, ya, yb, act_ref, gsem, ssem):
    i = pl.program_id(0)
    n_used = nu_ref[0]
    ts = TILE_SUBLANES
    gm = xa.shape[0] // ts
    n_chunks = w2_ref.shape[0] // MOE_CHUNK
    per_chunk = gm // n_chunks
    head_rows = gm - n_chunks * per_chunk

    def tile_of(ref, row):
        return ref.at[pl.ds(pl.multiple_of(row * ts, ts), ts)]

    def gather_row(src_ref, r, xdst, sem):
        pltpu.make_async_copy(tile_of(h_hbm, src_ref[0, 0, r]), tile_of(xdst, r), sem).start()

    def scatter_row(dst_ref, r, ysrc, sem):
        pltpu.make_async_copy(tile_of(ysrc, r), tile_of(y_hbm, dst_ref[0, 0, r]), sem).start()

    def wait_rows(buf, sem):
        pltpu.make_async_copy(h_hbm.at[pl.ds(0, gm * ts)], buf, sem).wait()

    @pl.when(i == 0)
    def _():
        yb[...] = jnp.zeros_like(yb)
        out_half = y_hbm.shape[0] // ts // 2
        spare = [y_hbm.at[pl.ds((k * out_half + (out_half - N_EXPERTS * gm) + e * gm) * ts, gm * ts)]
                 for k in range(2) for e in range(N_EXPERTS)]
        for blk in spare:
            pltpu.make_async_copy(yb, blk, ssem.at[1]).start()
        for blk in spare:
            pltpu.make_async_copy(yb, blk, ssem.at[1]).wait()

        def body(r, carry):
            gather_row(src_cur_ref, r, xa, gsem.at[0])
            return carry
        lax.fori_loop(0, gm, body, 0)

    def tile(x_cur, x_nxt, y_cur, y_prv, p):
        wait_rows(x_cur, gsem.at[p])

        @pl.when(i >= 1)
        def _():
            wait_rows(y_cur, ssem.at[p])

        def issue(r):
            gather_row(src_nxt_ref, r, x_nxt, gsem.at[1 - p])
            scatter_row(dst_prv_ref, r, y_prv, ssem.at[1 - p])

        for r in range(head_rows):
            issue(n_chunks * per_chunk + r)
        x = _rows_to_wide(x_cur).astype(BF16)

        def body(c, carry):
            for u in range(MOE_UNROLL):
                _swiglu_hidden_chunk(x, w13_ref, act_ref, c * MOE_UNROLL + u, MOE_CHUNK)
            for j in range(per_chunk * MOE_UNROLL):
                issue(c * (per_chunk * MOE_UNROLL) + j)
            return carry

        lax.fori_loop(0, n_chunks // MOE_UNROLL, body, 0)
        _wide_to_rows(y_cur, _dot(act_ref[...], w2_ref[...]))

        @pl.when(i == n_used - 1)
        def _():
            def body(r, carry):
                scatter_row(dst_cur_ref, r, y_cur, ssem.at[p])
                return carry
            lax.fori_loop(0, gm, body, 0)
            wait_rows(y_prv, ssem.at[1 - p])
            wait_rows(y_cur, ssem.at[p])
            wait_rows(x_nxt, gsem.at[1 - p])

    @pl.when((i < n_used) & (i % 2 == 0))
    def _():
        tile(xa, xb, ya, yb, 0)

    @pl.when((i < n_used) & (i % 2 == 1))
    def _():
        tile(xb, xa, yb, ya, 1)


def _moe_ffn(h_tiles, w13, w2, layer, tile_expert, n_used, src, dst, out_rows):
    nt = tile_expert.shape[0]
    gm = src.shape[-1]
    tab = lambda f: pl.BlockSpec((1, 1, gm), lambda i, te, nu: (f(i), 0, 0), memory_space=pltpu.SMEM)
    grid_spec = pltpu.PrefetchScalarGridSpec(
        num_scalar_prefetch=2,
        grid=(nt,),
        in_specs=[
            tab(lambda i: i),
            tab(lambda i: jnp.minimum(i + 1, nt - 1)),
            tab(lambda i: i),
            tab(lambda i: i + 1),
            pl.BlockSpec(memory_space=pl.ANY),
            pl.BlockSpec((None, None) + w13.shape[2:], lambda i, te, nu: (layer, te[i], 0, 0),
                         pipeline_mode=pl.Buffered(1)),
            pl.BlockSpec((None, None) + w2.shape[2:], lambda i, te, nu: (layer, te[i], 0, 0),
                         pipeline_mode=pl.Buffered(1)),
        ],
        out_specs=pl.BlockSpec(memory_space=pl.ANY),
        scratch_shapes=[pltpu.VMEM((gm * TILE_SUBLANES, LANES), F32)] * 4
        + [pltpu.VMEM((gm, w2.shape[2]), BF16), pltpu.SemaphoreType.DMA((2,)), pltpu.SemaphoreType.DMA((2,))],
    )
    return pl.pallas_call(
        _moe_kernel,
        grid_spec=grid_spec,
        out_shape=jax.ShapeDtypeStruct((out_rows * TILE_SUBLANES, LANES), F32),
        compiler_params=_cparams(("arbitrary",)),
        name="moe_ffn",
    )(tile_expert, n_used, src, src, dst, dst, h_tiles, w13, w2)


def _combine_kernel(x_ref, y0_ref, y1_ref, info_ref, o_ref):
    o_ref[...] = (x_ref[...] + info_ref[:, 2:3] * _rows_to_wide(y0_ref)
                  + info_ref[:, 3:4] * _rows_to_wide(y1_ref))


def _combine(x1, y, info):
    t_rows = x1.shape[0]
    tm = min(ROW_TILE, t_rows)
    row = lambda w: pl.BlockSpec((tm, w), lambda i: (i, 0))
    ysp = lambda kk: pl.BlockSpec((None, tm * TILE_SUBLANES, LANES), lambda i: (kk, i, 0))
    return pl.pallas_call(
        _combine_kernel,
        grid=(t_rows // tm,),
        in_specs=[row(D_MODEL), ysp(0), ysp(1), row(LANES)],
        out_specs=row(D_MODEL),
        out_shape=jax.ShapeDtypeStruct((t_rows, D_MODEL), F32),
        compiler_params=_cparams(("arbitrary",)),
        name="moe_combine",
    )(x1, y, y, info)


def _head_mean_matrix(width):
    idx = np.arange(width) // HEAD_DIM
    return jnp.asarray((idx[:, None] == idx[None, :]).astype(np.float32) / HEAD_DIM, dtype=BF16)


def _tril(n, strict):
    r = np.arange(n)
    m = (r[None, :] < r[:, None]) if strict else (r[None, :] <= r[:, None])
    return jnp.asarray(m.astype(np.float32), dtype=BF16)


def _rope_tables(seq):
    half = HEAD_DIM // 2
    inv_freq = ROPE_THETA ** (-jnp.arange(half, dtype=F32) / half)
    ang = jnp.arange(seq, dtype=F32)[:, None] * inv_freq[None, :]
    cos = jnp.cos(ang)
    sin = jnp.sin(ang)
    cos_t = jnp.concatenate([cos, cos, cos, cos], axis=1)
    sin_t = jnp.concatenate([-sin, sin, -sin, sin], axis=1)
    return cos_t, sin_t


def _band_bias():
    qi = np.arange(DIL_BLOCK)[None, :]
    ki = np.arange(-DIL_BLOCK, DIL_BLOCK)[:, None]
    rel = qi - ki
    return jnp.asarray(np.where((rel >= 0) & (rel <= DIL_BLOCK), 0.0, NEG_BIG).astype(np.float32))


def _head_expand():
    m = np.zeros((LANES, DIL_WIDTH), np.float32)
    for h in range(DIL_HEADS):
        m[h, h * HEAD_DIM:(h + 1) * HEAD_DIM] = 1.0
    return jnp.asarray(m, dtype=BF16)


def _bias_placement():
    pq = np.zeros((BIAS_PARTS, LANES, FOX_WIDTH), np.float32)
    pk = np.zeros((BIAS_PARTS, LANES, FOX_WIDTH), np.float32)
    qone = np.zeros((1, FOX_WIDTH), np.float32)
    kone = np.zeros((1, FOX_WIDTH), np.float32)
    for head in range(FOX_HEADS):
        base = (head // 2) * LANES + (head % 2) * BIAS_LANES
        for j in range(BIAS_PARTS):
            pq[j, head, base + j] = 1.0
            pk[j, head, base + BIAS_PARTS + j] = -1.0
            qone[0, base + BIAS_PARTS + j] = 1.0
            kone[0, base + j] = 1.0
    return (jnp.asarray(pq, dtype=BF16), jnp.asarray(pk, dtype=BF16), jnp.asarray(qone), jnp.asarray(kone))


def _even_layer(x2d, batch, seq, norm1, w_in, fox_qn, fox_kn, fox_fbias, conv_w, conv_b, dt_bias, a_log,
                d_skip, ssd_norm, w_out, norm2, ffn_w13, ffn_w2):
    w = FOX_WIDTH
    o_z = 3 * w + FOX_HEADS
    o_x = o_z + SSD_INNER
    o_dt = o_x + SSD_CONV_DIM
    wqk = w_in[:, :2 * w].astype(BF16)
    wvt = w_in[:, 2 * w:3 * w].T.astype(BF16)
    wzx = w_in[:, o_z:o_dt].astype(BF16)
    pad = LANES - FOX_HEADS - SSD_HEADS
    wfd = jnp.concatenate([w_in[:, 3 * w:o_z], w_in[:, o_dt:], jnp.zeros((D_MODEL, pad), F32)], axis=1).astype(BF16)
    fdb = jnp.concatenate([fox_fbias, dt_bias, jnp.zeros((pad,), F32)])[None, :]
    tm = min(ROW_TILE, seq)
    outs = _even_in_proj(
        x2d, seq, norm1[None, :], wqk, wvt, wzx, wfd, fox_qn.reshape(1, w), fox_kn.reshape(1, w), fdb,
        _head_mean_matrix(w), _tril(tm, False), *_bias_placement())
    q, qb, k, kb, vt, z, xbc, fd = outs[:8]
    if isinstance(x2d, tuple):
        x2d = outs[8]

    o_fox = _fox_attention(q, qb, k, kb, vt, batch, seq)

    dtt = jnp.transpose(fd[:, FOX_HEADS:FOX_HEADS + SSD_HEADS].reshape(batch, seq, SSD_HEADS), (0, 2, 1))
    d_skip_c = jnp.repeat(d_skip, HEAD_DIM)[None, :]
    y = _ssd(xbc, fd, dtt, z, conv_w, conv_b[None, :], a_log, d_skip_c, ssd_norm[None, :],
             _tril(SSD_CHUNK, False), batch, seq)

    return _even_out_ffn(o_fox, y, x2d, w_out.astype(BF16), norm2[None, :],
                         ffn_w13.astype(BF16), ffn_w2.astype(BF16))


def _moe_tables(info, counts, t_rows, gm):
    e1 = info[:, 0].astype(jnp.int32)
    e2 = info[:, 1].astype(jnp.int32)
    r1 = info[:, 4].astype(jnp.int32)
    r2 = info[:, 5].astype(jnp.int32)
    cnt = counts[0, :N_EXPERTS].astype(jnp.int32)
    tiles = (cnt + gm - 1) // gm
    tile_end = jnp.cumsum(tiles)
    offs = (tile_end - tiles) * gm
    nt = (2 * t_rows) // gm + N_EXPERTS
    tile_idx = jnp.arange(nt, dtype=jnp.int32)
    tile_expert = jnp.minimum(jnp.sum((tile_idx[:, None] >= tile_end[None, :]).astype(jnp.int32), axis=1),
                              N_EXPERTS - 1)
    n_used = tile_end[-1:].astype(jnp.int32)
    slots = jnp.concatenate([offs[e1] + r1, offs[e2] + r2])
    codes = jnp.full((nt * gm,), -1, jnp.int32).at[slots].set(
        jnp.arange(2 * t_rows, dtype=jnp.int32), unique_indices=True)
    out_half = t_rows + N_EXPERTS * gm
    second = codes >= t_rows
    src = jnp.where(codes < 0, 0, jnp.where(second, codes - t_rows, codes))
    spare = t_rows + jnp.repeat(tile_expert, gm) * gm + jnp.tile(jnp.arange(gm, dtype=jnp.int32), nt)
    dst = jnp.where(codes < 0, spare, jnp.where(second, codes - t_rows + out_half, codes))
    dst = jnp.concatenate([t_rows + jnp.arange(gm, dtype=jnp.int32), dst])
    return tile_expert, n_used, src.reshape(nt, 1, gm), dst.reshape(nt + 1, 1, gm), out_half


def _odd_layer(x2d, batch, seq, norm1, w_qkv, qn, kn, w_out, norm2, router, moe_w13, moe_w2, layer):
    t_rows = batch * seq
    cos, sin = _rope_tables(seq)
    qs, ks, vs = _odd_qkv(x2d, seq, norm1[None, :], w_qkv.astype(BF16), qn.reshape(1, DIL_WIDTH),
                          kn.reshape(1, DIL_WIDTH), _head_expand(), cos, sin)
    band = _band_bias()
    outs, lses = zip(*[_dilated_branch(q, k, v, band, d, batch)
                       for q, k, v, (_, d) in zip(qs, ks, vs, DIL_PATTERNS)])
    rw = jnp.concatenate([router, jnp.zeros((D_MODEL, LANES - N_EXPERTS), F32)], axis=1)
    tm = min(ROW_TILE, t_rows)
    x1, h, info, counts = _odd_out_router(*outs, *lses, x2d, w_out.astype(BF16), norm2[None, :], rw,
                                          _head_expand(), _tril(tm, True))
    gm = min(MOE_TILE, t_rows)
    tile_expert, n_used, src, dst, out_half = _moe_tables(info, counts, t_rows, gm)
    y = _moe_ffn(h, moe_w13, moe_w2, layer, tile_expert, n_used, src, dst, 2 * out_half)
    return x1, y.reshape(2, out_half * TILE_SUBLANES, LANES), info


def kernel(x, e_norm1, e_w_in, e_fox_qn, e_fox_kn, e_fox_fbias, e_conv_w, e_conv_b, e_dt_bias, e_a_log, e_d_skip, e_ssd_norm, e_w_out, e_norm2, e_ffn_w13, e_ffn_w2, o_norm1, o_w_qkv, o_qn, o_kn, o_w_out, o_norm2, o_router, o_moe_w13, o_moe_w2):
    batch, seq, _ = x.shape
    depth = e_norm1.shape[0] + o_norm1.shape[0]
    x2d = x.reshape(batch * seq, D_MODEL)
    moe_w13 = o_moe_w13.astype(BF16)
    moe_w2 = o_moe_w2.astype(BF16)
    for i in range(depth):
        j = i // 2
        if i % 2 == 0:
            x2d = _even_layer(x2d, batch, seq, e_norm1[j], e_w_in[j], e_fox_qn[j], e_fox_kn[j], e_fox_fbias[j],
                              e_conv_w[j], e_conv_b[j], e_dt_bias[j], e_a_log[j], e_d_skip[j], e_ssd_norm[j],
                              e_w_out[j], e_norm2[j], e_ffn_w13[j], e_ffn_w2[j])
        else:
            x2d = _odd_layer(x2d, batch, seq, o_norm1[j], o_w_qkv[j], o_qn[j], o_kn[j], o_w_out[j], o_norm2[j],
                             o_router[j], moe_w13, moe_w2, j)
            if i == depth - 1:
                x2d = _combine(*x2d)
    return x2d.reshape(batch, seq, D_MODEL)
```
